```python
import jax, jax.numpy as jnp
from jax import lax
import numpy as np

D_MODEL = 1024
BATCH = 16
SEQ = 2048
DEPTH = 1

PLE_DIM = 256
D_MIX = D_MODEL
MLSTM_HEADS = 4
MLSTM_HEAD_DIM = 128
MLSTM_WIDTH = MLSTM_HEADS * MLSTM_HEAD_DIM
HGRN_HEADS = 4
HGRN_HEAD_DIM = 128
HGRN_WIDTH = HGRN_HEADS * HGRN_HEAD_DIM
CONV_WIDTH = 4
MLSTM_CHUNK = 128
HGRN_CHUNK = 32
MLSTM_COLS = 4 * MLSTM_WIDTH + 2 * MLSTM_HEADS
HGRN_COLS = 4 * HGRN_WIDTH
IN_COLS = MLSTM_COLS + HGRN_COLS
N_GROUPS = 4
EXPERTS_PER_GROUP = 8
N_EXPERTS = N_GROUPS * EXPERTS_PER_GROUP
TOP_K = 2
D_EXPERT = 512
MOE_BLOCK = 128
EPS = 1e-6

kernel_name = "hybrid_mlstm_hgrn2_hmoe_block"


def rms_norm(u, g):
    u32 = u.astype(jnp.float32)
    y = u32 * lax.rsqrt(jnp.mean(u32 * u32, axis=-1, keepdims=True) + EPS) * g.astype(jnp.float32)
    return y.astype(u.dtype)


def to_chunks(u, n_heads, chunk):
    b, t, _ = u.shape
    return u.reshape(b, t // chunk, chunk, n_heads, -1).transpose(1, 0, 3, 2, 4)


def gates_to_chunks(u, chunk):
    b, t, h = u.shape
    return u.reshape(b, t // chunk, chunk, h).transpose(1, 0, 3, 2)


def from_chunks(u):
    nc, b, h, l, d = u.shape
    return u.transpose(1, 0, 3, 2, 4).reshape(b, nc * l, h, d)


def causal_depthwise_conv(u, w):
    k, c = w.shape
    return lax.conv_general_dilated(u, w[:, None, :].astype(u.dtype), window_strides=(1,),
                                    padding=[(k - 1, 0)], dimension_numbers=("NWC", "WIO", "NWC"),
                                    feature_group_count=c)


def mlstm_mixer(qk_pre, v_in, o_pre, gate_pre, conv_w, g_norm):
    b, t, _ = v_in.shape
    qk = jax.nn.silu(causal_depthwise_conv(qk_pre, conv_w)).astype(jnp.float32)
    q = qk[..., :MLSTM_WIDTH]
    k = qk[..., MLSTM_WIDTH:] * (MLSTM_HEAD_DIM ** -0.5)
    v = v_in.astype(jnp.float32)
    ig = gate_pre[..., :MLSTM_HEADS]
    lf = jax.nn.log_sigmoid(gate_pre[..., MLSTM_HEADS:])
    qc, kc, vc = (to_chunks(u, MLSTM_HEADS, MLSTM_CHUNK) for u in (q, k, v))
    igc, lfc = gates_to_chunks(ig, MLSTM_CHUNK), gates_to_chunks(lf, MLSTM_CHUNK)
    mask = jnp.tril(jnp.ones((MLSTM_CHUNK, MLSTM_CHUNK), bool))

    def step(carry, xs):
        c_st, n_st, m_st = carry
        qq, kk, vv, ii, ff = xs
        bcum = jnp.cumsum(ff, axis=-1)
        log_d = jnp.where(mask, bcum[..., :, None] - bcum[..., None, :] + ii[..., None, :], -jnp.inf)
        inter = bcum + m_st[..., None]
        m_t = jnp.maximum(inter, jnp.max(log_d, axis=-1))
        s = jnp.einsum("bhtd,bhsd->bhts", qq, kk) * jnp.exp(log_d - m_t[..., None])
        w_inter = jnp.exp(inter - m_t)
        num = jnp.einsum("bhts,bhse->bhte", s, vv) + w_inter[..., None] * jnp.einsum("bhtd,bhde->bhte", qq, c_st)
        den = jnp.sum(s, axis=-1) + w_inter * jnp.einsum("bhtd,bhd->bht", qq, n_st)
        h = num / jnp.maximum(jnp.abs(den), jnp.exp(-m_t))[..., None]
        b_last = bcum[..., -1]
        w_log = b_last[..., None] - bcum + ii
        m_new = jnp.maximum(b_last + m_st, jnp.max(w_log, axis=-1))
        w = jnp.exp(w_log - m_new[..., None])
        decay = jnp.exp(b_last + m_st - m_new)
        c_new = decay[..., None, None] * c_st + jnp.einsum("bhs,bhsd,bhse->bhde", w, kk, vv)
        n_new = decay[..., None] * n_st + jnp.einsum("bhs,bhsd->bhd", w, kk)
        return (c_new, n_new, m_new), h

    init = (jnp.zeros((b, MLSTM_HEADS, MLSTM_HEAD_DIM, MLSTM_HEAD_DIM), jnp.float32),
            jnp.zeros((b, MLSTM_HEADS, MLSTM_HEAD_DIM), jnp.float32),
            jnp.zeros((b, MLSTM_HEADS), jnp.float32))
    _, hc = lax.scan(step, init, (qc, kc, vc, igc, lfc))
    h = from_chunks(hc)
    h = h * jax.nn.sigmoid(o_pre.astype(jnp.float32)).reshape(h.shape)
    h = h * lax.rsqrt(jnp.mean(h * h, axis=-1, keepdims=True) + EPS)
    return (h.reshape(b, t, MLSTM_WIDTH) * g_norm.astype(jnp.float32)).astype(v_in.dtype)


def hgrn2_mixer(q_pre, f_pre, i_in, g_pre, lb, g_norm):
    b, t, _ = q_pre.shape
    q = jax.nn.silu(q_pre.astype(jnp.float32))
    f32p = f_pre.astype(jnp.float32)
    f = lb + (1.0 - lb) * jax.nn.sigmoid(f32p)
    k = (1.0 - lb) * jax.nn.sigmoid(-f32p)
    lf = jnp.log(f)
    v = i_in.astype(jnp.float32)
    qc, kc, vc, lfc = (to_chunks(u, HGRN_HEADS, HGRN_CHUNK) for u in (q, k, v, lf))
    mask = jnp.tril(jnp.ones((HGRN_CHUNK, HGRN_CHUNK), bool))[:, :, None]

    def step(s_st, xs):
        qq, kk, vv, ll = xs
        bcum = jnp.cumsum(ll, axis=2)
        diff = bcum[:, :, :, None, :] - bcum[:, :, None, :, :]
        decay = jnp.exp(jnp.where(mask, diff, -jnp.inf))
        a = jnp.einsum("bhtd,bhsd,bhtsd->bhts", qq, kk, decay)
        o = jnp.einsum("bhts,bhse->bhte", a, vv) + jnp.einsum("bhtd,bhde->bhte", qq * jnp.exp(bcum), s_st)
        b_last = bcum[:, :, -1:, :]
        s_new = jnp.exp(b_last[:, :, 0, :])[..., None] * s_st + \
            jnp.einsum("bhsd,bhse->bhde", kk * jnp.exp(b_last - bcum), vv)
        return s_new, o

    s0 = jnp.zeros((b, HGRN_HEADS, HGRN_HEAD_DIM, HGRN_HEAD_DIM), jnp.float32)
    _, oc = lax.scan(step, s0, (qc, kc, vc, lfc))
    o = from_chunks(oc)
    o = o * lax.rsqrt(jnp.mean(o * o, axis=-1, keepdims=True) + EPS) * g_norm.astype(jnp.float32)
    o = o.reshape(b, t, HGRN_WIDTH) * jax.nn.silu(g_pre.astype(jnp.float32))
    return o.astype(q_pre.dtype)


def hier_moe(h, w_rg, b_rg, w_re, b_re, w_gate, w_up, w_down):
    b, t, d = h.shape
    n_tok = b * t
    hf = h.reshape(n_tok, d)
    h32 = hf.astype(jnp.float32)
    g_logits = h32 @ w_rg.astype(jnp.float32) + b_rg.astype(jnp.float32)
    g_prob = jax.nn.softmax(g_logits, axis=-1)
    g_val, g_sel = lax.top_k(g_prob, 1)
    e_logits = (h32 @ w_re.astype(jnp.float32) + b_re.astype(jnp.float32)).reshape(n_tok, N_GROUPS, EXPERTS_PER_GROUP)
    e_in_group = jnp.take_along_axis(e_logits, g_sel[:, :, None], axis=1)[:, 0]
    top_v, top_i = lax.top_k(e_in_group, TOP_K)
    comb = jax.nn.softmax(top_v, axis=-1) * g_val
    expert_id = g_sel * EXPERTS_PER_GROUP + top_i

    n_asg = n_tok * TOP_K
    flat_e = expert_id.reshape(n_asg).astype(jnp.int32)
    flat_w = comb.reshape(n_asg)
    flat_tok = jnp.arange(n_asg, dtype=jnp.int32) // TOP_K
    order = jnp.argsort(flat_e)
    se, st, sw = flat_e[order], flat_tok[order], flat_w[order]
    counts = jnp.bincount(flat_e, length=N_EXPERTS).astype(jnp.int32)
    padded = (counts + MOE_BLOCK - 1) // MOE_BLOCK * MOE_BLOCK
    start = jnp.cumsum(counts) - counts
    pend = jnp.cumsum(padded)
    pstart = pend - padded
    dest = pstart[se] + jnp.arange(n_asg, dtype=jnp.int32) - start[se]
    n_blocks = -(-n_asg // MOE_BLOCK) + N_EXPERTS
    n_rows = n_blocks * MOE_BLOCK
    row_tok = jnp.full((n_rows,), n_tok, jnp.int32).at[dest].set(st)
    row_w = jnp.zeros((n_rows,), jnp.float32).at[dest].set(sw)
    block_e = jnp.minimum(jnp.searchsorted(pend, jnp.arange(n_blocks, dtype=jnp.int32) * MOE_BLOCK, side="right"),
                          N_EXPERTS - 1)
    x_pad = jnp.concatenate([hf, jnp.zeros((1, d), hf.dtype)], axis=0)
    xb = x_pad[row_tok].reshape(n_blocks, MOE_BLOCK, d)

    def expert_block(args):
        xblk, e = args
        return (jax.nn.silu(xblk @ w_gate[e]) * (xblk @ w_up[e])) @ w_down[e]

    yb = lax.map(expert_block, (xb, block_e)).reshape(n_rows, d)
    y = jax.ops.segment_sum(yb * row_w[:, None].astype(yb.dtype), row_tok, num_segments=n_tok + 1)[:n_tok]
    return y.reshape(b, t, d).astype(h.dtype)


def setup_inputs(seed: int = 0) -> dict:
    key = jax.random.key(seed)
    ks = jax.random.split(key, 24)
    f32 = jnp.float32

    def nrm(k, shape, scale):
        return jax.random.normal(k, shape, f32) * scale

    b_i = nrm(ks[4], (DEPTH, MLSTM_HEADS), 0.1)
    b_f = jnp.linspace(3.0, 6.0, MLSTM_HEADS, dtype=f32)[None, :] + nrm(ks[5], (DEPTH, MLSTM_HEADS), 0.1)
    return {
        "x": nrm(ks[0], (BATCH, SEQ, D_MODEL), 1.0),
        "p": nrm(ks[1], (DEPTH, BATCH, SEQ, PLE_DIM), 1.0),
        "g_mix": 1.0 + nrm(ks[2], (DEPTH, D_MODEL), 0.02),
        "w_in": nrm(ks[3], (DEPTH, D_MODEL, IN_COLS), D_MODEL ** -0.5),
        "b_mgate": jnp.concatenate([b_i, b_f], axis=-1),
        "conv_qk": nrm(ks[6], (DEPTH, CONV_WIDTH, 2 * MLSTM_WIDTH), CONV_WIDTH ** -0.5),
        "g_mlstm": 1.0 + nrm(ks[7], (DEPTH, MLSTM_WIDTH), 0.02),
        "hg_lb": nrm(ks[8], (DEPTH + 1, HGRN_WIDTH), 0.5),
        "g_hgrn": 1.0 + nrm(ks[9], (DEPTH, HGRN_HEAD_DIM), 0.02),
        "w_out": nrm(ks[10], (DEPTH, D_MIX, D_MODEL), D_MIX ** -0.5),
        "g_ffn": 1.0 + nrm(ks[11], (DEPTH, D_MODEL), 0.02),
        "w_rg": nrm(ks[12], (DEPTH, D_MODEL, N_GROUPS), D_MODEL ** -0.5),
        "b_rg": nrm(ks[13], (DEPTH, N_GROUPS), 0.01),
        "w_re": nrm(ks[14], (DEPTH, D_MODEL, N_EXPERTS), D_MODEL ** -0.5),
        "b_re": nrm(ks[15], (DEPTH, N_EXPERTS), 0.01),
        "w_e_gate": nrm(ks[16], (DEPTH, N_EXPERTS, D_MODEL, D_EXPERT), D_MODEL ** -0.5),
        "w_e_up": nrm(ks[17], (DEPTH, N_EXPERTS, D_MODEL, D_EXPERT), D_MODEL ** -0.5),
        "w_e_down": nrm(ks[18], (DEPTH, N_EXPERTS, D_EXPERT, D_MODEL), D_EXPERT ** -0.5),
        "g_pl": 1.0 + nrm(ks[19], (DEPTH, D_MODEL), 0.02),
        "w_pl_gate": nrm(ks[20], (DEPTH, D_MODEL, D_MODEL), D_MODEL ** -0.5),
        "w_pl_proj": nrm(ks[21], (DEPTH, PLE_DIM, D_MODEL), PLE_DIM ** -0.5),
        "g_final": 1.0 + nrm(ks[22], (D_MODEL,), 0.02),
    }


def reference(x, p, g_mix, w_in, b_mgate, conv_qk, g_mlstm, hg_lb, g_hgrn, w_out, g_ffn,
              w_rg, b_rg, w_re, b_re, w_e_gate, w_e_up, w_e_down, g_pl, w_pl_gate, w_pl_proj, g_final):
    lower_bounds = jnp.cumsum(jax.nn.softmax(hg_lb.astype(jnp.float32), axis=0), axis=0)
    w = MLSTM_WIDTH
    for i in range(DEPTH):
        h = rms_norm(x, g_mix[i])
        z = h @ w_in[i]
        gate_pre = z[..., 4 * w:MLSTM_COLS].astype(jnp.float32) + b_mgate[i].astype(jnp.float32)
        y_m = mlstm_mixer(z[..., :2 * w], z[..., 2 * w:3 * w], z[..., 3 * w:4 * w], gate_pre,
                          conv_qk[i], g_mlstm[i])
        o = MLSTM_COLS
        y_h = hgrn2_mixer(z[..., o:o + HGRN_WIDTH], z[..., o + HGRN_WIDTH:o + 2 * HGRN_WIDTH],
                          z[..., o + 2 * HGRN_WIDTH:o + 3 * HGRN_WIDTH], z[..., o + 3 * HGRN_WIDTH:o + 4 * HGRN_WIDTH],
                          lower_bounds[i], g_hgrn[i])
        x = x + jnp.concatenate([y_m, y_h], axis=-1) @ w_out[i]
        x = x + hier_moe(rms_norm(x, g_ffn[i]), w_rg[i], b_rg[i], w_re[i], b_re[i],
                         w_e_gate[i], w_e_up[i], w_e_down[i])
        x = x + jax.nn.sigmoid(rms_norm(x, g_pl[i]) @ w_pl_gate[i]) * (p[i] @ w_pl_proj[i])
    return rms_norm(x, g_final)
```

```python
import functools

import jax
import jax.numpy as jnp
from jax import lax
from jax.experimental import pallas as pl
from jax.experimental.pallas import tpu as pltpu

F32 = jnp.float32
BF16 = jnp.bfloat16
I32 = jnp.int32
U32 = jnp.uint32
EPS = 1e-6
HIGHEST = lax.Precision.HIGHEST

LANES = 128
D_MODEL = 1024
W_MIX = 512
N_HEADS = 4
HEAD_DIM = 128
N_GROUPS = 4
EXPERTS_PER_GROUP = 8
N_EXPERTS = N_GROUPS * EXPERTS_PER_GROUP
D_EXPERT = 512
PLE_DIM = 256
CONV_WIDTH = 4
CHUNK = 128
EXPERT_BLOCK = 256
VMEM_LIMIT = 56 * 1024 * 1024


def _dot(a, b):
    return jnp.dot(a, b, preferred_element_type=F32)


def _dot_nt(a, b):
    return lax.dot_general(a, b, (((1,), (1,)), ((), ())), preferred_element_type=F32)


def _dot_tn(a, b):
    return lax.dot_general(a, b, (((0,), (0,)), ((), ())), preferred_element_type=F32)


def _rms(u, g):
    return u * lax.rsqrt(jnp.mean(u * u, axis=-1, keepdims=True) + EPS) * g


def _silu(u):
    return u * jax.nn.sigmoid(u)


def _log_sigmoid(u):
    return jnp.minimum(u, 0.0) - jnp.log1p(jnp.exp(-jnp.abs(u)))


def _params(*sem):
    return pltpu.CompilerParams(dimension_semantics=sem, vmem_limit_bytes=VMEM_LIMIT)


def _in_proj_kernel(x_ref, g_ref, wm_ref, wh_ref, wg_ref, wgt_ref, bg_ref, bgt_ref,
                    zm_ref, zh_ref, gate_ref, gate_t_ref):
    h = _rms(x_ref[...], g_ref[...]).astype(BF16)
    step = 512
    for c0 in range(0, 4 * W_MIX, step):
        zm_ref[:, c0:c0 + step] = _dot(h, wm_ref[:, c0:c0 + step]).astype(BF16)
        zh_ref[:, c0:c0 + step] = _dot(h, wh_ref[:, c0:c0 + step]).astype(BF16)
    gate_ref[...] = _dot(h, wg_ref[...]) + bg_ref[...]
    gate_t_ref[...] = _dot_nt(wgt_ref[...], h) + bgt_ref[...]


def _in_proj(x2, g_mix, w_m, w_h, w_g, w_gt, b_g, b_gt, tm):
    n = x2.shape[0]
    const = lambda i: (0, 0)
    return pl.pallas_call(
        _in_proj_kernel,
        grid=(n // tm,),
        in_specs=[
            pl.BlockSpec((tm, D_MODEL), lambda i: (i, 0)),
            pl.BlockSpec((1, D_MODEL), const),
            pl.BlockSpec((D_MODEL, 4 * W_MIX), const),
            pl.BlockSpec((D_MODEL, 4 * W_MIX), const),
            pl.BlockSpec((D_MODEL, LANES), const),
            pl.BlockSpec((8, D_MODEL), const),
            pl.BlockSpec((1, LANES), const),
            pl.BlockSpec((8, 1), const),
        ],
        out_specs=[
            pl.BlockSpec((tm, 4 * W_MIX), lambda i: (i, 0)),
            pl.BlockSpec((tm, 4 * W_MIX), lambda i: (i, 0)),
            pl.BlockSpec((tm, LANES), lambda i: (i, 0)),
            pl.BlockSpec((8, tm), lambda i: (0, i)),
        ],
        out_shape=[
            jax.ShapeDtypeStruct((n, 4 * W_MIX), BF16),
            jax.ShapeDtypeStruct((n, 4 * W_MIX), BF16),
            jax.ShapeDtypeStruct((n, LANES), F32),
            jax.ShapeDtypeStruct((8, n), F32),
        ],
        compiler_params=_params("parallel"),
        name="in_proj",
    )(x2, g_mix, w_m, w_h, w_g, w_gt, b_g, b_gt)


def _mlstm_kernel(zm_ref, gate_ref, gate_t_ref, conv_ref, gn_ref, y_ref,
                  cbuf, c_ref, n_ref, m_ref):
    L = CHUNK

    @pl.when(pl.program_id(1) == 0)
    def _():
        cbuf[0:8, :] = jnp.zeros((8, 2 * W_MIX), F32)
        c_ref[...] = jnp.zeros_like(c_ref)
        n_ref[...] = jnp.zeros_like(n_ref)
        m_ref[...] = jnp.zeros_like(m_ref)

    cbuf[8:8 + L, :] = zm_ref[:, 0:2 * W_MIX].astype(F32)
    acc = cbuf[8:8 + L, :] * conv_ref[CONV_WIDTH - 1:CONV_WIDTH, :]
    for j in range(CONV_WIDTH - 1):
        acc = acc + cbuf[5 + j:5 + j + L, :] * conv_ref[j:j + 1, :]
    cbuf[0:8, :] = cbuf[L:L + 8, :]
    qk = _silu(acc)

    gate = gate_ref[...]
    gate_t = gate_t_ref[...]
    row = lax.broadcasted_iota(I32, (L, L), 0)
    col = lax.broadcasted_iota(I32, (L, L), 1)
    causal = col <= row
    tri = causal.astype(F32)
    bcum_c = jnp.dot(tri, _log_sigmoid(gate), precision=HIGHEST, preferred_element_type=F32)
    bcum_r = jnp.dot(_log_sigmoid(gate_t), (row <= col).astype(F32), precision=HIGHEST,
                     preferred_element_type=F32)

    for h in range(N_HEADS):
        hs = slice(h * HEAD_DIM, (h + 1) * HEAD_DIM)
        q = qk[:, h * HEAD_DIM:(h + 1) * HEAD_DIM]
        k = qk[:, W_MIX + h * HEAD_DIM:W_MIX + (h + 1) * HEAD_DIM] * (HEAD_DIM ** -0.5)
        v = zm_ref[:, 2 * W_MIX + h * HEAD_DIM:2 * W_MIX + (h + 1) * HEAD_DIM]
        o_pre = zm_ref[:, 3 * W_MIX + h * HEAD_DIM:3 * W_MIX + (h + 1) * HEAD_DIM].astype(F32)
        qb = q.astype(BF16)

        bc = bcum_c[:, N_HEADS + h:N_HEADS + h + 1]
        ic = gate[:, h:h + 1]
        br = bcum_r[N_HEADS + h:N_HEADS + h + 1, :]
        ir = gate_t[h:h + 1, :]
        m_prev = m_ref[h:h + 1, 0:1]
        b_last = bc[L - 1:L, :]

        log_d = jnp.where(causal, bc - br + ir, -jnp.inf)
        inter = bc + m_prev
        m_t = jnp.maximum(inter, jnp.max(log_d, axis=-1, keepdims=True))
        s = _dot_nt(qb, k.astype(BF16)) * jnp.exp(log_d - m_t)
        w_inter = jnp.exp(inter - m_t)
        num = _dot(s.astype(BF16), v) + w_inter * _dot(qb, c_ref[h].astype(BF16))
        den = jnp.sum(s, axis=-1, keepdims=True) + w_inter * jnp.sum(q * n_ref[h:h + 1, :], axis=-1, keepdims=True)
        hh = num * (1.0 / jnp.maximum(jnp.abs(den), jnp.exp(-m_t)))

        w_log = b_last - bc + ic
        m_new = jnp.maximum(b_last + m_prev, jnp.max(w_log, axis=0, keepdims=True))
        kw = k * jnp.exp(w_log - m_new)
        decay = jnp.exp(b_last + m_prev - m_new)
        c_ref[h] = decay * c_ref[h] + _dot_tn(kw.astype(BF16), v)
        n_ref[h:h + 1, :] = decay * n_ref[h:h + 1, :] + jnp.sum(kw, axis=0, keepdims=True)
        m_ref[h:h + 1, :] = jnp.broadcast_to(m_new, (1, LANES))

        hh = hh * jax.nn.sigmoid(o_pre)
        hh = hh * lax.rsqrt(jnp.mean(hh * hh, axis=-1, keepdims=True) + EPS) * gn_ref[:, hs]
        y_ref[:, hs] = hh.astype(BF16)


def _mlstm(zm, gate, gate_t, conv_w, g_norm, batch, n_chunks):
    n = zm.shape[0]
    rows = lambda b, c: (b * n_chunks + c, 0)
    const = lambda b, c: (0, 0)
    return pl.pallas_call(
        _mlstm_kernel,
        grid=(batch, n_chunks),
        in_specs=[
            pl.BlockSpec((CHUNK, 4 * W_MIX), rows),
            pl.BlockSpec((CHUNK, LANES), rows),
            pl.BlockSpec((8, CHUNK), lambda b, c: (0, b * n_chunks + c)),
            pl.BlockSpec((CONV_WIDTH, 2 * W_MIX), const),
            pl.BlockSpec((1, W_MIX), const),
        ],
        out_specs=pl.BlockSpec((CHUNK, W_MIX), rows),
        out_shape=jax.ShapeDtypeStruct((n, W_MIX), BF16),
        scratch_shapes=[
            pltpu.VMEM((CHUNK + 8, 2 * W_MIX), F32),
            pltpu.VMEM((N_HEADS, HEAD_DIM, HEAD_DIM), F32),
            pltpu.VMEM((8, HEAD_DIM), F32),
            pltpu.VMEM((8, LANES), F32),
        ],
        compiler_params=_params("parallel", "arbitrary"),
        name="mlstm",
    )(zm, gate, gate_t, conv_w, g_norm)


def _rows_bcast(ref, rows, span, hs):
    return jnp.concatenate([jnp.broadcast_to(ref[r:r + 1, hs], (span, HEAD_DIM)) for r in rows], axis=0)


def _hgrn_kernel(zh_ref, lb_ref, gn_ref, y_ref, st_ref, b_scr):
    L = CHUNK

    @pl.when(pl.program_id(1) == 0)
    def _():
        st_ref[...] = jnp.zeros_like(st_ref)

    lb = lb_ref[...]
    f_pre = zh_ref[:, W_MIX:2 * W_MIX].astype(F32)
    lf = jnp.log(lb + (1.0 - lb) * jax.nn.sigmoid(f_pre))
    k_all = (1.0 - lb) * jax.nn.sigmoid(-f_pre)
    row = lax.broadcasted_iota(I32, (L, L), 0)
    col = lax.broadcasted_iota(I32, (L, L), 1)
    b_scr[...] = jnp.dot((col <= row).astype(F32), lf, precision=HIGHEST, preferred_element_type=F32)

    for h in range(N_HEADS):
        hs = slice(h * HEAD_DIM, (h + 1) * HEAD_DIM)
        b = b_scr[:, hs]
        q = _silu(zh_ref[:, hs].astype(F32))
        k = k_all[:, hs]
        v = zh_ref[:, 2 * W_MIX + h * HEAD_DIM:2 * W_MIX + (h + 1) * HEAD_DIM]
        b_last = b_scr[L - 1:L, hs]

        o = _dot_nt((q * jnp.exp(b)).astype(BF16), st_ref[h].astype(BF16))

        a = jnp.zeros((L, L), F32)
        span = L // 2
        while span >= 16:
            mids = range(span, L, 2 * span)
            r = _rows_bcast(b_scr, [m for m in mids for _ in (0, 1)], span, hs)
            lower = (row & span) != 0
            e = jnp.exp(jnp.where(lower, b - r, r - b))
            qt = jnp.where(lower, q * e, 0.0).astype(BF16)
            kt = jnp.where(lower, 0.0, k * e).astype(BF16)
            p = _dot_nt(qt, kt)
            same = (row & -(2 * span)) == (col & -(2 * span))
            a = a + jnp.where(same, p, 0.0)
            span //= 2
        r = _rows_bcast(b_scr, range(0, L, 16), 16, hs)
        p = _dot_nt((q * jnp.exp(b - r)).astype(BF16), (k * jnp.exp(r - b)).astype(BF16))
        a = a + jnp.where(((row & -16) == (col & -16)) & (col <= row), p, 0.0)

        o = o + _dot(a.astype(BF16), v)

        ke = (k * jnp.exp(b_last - b)).astype(BF16)
        st_ref[h] = jnp.exp(b_last) * st_ref[h] + _dot_tn(v, ke)

        o = o * lax.rsqrt(jnp.mean(o * o, axis=-1, keepdims=True) + EPS) * gn_ref[...]
        o = o * _silu(zh_ref[:, 3 * W_MIX + h * HEAD_DIM:3 * W_MIX + (h + 1) * HEAD_DIM].astype(F32))
        y_ref[:, hs] = o.astype(BF16)


def _hgrn(zh, lb, g_norm, batch, n_chunks):
    n = zh.shape[0]
    rows = lambda b, c: (b * n_chunks + c, 0)
    const = lambda b, c: (0, 0)
    return pl.pallas_call(
        _hgrn_kernel,
        grid=(batch, n_chunks),
        in_specs=[
            pl.BlockSpec((CHUNK, 4 * W_MIX), rows),
            pl.BlockSpec((1, W_MIX), const),
            pl.BlockSpec((1, HEAD_DIM), const),
        ],
        out_specs=pl.BlockSpec((CHUNK, W_MIX), rows),
        out_shape=jax.ShapeDtypeStruct((n, W_MIX), BF16),
        scratch_shapes=[
            pltpu.VMEM((N_HEADS, HEAD_DIM, HEAD_DIM), F32),
            pltpu.VMEM((CHUNK, W_MIX), F32),
        ],
        compiler_params=_params("parallel", "arbitrary"),
        name="hgrn",
    )(zh, lb, g_norm)


def _pack_bf16_pair(lo, hi):
    lo_bits = pltpu.bitcast(lo.astype(BF16).astype(F32), U32)
    hi_bits = pltpu.bitcast(hi.astype(BF16).astype(F32), U32)
    return (hi_bits & jnp.uint32(0xFFFF0000)) | (lo_bits >> 16)


def _unpack_bf16_pair(w):
    lo = pltpu.bitcast(w << 16, F32).astype(BF16)
    hi = pltpu.bitcast(w & jnp.uint32(0xFFFF0000), F32).astype(BF16)
    return lo, hi


def _post_mix_kernel(x_ref, ym_ref, yh_ref, wom_ref, woh_ref, g_ref, wr_ref, br_ref,
                     x1_ref, hn_ref, route_ref, count_ref, run_ref):
    tm = x_ref.shape[0]

    @pl.when(pl.program_id(0) == 0)
    def _():
        run_ref[...] = jnp.zeros_like(run_ref)

    x1 = x_ref[...] + _dot(ym_ref[...], wom_ref[...]) + _dot(yh_ref[...], woh_ref[...])
    x1_ref[...] = x1
    hn = _rms(x1, g_ref[...])
    half = D_MODEL // 2
    hn_ref[...] = _pack_bf16_pair(hn[:, :half], hn[:, half:])

    logits = jnp.dot(hn, wr_ref[...], precision=HIGHEST, preferred_element_type=F32) + br_ref[...]
    lane = lax.broadcasted_iota(I32, (tm, LANES), 1)
    neg = -jnp.inf
    g_l = jnp.where(lane < N_GROUPS, logits, neg)
    g_max = jnp.max(g_l, axis=-1, keepdims=True)
    g_sel = jnp.min(jnp.where(g_l == g_max, lane, LANES), axis=-1, keepdims=True)
    g_val = 1.0 / jnp.sum(jnp.exp(g_l - g_max), axis=-1, keepdims=True)

    e_lane = lane - N_GROUPS
    in_group = (e_lane >= g_sel * EXPERTS_PER_GROUP) & (e_lane < (g_sel + 1) * EXPERTS_PER_GROUP)
    e_l = jnp.where(in_group, logits, neg)
    v1 = jnp.max(e_l, axis=-1, keepdims=True)
    i1 = jnp.min(jnp.where(e_l == v1, lane, LANES), axis=-1, keepdims=True)
    e_l2 = jnp.where(lane == i1, neg, e_l)
    v2 = jnp.max(e_l2, axis=-1, keepdims=True)
    i2 = jnp.min(jnp.where(e_l2 == v2, lane, LANES), axis=-1, keepdims=True)
    t = jnp.exp(v2 - v1)
    c1 = g_val / (1.0 + t)
    c2 = g_val * t / (1.0 + t)

    hot1 = lane == i1
    hot2 = lane == i2
    hot = (hot1 | hot2).astype(F32)
    r_i = lax.broadcasted_iota(I32, (tm, tm), 0)
    c_i = lax.broadcasted_iota(I32, (tm, tm), 1)
    before = _dot((c_i < r_i).astype(BF16), hot.astype(BF16)) + run_ref[0:1, :]
    rank1 = jnp.sum(jnp.where(hot1, before, 0.0), axis=-1, keepdims=True)
    rank2 = jnp.sum(jnp.where(hot2, before, 0.0), axis=-1, keepdims=True)
    run_ref[0:1, :] = run_ref[0:1, :] + jnp.sum(hot, axis=0, keepdims=True)
    count_ref[...] = jnp.broadcast_to(run_ref[0:1, :], count_ref.shape)

    out = jnp.where(lane == 0, (i1 - N_GROUPS).astype(F32), 0.0)
    out = jnp.where(lane == 1, (i2 - N_GROUPS).astype(F32), out)
    out = jnp.where(lane == 2, rank1, out)
    out = jnp.where(lane == 3, rank2, out)
    out = jnp.where(lane == 4, c1, out)
    out = jnp.where(lane == 5, c2, out)
    route_ref[...] = out


def _post_mix(x2, y_m, y_h, wo_m, wo_h, g_ffn, w_r, b_r, tm):
    n = x2.shape[0]
    const = lambda i: (0, 0)
    rows = lambda i: (i, 0)
    return pl.pallas_call(
        _post_mix_kernel,
        grid=(n // tm,),
        in_specs=[
            pl.BlockSpec((tm, D_MODEL), rows),
            pl.BlockSpec((tm, W_MIX), rows),
            pl.BlockSpec((tm, W_MIX), rows),
            pl.BlockSpec((W_MIX, D_MODEL), const),
            pl.BlockSpec((W_MIX, D_MODEL), const),
            pl.BlockSpec((1, D_MODEL), const),
            pl.BlockSpec((D_MODEL, LANES), const),
            pl.BlockSpec((1, LANES), const),
        ],
        out_specs=[
            pl.BlockSpec((tm, D_MODEL), rows),
            pl.BlockSpec((tm, D_MODEL // 2), rows),
            pl.BlockSpec((tm, LANES), rows),
            pl.BlockSpec((8, LANES), const),
        ],
        out_shape=[
            jax.ShapeDtypeStruct((n, D_MODEL), F32),
            jax.ShapeDtypeStruct((n, D_MODEL // 2), U32),
            jax.ShapeDtypeStruct((n, LANES), F32),
            jax.ShapeDtypeStruct((8, LANES), F32),
        ],
        scratch_shapes=[pltpu.VMEM((8, LANES), F32)],
        compiler_params=_params("arbitrary"),
        name="post_mix",
    )(x2, y_m, y_h, wo_m, wo_h, g_ffn, w_r, b_r)


def _dispatch_kernel(dest_ref, hn_ref, xb_init_ref, xb_ref, sem):
    del xb_init_ref
    ts = hn_ref.shape[0]

    def row_copy(j):
        return pltpu.make_async_copy(hn_ref.at[pl.ds(j >> 1, 1), :],
                                     xb_ref.at[pl.ds(dest_ref[0, 0, j], 1), :], sem.at[0])

    lax.fori_loop(0, 2 * ts, lambda j, c: (row_copy(j).start(), c)[1], 0)
    lax.fori_loop(0, 2 * ts, lambda j, c: (row_copy(j).wait(), c)[1], 0)


def _dispatch(dest3, hn, n_rows, ts):
    n = hn.shape[0]
    return pl.pallas_call(
        _dispatch_kernel,
        grid=(n // ts,),
        in_specs=[
            pl.BlockSpec((1, 1, 2 * ts), lambda i: (i, 0, 0), memory_space=pltpu.SMEM),
            pl.BlockSpec((ts, D_MODEL // 2), lambda i: (i, 0)),
            pl.BlockSpec(memory_space=pl.ANY),
        ],
        out_specs=pl.BlockSpec(memory_space=pl.ANY),
        out_shape=jax.ShapeDtypeStruct((n_rows, D_MODEL // 2), U32),
        scratch_shapes=[pltpu.SemaphoreType.DMA((1,))],
        input_output_aliases={2: 0},
        compiler_params=_params("arbitrary"),
        name="dispatch",
    )(dest3, hn, jnp.zeros((n_rows, D_MODEL // 2), U32))


def _experts_kernel(be_ref, nu_ref, xb_ref, wg_ref, wu_ref, wd_ref, yb_ref):
    i = pl.program_id(0)

    @pl.when(i < nu_ref[0])
    def _():
        half = D_MODEL // 2
        x_lo, x_hi = _unpack_bf16_pair(xb_ref[...])
        g = _dot(x_lo, wg_ref[0:half, :]) + _dot(x_hi, wg_ref[half:, :])
        u = _dot(x_lo, wu_ref[0:half, :]) + _dot(x_hi, wu_ref[half:, :])
        yb_ref[...] = _dot((_silu(g) * u).astype(BF16), wd_ref[...])

    @pl.when(i >= nu_ref[0])
    def _():
        yb_ref[...] = jnp.zeros_like(yb_ref)


def _experts(block_e, n_used, xb, w_gate, w_up, w_down):
    n_rows = xb.shape[0]
    n_blocks = n_rows // EXPERT_BLOCK
    xrow = lambda i, be, nu: (jnp.minimum(i, nu[0] - 1), 0)
    wsel = lambda i, be, nu: (be[i], 0, 0)
    grid_spec = pltpu.PrefetchScalarGridSpec(
        num_scalar_prefetch=2,
        grid=(n_blocks,),
        in_specs=[
            pl.BlockSpec((EXPERT_BLOCK, D_MODEL // 2), xrow),
            pl.BlockSpec((None, D_MODEL, D_EXPERT), wsel),
            pl.BlockSpec((None, D_MODEL, D_EXPERT), wsel),
            pl.BlockSpec((None, D_EXPERT, D_MODEL), wsel),
        ],
        out_specs=pl.BlockSpec((EXPERT_BLOCK, D_MODEL), lambda i, be, nu: (i, 0)),
    )
    return pl.pallas_call(
        _experts_kernel,
        grid_spec=grid_spec,
        out_shape=jax.ShapeDtypeStruct((n_rows, D_MODEL), F32),
        compiler_params=_params("arbitrary"),
        name="experts",
    )(block_e, n_used, xb, w_gate, w_up, w_down)


def _combine_kernel(dest_ref, dest_next_ref, x1_ref, p_ref, route_ref, gpl_ref, wplg_ref, wplp_ref, gfin_ref,
                    yb_ref, out_ref, gbuf, sem):
    tf = x1_ref.shape[0]
    i = pl.program_id(0)
    n_steps = pl.num_programs(0)
    slot = i & 1

    def row_copy(d_ref, s, j):
        return pltpu.make_async_copy(yb_ref.at[pl.ds(d_ref[0, 0, j], 1), :],
                                     gbuf.at[s, j & 1, pl.ds(j >> 1, 1), :], sem.at[s])

    @pl.when(i == 0)
    def _():
        lax.fori_loop(0, 2 * tf, lambda j, c: (row_copy(dest_ref, 0, j).start(), c)[1], 0)

    @pl.when(i + 1 < n_steps)
    def _():
        lax.fori_loop(0, 2 * tf, lambda j, c: (row_copy(dest_next_ref, 1 - slot, j).start(), c)[1], 0)

    lax.fori_loop(0, 2 * tf, lambda j, c: (row_copy(dest_ref, slot, j).wait(), c)[1], 0)

    route = route_ref[...]
    y = route[:, 4:5] * gbuf[slot, 0] + route[:, 5:6] * gbuf[slot, 1]
    x2 = x1_ref[...] + y
    gate = jax.nn.sigmoid(_dot(_rms(x2, gpl_ref[...]).astype(BF16), wplg_ref[...]))
    x3 = x2 + gate * _dot(p_ref[...].astype(BF16), wplp_ref[...])
    out_ref[...] = _rms(x3, gfin_ref[...])


def _combine(dest3, x1, p2, route, g_pl, w_plg, w_plp, g_final, yb, tf):
    n = x1.shape[0]
    n_steps = n // tf
    const = lambda i: (0, 0)
    rows = lambda i: (i, 0)
    return pl.pallas_call(
        _combine_kernel,
        grid=(n_steps,),
        in_specs=[
            pl.BlockSpec((1, 1, 2 * tf), lambda i: (i, 0, 0), memory_space=pltpu.SMEM),
            pl.BlockSpec((1, 1, 2 * tf), lambda i: (jnp.minimum(i + 1, n_steps - 1), 0, 0), memory_space=pltpu.SMEM),
            pl.BlockSpec((tf, D_MODEL), rows),
            pl.BlockSpec((tf, PLE_DIM), rows),
            pl.BlockSpec((tf, LANES), rows),
            pl.BlockSpec((1, D_MODEL), const),
            pl.BlockSpec((D_MODEL, D_MODEL), const),
            pl.BlockSpec((PLE_DIM, D_MODEL), const),
            pl.BlockSpec((1, D_MODEL), const),
            pl.BlockSpec(memory_space=pl.ANY),
        ],
        out_specs=pl.BlockSpec((tf, D_MODEL), rows),
        out_shape=jax.ShapeDtypeStruct((n, D_MODEL), F32),
        scratch_shapes=[pltpu.VMEM((2, 2, tf, D_MODEL), F32), pltpu.SemaphoreType.DMA((2,))],
        compiler_params=_params("arbitrary"),
        name="combine",
    )(dest3, dest3, x1, p2, route, g_pl, w_plg, w_plp, g_final, yb)


def _layer(x2, p2, batch, seq, g_mix, w_in, b_mgate, conv_qk, g_mlstm, lb, g_hgrn, w_out, g_ffn,
           w_rg, b_rg, w_re, b_re, w_e_gate, w_e_up, w_e_down, g_pl, w_pl_gate, w_pl_proj, g_out):
    n = x2.shape[0]
    n_chunks = seq // CHUNK
    m_cols = 4 * W_MIX
    n_gate = 2 * N_HEADS

    w_in_b = w_in.astype(BF16)
    w_m = w_in_b[:, :m_cols]
    w_gcols = w_in_b[:, m_cols:m_cols + n_gate]
    w_h = w_in_b[:, m_cols + n_gate:]
    w_g = jnp.pad(w_gcols, ((0, 0), (0, LANES - n_gate)))
    b_g = jnp.pad(b_mgate.astype(F32), (0, LANES - n_gate))[None, :]
    zm, zh, gate, gate_t = _in_proj(x2, g_mix[None, :], w_m, w_h, w_g, w_gcols.T, b_g,
                                    b_mgate.astype(F32)[:, None], tm=min(512, n))

    y_m = _mlstm(zm, gate, gate_t, conv_qk, g_mlstm[None, :], batch, n_chunks)
    y_h = _hgrn(zh, lb[None, :], g_hgrn[None, :], batch, n_chunks)

    w_out_b = w_out.astype(BF16)
    w_r = jnp.pad(jnp.concatenate([w_rg, w_re], axis=1), ((0, 0), (0, LANES - N_GROUPS - N_EXPERTS)))
    b_r = jnp.pad(jnp.concatenate([b_rg, b_re]), (0, LANES - N_GROUPS - N_EXPERTS))[None, :]
    x1, hn, route, counts = _post_mix(x2, y_m, y_h, w_out_b[:W_MIX], w_out_b[W_MIX:], g_ffn[None, :],
                                      w_r, b_r, tm=min(512, n))

    counts = counts[0, N_GROUPS:N_GROUPS + N_EXPERTS].astype(I32)
    padded = (counts + EXPERT_BLOCK - 1) // EXPERT_BLOCK * EXPERT_BLOCK
    pend = jnp.cumsum(padded)
    pstart = pend - padded
    n_blocks = (2 * n) // EXPERT_BLOCK + N_EXPERTS
    n_rows = n_blocks * EXPERT_BLOCK
    n_used = (pend[-1] // EXPERT_BLOCK).astype(I32)
    block_e = jnp.minimum(jnp.searchsorted(pend, jnp.arange(n_blocks, dtype=I32) * EXPERT_BLOCK, side="right"),
                          N_EXPERTS - 1).astype(I32)
    block_e = jnp.where(jnp.arange(n_blocks) < n_used, block_e, block_e[jnp.maximum(n_used - 1, 0)])
    expert_id = route[:, 0:2].astype(I32)
    dest = pstart[expert_id] + route[:, 2:4].astype(I32)

    ts = min(512, n)
    xb = _dispatch(dest.reshape(n // ts, 1, 2 * ts), hn, n_rows, ts)
    yb = _experts(block_e, n_used[None], xb, w_e_gate.astype(BF16), w_e_up.astype(BF16), w_e_down.astype(BF16))
    tf = min(256, n)
    return _combine(dest.reshape(n // tf, 1, 2 * tf), x1, p2, route, g_pl[None, :], w_pl_gate.astype(BF16),
                    w_pl_proj.astype(BF16), g_out[None, :], yb, tf)


def kernel(x, p, g_mix, w_in, b_mgate, conv_qk, g_mlstm, hg_lb, g_hgrn, w_out, g_ffn, w_rg, b_rg, w_re, b_re,
           w_e_gate, w_e_up, w_e_down, g_pl, w_pl_gate, w_pl_proj, g_final):
    batch, seq, d = x.shape
    depth = p.shape[0]
    assert depth == 1, "the fused final norm assumes a single layer"
    lower_bounds = jnp.cumsum(jax.nn.softmax(hg_lb.astype(F32), axis=0), axis=0)
    i = 0
    out = _layer(x.reshape(batch * seq, d), p[i].reshape(batch * seq, PLE_DIM), batch, seq,
                 g_mix[i], w_in[i], b_mgate[i], conv_qk[i], g_mlstm[i], lower_bounds[i], g_hgrn[i], w_out[i],
                 g_ffn[i], w_rg[i], b_rg[i], w_re[i], b_re[i], w_e_gate[i], w_e_up[i], w_e_down[i],
                 g_pl[i], w_pl_gate[i], w_pl_proj[i], g_final)
    return out.reshape(batch, seq, d)
```

```python
import functools

import jax
import jax.numpy as jnp
from jax import lax
from jax.experimental import pallas as pl
from jax.experimental.pallas import tpu as pltpu

F32 = jnp.float32
BF16 = jnp.bfloat16
I32 = jnp.int32
U32 = jnp.uint32
EPS = 1e-6
HIGHEST = lax.Precision.HIGHEST

LANES = 128
D_MODEL = 1024
W_MIX = 512
N_HEADS = 4
HEAD_DIM = 128
N_GROUPS = 4
EXPERTS_PER_GROUP = 8
N_EXPERTS = N_GROUPS * EXPERTS_PER_GROUP
D_EXPERT = 512
PLE_DIM = 256
CONV_WIDTH = 4
CHUNK = 128
EXPERT_BLOCK = 256
PACK_ROWS = D_MODEL // 2 // LANES
OUT_ROWS = D_MODEL // LANES
DMA_UNROLL = 8
VMEM_LIMIT = 56 * 1024 * 1024


def _dot(a, b):
    return jnp.dot(a, b, preferred_element_type=F32)


def _dot_nt(a, b):
    return lax.dot_general(a, b, (((1,), (1,)), ((), ())), preferred_element_type=F32)


def _dot_tn(a, b):
    return lax.dot_general(a, b, (((0,), (0,)), ((), ())), preferred_element_type=F32)


def _rms(u, g):
    return u * lax.rsqrt(jnp.mean(u * u, axis=-1, keepdims=True) + EPS) * g


def _silu(u):
    return u * jax.nn.sigmoid(u)


def _log_sigmoid(u):
    return jnp.minimum(u, 0.0) - jnp.log1p(jnp.exp(-jnp.abs(u)))


def _params(*sem):
    return pltpu.CompilerParams(dimension_semantics=sem, vmem_limit_bytes=VMEM_LIMIT)


def _in_proj_kernel(x_ref, g_ref, wm_ref, wh_ref, wg_ref, wgt_ref, bg_ref, bgt_ref,
                    zm_ref, zh_ref, gate_ref, gate_t_ref):
    h = _rms(x_ref[...], g_ref[...]).astype(BF16)
    step = 512
    for c0 in range(0, 4 * W_MIX, step):
        zm_ref[:, c0:c0 + step] = _dot(h, wm_ref[:, c0:c0 + step]).astype(BF16)
        zh_ref[:, c0:c0 + step] = _dot(h, wh_ref[:, c0:c0 + step]).astype(BF16)
    gate_ref[...] = _dot(h, wg_ref[...]) + bg_ref[...]
    gate_t_ref[...] = _dot_nt(wgt_ref[...], h) + bgt_ref[...]


def _in_proj(x2, g_mix, w_m, w_h, w_g, w_gt, b_g, b_gt, tm):
    n = x2.shape[0]
    const = lambda i: (0, 0)
    return pl.pallas_call(
        _in_proj_kernel,
        grid=(n // tm,),
        in_specs=[
            pl.BlockSpec((tm, D_MODEL), lambda i: (i, 0)),
            pl.BlockSpec((1, D_MODEL), const),
            pl.BlockSpec((D_MODEL, 4 * W_MIX), const),
            pl.BlockSpec((D_MODEL, 4 * W_MIX), const),
            pl.BlockSpec((D_MODEL, LANES), const),
            pl.BlockSpec((8, D_MODEL), const),
            pl.BlockSpec((1, LANES), const),
            pl.BlockSpec((8, 1), const),
        ],
        out_specs=[
            pl.BlockSpec((tm, 4 * W_MIX), lambda i: (i, 0)),
            pl.BlockSpec((tm, 4 * W_MIX), lambda i: (i, 0)),
            pl.BlockSpec((tm, LANES), lambda i: (i, 0)),
            pl.BlockSpec((8, tm), lambda i: (0, i)),
        ],
        out_shape=[
            jax.ShapeDtypeStruct((n, 4 * W_MIX), BF16),
            jax.ShapeDtypeStruct((n, 4 * W_MIX), BF16),
            jax.ShapeDtypeStruct((n, LANES), F32),
            jax.ShapeDtypeStruct((8, n), F32),
        ],
        compiler_params=_params("parallel"),
        name="in_proj",
    )(x2, g_mix, w_m, w_h, w_g, w_gt, b_g, b_gt)


def _mlstm_kernel(zm_ref, gate_ref, gate_t_ref, conv_ref, gn_ref, y_ref,
                  cbuf, c_ref, n_ref, m_ref):
    L = CHUNK

    @pl.when(pl.program_id(1) == 0)
    def _():
        cbuf[0:8, :] = jnp.zeros((8, 2 * W_MIX), F32)
        c_ref[...] = jnp.zeros_like(c_ref)
        n_ref[...] = jnp.zeros_like(n_ref)
        m_ref[...] = jnp.zeros_like(m_ref)

    cbuf[8:8 + L, :] = zm_ref[:, 0:2 * W_MIX].astype(F32)
    acc = cbuf[8:8 + L, :] * conv_ref[CONV_WIDTH - 1:CONV_WIDTH, :]
    for j in range(CONV_WIDTH - 1):
        acc = acc + cbuf[5 + j:5 + j + L, :] * conv_ref[j:j + 1, :]
    cbuf[0:8, :] = cbuf[L:L + 8, :]
    qk = _silu(acc)

    gate = gate_ref[...]
    gate_t = gate_t_ref[...]
    row = lax.broadcasted_iota(I32, (L, L), 0)
    col = lax.broadcasted_iota(I32, (L, L), 1)
    causal = col <= row
    tri = causal.astype(F32)
    bcum_c = jnp.dot(tri, _log_sigmoid(gate), precision=HIGHEST, preferred_element_type=F32)
    bcum_r = jnp.dot(_log_sigmoid(gate_t), (row <= col).astype(F32), precision=HIGHEST,
                     preferred_element_type=F32)

    for h in range(N_HEADS):
        hs = slice(h * HEAD_DIM, (h + 1) * HEAD_DIM)
        q = qk[:, h * HEAD_DIM:(h + 1) * HEAD_DIM]
        k = qk[:, W_MIX + h * HEAD_DIM:W_MIX + (h + 1) * HEAD_DIM] * (HEAD_DIM ** -0.5)
        v = zm_ref[:, 2 * W_MIX + h * HEAD_DIM:2 * W_MIX + (h + 1) * HEAD_DIM]
        o_pre = zm_ref[:, 3 * W_MIX + h * HEAD_DIM:3 * W_MIX + (h + 1) * HEAD_DIM].astype(F32)
        qb = q.astype(BF16)

        bc = bcum_c[:, N_HEADS + h:N_HEADS + h + 1]
        ic = gate[:, h:h + 1]
        br = bcum_r[N_HEADS + h:N_HEADS + h + 1, :]
        ir = gate_t[h:h + 1, :]
        m_prev = m_ref[h:h + 1, 0:1]
        b_last = bc[L - 1:L, :]

        log_d = jnp.where(causal, bc - br + ir, -jnp.inf)
        inter = bc + m_prev
        m_t = jnp.maximum(inter, jnp.max(log_d, axis=-1, keepdims=True))
        s = _dot_nt(qb, k.astype(BF16)) * jnp.exp(log_d - m_t)
        w_inter = jnp.exp(inter - m_t)
        num = _dot(s.astype(BF16), v) + w_inter * _dot(qb, c_ref[h].astype(BF16))
        den = jnp.sum(s, axis=-1, keepdims=True) + w_inter * jnp.sum(q * n_ref[h:h + 1, :], axis=-1, keepdims=True)
        hh = num * (1.0 / jnp.maximum(jnp.abs(den), jnp.exp(-m_t)))

        w_log = b_last - bc + ic
        m_new = jnp.maximum(b_last + m_prev, jnp.max(w_log, axis=0, keepdims=True))
        kw = k * jnp.exp(w_log - m_new)
        decay = jnp.exp(b_last + m_prev - m_new)
        c_ref[h] = decay * c_ref[h] + _dot_tn(kw.astype(BF16), v)
        n_ref[h:h + 1, :] = decay * n_ref[h:h + 1, :] + jnp.sum(kw, axis=0, keepdims=True)
        m_ref[h:h + 1, :] = jnp.broadcast_to(m_new, (1, LANES))

        hh = hh * jax.nn.sigmoid(o_pre)
        hh = hh * lax.rsqrt(jnp.mean(hh * hh, axis=-1, keepdims=True) + EPS) * gn_ref[:, hs]
        y_ref[:, hs] = hh.astype(BF16)


def _mlstm(zm, gate, gate_t, conv_w, g_norm, batch, n_chunks):
    n = zm.shape[0]
    rows = lambda b, c: (b * n_chunks + c, 0)
    const = lambda b, c: (0, 0)
    return pl.pallas_call(
        _mlstm_kernel,
        grid=(batch, n_chunks),
        in_specs=[
            pl.BlockSpec((CHUNK, 4 * W_MIX), rows),
            pl.BlockSpec((CHUNK, LANES), rows),
            pl.BlockSpec((8, CHUNK), lambda b, c: (0, b * n_chunks + c)),
            pl.BlockSpec((CONV_WIDTH, 2 * W_MIX), const),
            pl.BlockSpec((1, W_MIX), const),
        ],
        out_specs=pl.BlockSpec((CHUNK, W_MIX), rows),
        out_shape=jax.ShapeDtypeStruct((n, W_MIX), BF16),
        scratch_shapes=[
            pltpu.VMEM((CHUNK + 8, 2 * W_MIX), F32),
            pltpu.VMEM((N_HEADS, HEAD_DIM, HEAD_DIM), F32),
            pltpu.VMEM((8, HEAD_DIM), F32),
            pltpu.VMEM((8, LANES), F32),
        ],
        compiler_params=_params("parallel", "arbitrary"),
        name="mlstm",
    )(zm, gate, gate_t, conv_w, g_norm)


def _rows_bcast(ref, rows, span, hs):
    return jnp.concatenate([jnp.broadcast_to(ref[r:r + 1, hs], (span, HEAD_DIM)) for r in rows], axis=0)


def _hgrn_kernel(zh_ref, lb_ref, gn_ref, y_ref, st_ref, b_scr):
    L = CHUNK

    @pl.when(pl.program_id(1) == 0)
    def _():
        st_ref[...] = jnp.zeros_like(st_ref)

    lb = lb_ref[...]
    f_pre = zh_ref[:, W_MIX:2 * W_MIX].astype(F32)
    lf = jnp.log(lb + (1.0 - lb) * jax.nn.sigmoid(f_pre))
    k_all = (1.0 - lb) * jax.nn.sigmoid(-f_pre)
    row = lax.broadcasted_iota(I32, (L, L), 0)
    col = lax.broadcasted_iota(I32, (L, L), 1)
    b_scr[...] = jnp.dot((col <= row).astype(F32), lf, precision=HIGHEST, preferred_element_type=F32)

    for h in range(N_HEADS):
        hs = slice(h * HEAD_DIM, (h + 1) * HEAD_DIM)
        b = b_scr[:, hs]
        q = _silu(zh_ref[:, hs].astype(F32))
        k = k_all[:, hs]
        v = zh_ref[:, 2 * W_MIX + h * HEAD_DIM:2 * W_MIX + (h + 1) * HEAD_DIM]
        b_last = b_scr[L - 1:L, hs]

        o = _dot_nt((q * jnp.exp(b)).astype(BF16), st_ref[h].astype(BF16))

        a = jnp.zeros((L, L), F32)
        span = L // 2
        while span >= 16:
            mids = range(span, L, 2 * span)
            r = _rows_bcast(b_scr, [m for m in mids for _ in (0, 1)], span, hs)
            lower = (row & span) != 0
            e = jnp.exp(jnp.where(lower, b - r, r - b))
            qt = jnp.where(lower, q * e, 0.0).astype(BF16)
            kt = jnp.where(lower, 0.0, k * e).astype(BF16)
            p = _dot_nt(qt, kt)
            same = (row & -(2 * span)) == (col & -(2 * span))
            a = a + jnp.where(same, p, 0.0)
            span //= 2
        r = _rows_bcast(b_scr, range(0, L, 16), 16, hs)
        p = _dot_nt((q * jnp.exp(b - r)).astype(BF16), (k * jnp.exp(r - b)).astype(BF16))
        a = a + jnp.where(((row & -16) == (col & -16)) & (col <= row), p, 0.0)

        o = o + _dot(a.astype(BF16), v)

        ke = (k * jnp.exp(b_last - b)).astype(BF16)
        st_ref[h] = jnp.exp(b_last) * st_ref[h] + _dot_tn(v, ke)

        o = o * lax.rsqrt(jnp.mean(o * o, axis=-1, keepdims=True) + EPS) * gn_ref[...]
        o = o * _silu(zh_ref[:, 3 * W_MIX + h * HEAD_DIM:3 * W_MIX + (h + 1) * HEAD_DIM].astype(F32))
        y_ref[:, hs] = o.astype(BF16)


def _hgrn(zh, lb, g_norm, batch, n_chunks):
    n = zh.shape[0]
    rows = lambda b, c: (b * n_chunks + c, 0)
    const = lambda b, c: (0, 0)
    return pl.pallas_call(
        _hgrn_kernel,
        grid=(batch, n_chunks),
        in_specs=[
            pl.BlockSpec((CHUNK, 4 * W_MIX), rows),
            pl.BlockSpec((1, W_MIX), const),
            pl.BlockSpec((1, HEAD_DIM), const),
        ],
        out_specs=pl.BlockSpec((CHUNK, W_MIX), rows),
        out_shape=jax.ShapeDtypeStruct((n, W_MIX), BF16),
        scratch_shapes=[
            pltpu.VMEM((N_HEADS, HEAD_DIM, HEAD_DIM), F32),
            pltpu.VMEM((CHUNK, W_MIX), F32),
        ],
        compiler_params=_params("parallel", "arbitrary"),
        name="hgrn",
    )(zh, lb, g_norm)


def _pack_bf16_pair(lo, hi):
    lo_bits = pltpu.bitcast(lo.astype(BF16).astype(F32), U32)
    hi_bits = pltpu.bitcast(hi.astype(BF16).astype(F32), U32)
    return (hi_bits & jnp.uint32(0xFFFF0000)) | (lo_bits >> 16)


def _unpack_bf16_pair(w):
    lo = pltpu.bitcast(w << 16, F32).astype(BF16)
    hi = pltpu.bitcast(w & jnp.uint32(0xFFFF0000), F32).astype(BF16)
    return lo, hi


def _post_mix_kernel(x_ref, ym_ref, yh_ref, wom_ref, woh_ref, g_ref, wr_ref, br_ref,
                     x1_ref, hn_ref, route_ref, count_ref, run_ref):
    tm = x_ref.shape[0]

    @pl.when(pl.program_id(0) == 0)
    def _():
        run_ref[...] = jnp.zeros_like(run_ref)

    x1 = x_ref[...] + _dot(ym_ref[...], wom_ref[...]) + _dot(yh_ref[...], woh_ref[...])
    x1_ref[...] = x1
    hn = _rms(x1, g_ref[...])
    half = D_MODEL // 2
    packed = _pack_bf16_pair(hn[:, :half], hn[:, half:])
    for j in range(PACK_ROWS):
        hn_ref[pl.ds(j, tm, stride=PACK_ROWS), :] = packed[:, j * LANES:(j + 1) * LANES]

    logits = jnp.dot(hn, wr_ref[...], precision=HIGHEST, preferred_element_type=F32) + br_ref[...]
    lane = lax.broadcasted_iota(I32, (tm, LANES), 1)
    neg = -jnp.inf
    g_l = jnp.where(lane < N_GROUPS, logits, neg)
    g_max = jnp.max(g_l, axis=-1, keepdims=True)
    g_sel = jnp.min(jnp.where(g_l == g_max, lane, LANES), axis=-1, keepdims=True)
    g_val = 1.0 / jnp.sum(jnp.exp(g_l - g_max), axis=-1, keepdims=True)

    e_lane = lane - N_GROUPS
    in_group = (e_lane >= g_sel * EXPERTS_PER_GROUP) & (e_lane < (g_sel + 1) * EXPERTS_PER_GROUP)
    e_l = jnp.where(in_group, logits, neg)
    v1 = jnp.max(e_l, axis=-1, keepdims=True)
    i1 = jnp.min(jnp.where(e_l == v1, lane, LANES), axis=-1, keepdims=True)
    e_l2 = jnp.where(lane == i1, neg, e_l)
    v2 = jnp.max(e_l2, axis=-1, keepdims=True)
    i2 = jnp.min(jnp.where(e_l2 == v2, lane, LANES), axis=-1, keepdims=True)
    t = jnp.exp(v2 - v1)
    c1 = g_val / (1.0 + t)
    c2 = g_val * t / (1.0 + t)

    hot1 = lane == i1
    hot2 = lane == i2
    hot = (hot1 | hot2).astype(F32)
    r_i = lax.broadcasted_iota(I32, (tm, tm), 0)
    c_i = lax.broadcasted_iota(I32, (tm, tm), 1)
    before = _dot((c_i < r_i).astype(BF16), hot.astype(BF16)) + run_ref[0:1, :]
    rank1 = jnp.sum(jnp.where(hot1, before, 0.0), axis=-1, keepdims=True)
    rank2 = jnp.sum(jnp.where(hot2, before, 0.0), axis=-1, keepdims=True)
    run_ref[0:1, :] = run_ref[0:1, :] + jnp.sum(hot, axis=0, keepdims=True)
    count_ref[...] = jnp.broadcast_to(run_ref[0:1, :], count_ref.shape)

    out = jnp.where(lane == 0, (i1 - N_GROUPS).astype(F32), 0.0)
    out = jnp.where(lane == 1, (i2 - N_GROUPS).astype(F32), out)
    out = jnp.where(lane == 2, rank1, out)
    out = jnp.where(lane == 3, rank2, out)
    out = jnp.where(lane == 4, c1, out)
    out = jnp.where(lane == 5, c2, out)
    route_ref[...] = out


def _post_mix(x2, y_m, y_h, wo_m, wo_h, g_ffn, w_r, b_r, tm):
    n = x2.shape[0]
    const = lambda i: (0, 0)
    rows = lambda i: (i, 0)
    return pl.pallas_call(
        _post_mix_kernel,
        grid=(n // tm,),
        in_specs=[
            pl.BlockSpec((tm, D_MODEL), rows),
            pl.BlockSpec((tm, W_MIX), rows),
            pl.BlockSpec((tm, W_MIX), rows),
            pl.BlockSpec((W_MIX, D_MODEL), const),
            pl.BlockSpec((W_MIX, D_MODEL), const),
            pl.BlockSpec((1, D_MODEL), const),
            pl.BlockSpec((D_MODEL, LANES), const),
            pl.BlockSpec((1, LANES), const),
        ],
        out_specs=[
            pl.BlockSpec((tm, D_MODEL), rows),
            pl.BlockSpec((tm * PACK_ROWS, LANES), rows),
            pl.BlockSpec((tm, LANES), rows),
            pl.BlockSpec((8, LANES), const),
        ],
        out_shape=[
            jax.ShapeDtypeStruct((n, D_MODEL), F32),
            jax.ShapeDtypeStruct((n * PACK_ROWS, LANES), U32),
            jax.ShapeDtypeStruct((n, LANES), F32),
            jax.ShapeDtypeStruct((8, LANES), F32),
        ],
        scratch_shapes=[pltpu.VMEM((8, LANES), F32)],
        compiler_params=_params("arbitrary"),
        name="post_mix",
    )(x2, y_m, y_h, wo_m, wo_h, g_ffn, w_r, b_r)


def _dispatch_kernel(dest_ref, hn_ref, xb_init_ref, xb_ref, sem):
    del xb_init_ref
    ts = hn_ref.shape[0] // PACK_ROWS

    def start(j, carry):
        src = pl.multiple_of((j >> 1) * PACK_ROWS, PACK_ROWS)
        dst = pl.multiple_of(dest_ref[0, 0, j] * PACK_ROWS, PACK_ROWS)
        pltpu.make_async_copy(hn_ref.at[pl.ds(src, PACK_ROWS), :], xb_ref.at[pl.ds(dst, PACK_ROWS), :],
                              sem.at[0]).start()
        return carry

    lax.fori_loop(0, 2 * ts, start, 0, unroll=DMA_UNROLL)
    for _ in range(2):
        pltpu.make_async_copy(hn_ref, xb_ref.at[pl.ds(0, ts * PACK_ROWS), :], sem.at[0]).wait()


def _dispatch(dest3, hn, n_rows, ts):
    n = hn.shape[0] // PACK_ROWS
    return pl.pallas_call(
        _dispatch_kernel,
        grid=(n // ts,),
        in_specs=[
            pl.BlockSpec((1, 1, 2 * ts), lambda i: (i, 0, 0), memory_space=pltpu.SMEM),
            pl.BlockSpec((ts * PACK_ROWS, LANES), lambda i: (i, 0)),
            pl.BlockSpec(memory_space=pl.ANY),
        ],
        out_specs=pl.BlockSpec(memory_space=pl.ANY),
        out_shape=jax.ShapeDtypeStruct((n_rows * PACK_ROWS, LANES), U32),
        scratch_shapes=[pltpu.SemaphoreType.DMA((1,))],
        input_output_aliases={2: 0},
        compiler_params=_params("arbitrary"),
        name="dispatch",
    )(dest3, hn, jnp.zeros((n_rows * PACK_ROWS, LANES), U32))


def _experts_kernel(be_ref, nu_ref, xb_ref, wg_ref, wu_ref, wd_ref, yb_ref):
    i = pl.program_id(0)

    @pl.when(i < nu_ref[0])
    def _():
        pairs = [_unpack_bf16_pair(xb_ref[pl.ds(j, EXPERT_BLOCK, stride=PACK_ROWS), :]) for j in range(PACK_ROWS)]
        x = jnp.concatenate([lo for lo, _ in pairs] + [hi for _, hi in pairs], axis=1)
        a = (_silu(_dot(x, wg_ref[...])) * _dot(x, wu_ref[...])).astype(BF16)
        y = _dot(a, wd_ref[...])
        for j in range(OUT_ROWS):
            yb_ref[pl.ds(j, EXPERT_BLOCK, stride=OUT_ROWS), :] = y[:, j * LANES:(j + 1) * LANES]

    @pl.when(i >= nu_ref[0])
    def _():
        yb_ref[...] = jnp.zeros_like(yb_ref)


def _experts(block_e, n_used, xb, w_gate, w_up, w_down):
    n_rows = xb.shape[0] // PACK_ROWS
    n_blocks = n_rows // EXPERT_BLOCK
    xrow = lambda i, be, nu: (jnp.minimum(i, nu[0] - 1), 0)
    wsel = lambda i, be, nu: (be[i], 0, 0)
    grid_spec = pltpu.PrefetchScalarGridSpec(
        num_scalar_prefetch=2,
        grid=(n_blocks,),
        in_specs=[
            pl.BlockSpec((EXPERT_BLOCK * PACK_ROWS, LANES), xrow),
            pl.BlockSpec((None, D_MODEL, D_EXPERT), wsel),
            pl.BlockSpec((None, D_MODEL, D_EXPERT), wsel),
            pl.BlockSpec((None, D_EXPERT, D_MODEL), wsel),
        ],
        out_specs=pl.BlockSpec((EXPERT_BLOCK * OUT_ROWS, LANES), lambda i, be, nu: (i, 0)),
    )
    return pl.pallas_call(
        _experts_kernel,
        grid_spec=grid_spec,
        out_shape=jax.ShapeDtypeStruct((n_rows * OUT_ROWS, LANES), F32),
        compiler_params=_params("arbitrary"),
        name="experts",
    )(block_e, n_used, xb, w_gate, w_up, w_down)


def _combine_kernel(dest_ref, dest_next_ref, x1_ref, p_ref, route_ref, gpl_ref, wplg_ref, wplp_ref, gfin_ref,
                    yb_ref, out_ref, gbuf, sem):
    tf = x1_ref.shape[0]
    i = pl.program_id(0)
    n_steps = pl.num_programs(0)
    slot = i & 1

    def start_gathers(d_ref, s):
        def body(t, carry):
            row = pl.multiple_of(t * OUT_ROWS, OUT_ROWS)
            for k in range(2):
                src = pl.multiple_of(d_ref[0, 0, 2 * t + k] * OUT_ROWS, OUT_ROWS)
                pltpu.make_async_copy(yb_ref.at[pl.ds(src, OUT_ROWS), :], gbuf.at[s, k, pl.ds(row, OUT_ROWS), :],
                                      sem.at[s]).start()
            return carry

        lax.fori_loop(0, tf, body, 0, unroll=DMA_UNROLL // 2)

    @pl.when(i == 0)
    def _():
        start_gathers(dest_ref, 0)

    @pl.when(i + 1 < n_steps)
    def _():
        start_gathers(dest_next_ref, 1 - slot)

    for k in range(2):
        pltpu.make_async_copy(yb_ref.at[pl.ds(0, tf * OUT_ROWS), :], gbuf.at[slot, k], sem.at[slot]).wait()

    route = route_ref[...]
    g1, g2 = (jnp.concatenate([gbuf[slot, k, pl.ds(j, tf, stride=OUT_ROWS), :] for j in range(OUT_ROWS)], axis=1)
              for k in range(2))
    y = route[:, 4:5] * g1 + route[:, 5:6] * g2
    x2 = x1_ref[...] + y
    gate = jax.nn.sigmoid(_dot(_rms(x2, gpl_ref[...]).astype(BF16), wplg_ref[...]))
    x3 = x2 + gate * _dot(p_ref[...].astype(BF16), wplp_ref[...])
    out_ref[...] = _rms(x3, gfin_ref[...])


def _combine(dest3, x1, p2, route, g_pl, w_plg, w_plp, g_final, yb, tf):
    n = x1.shape[0]
    n_steps = n // tf
    const = lambda i: (0, 0)
    rows = lambda i: (i, 0)
    return pl.pallas_call(
        _combine_kernel,
        grid=(n_steps,),
        in_specs=[
            pl.BlockSpec((1, 1, 2 * tf), lambda i: (i, 0, 0), memory_space=pltpu.SMEM),
            pl.BlockSpec((1, 1, 2 * tf), lambda i: (jnp.minimum(i + 1, n_steps - 1), 0, 0), memory_space=pltpu.SMEM),
            pl.BlockSpec((tf, D_MODEL), rows),
            pl.BlockSpec((tf, PLE_DIM), rows),
            pl.BlockSpec((tf, LANES), rows),
            pl.BlockSpec((1, D_MODEL), const),
            pl.BlockSpec((D_MODEL, D_MODEL), const),
            pl.BlockSpec((PLE_DIM, D_MODEL), const),
            pl.BlockSpec((1, D_MODEL), const),
            pl.BlockSpec(memory_space=pl.ANY),
        ],
        out_specs=pl.BlockSpec((tf, D_MODEL), rows),
        out_shape=jax.ShapeDtypeStruct((n, D_MODEL), F32),
        scratch_shapes=[pltpu.VMEM((2, 2, tf * OUT_ROWS, LANES), F32), pltpu.SemaphoreType.DMA((2,))],
        compiler_params=_params("arbitrary"),
        name="combine",
    )(dest3, dest3, x1, p2, route, g_pl, w_plg, w_plp, g_final, yb)


def _layer(x2, p2, batch, seq, g_mix, w_in, b_mgate, conv_qk, g_mlstm, lb, g_hgrn, w_out, g_ffn,
           w_rg, b_rg, w_re, b_re, w_e_gate, w_e_up, w_e_down, g_pl, w_pl_gate, w_pl_proj, g_out):
    n = x2.shape[0]
    n_chunks = seq // CHUNK
    m_cols = 4 * W_MIX
    n_gate = 2 * N_HEADS

    w_in_b = w_in.astype(BF16)
    w_m = w_in_b[:, :m_cols]
    w_gcols = w_in_b[:, m_cols:m_cols + n_gate]
    w_h = w_in_b[:, m_cols + n_gate:]
    w_g = jnp.pad(w_gcols, ((0, 0), (0, LANES - n_gate)))
    b_g = jnp.pad(b_mgate.astype(F32), (0, LANES - n_gate))[None, :]
    zm, zh, gate, gate_t = _in_proj(x2, g_mix[None, :], w_m, w_h, w_g, w_gcols.T, b_g,
                                    b_mgate.astype(F32)[:, None], tm=min(512, n))

    y_m = _mlstm(zm, gate, gate_t, conv_qk, g_mlstm[None, :], batch, n_chunks)
    y_h = _hgrn(zh, lb[None, :], g_hgrn[None, :], batch, n_chunks)

    w_out_b = w_out.astype(BF16)
    w_r = jnp.pad(jnp.concatenate([w_rg, w_re], axis=1), ((0, 0), (0, LANES - N_GROUPS - N_EXPERTS)))
    b_r = jnp.pad(jnp.concatenate([b_rg, b_re]), (0, LANES - N_GROUPS - N_EXPERTS))[None, :]
    x1, hn, route, counts = _post_mix(x2, y_m, y_h, w_out_b[:W_MIX], w_out_b[W_MIX:], g_ffn[None, :],
                                      w_r, b_r, tm=min(512, n))

    counts = counts[0, N_GROUPS:N_GROUPS + N_EXPERTS].astype(I32)
    padded = (counts + EXPERT_BLOCK - 1) // EXPERT_BLOCK * EXPERT_BLOCK
    pend = jnp.cumsum(padded)
    pstart = pend - padded
    n_blocks = (2 * n) // EXPERT_BLOCK + N_EXPERTS
    n_rows = n_blocks * EXPERT_BLOCK
    n_used = (pend[-1] // EXPERT_BLOCK).astype(I32)
    block_row = jnp.minimum(jnp.arange(n_blocks, dtype=I32), n_used - 1) * EXPERT_BLOCK
    block_e = jnp.sum((pend[None, :] <= block_row[:, None]).astype(I32), axis=1)
    expert_id = route[:, 0:2].astype(I32)
    hot = expert_id[:, :, None] == jnp.arange(N_EXPERTS, dtype=I32)
    dest = jnp.sum(jnp.where(hot, pstart, 0), axis=-1) + route[:, 2:4].astype(I32)

    ts = min(512, n)
    xb = _dispatch(dest.reshape(n // ts, 1, 2 * ts), hn, n_rows, ts)
    yb = _experts(block_e, n_used[None], xb, w_e_gate.astype(BF16), w_e_up.astype(BF16), w_e_down.astype(BF16))
    tf = min(256, n)
    return _combine(dest.reshape(n // tf, 1, 2 * tf), x1, p2, route, g_pl[None, :], w_pl_gate.astype(BF16),
                    w_pl_proj.astype(BF16), g_out[None, :], yb, tf)


def kernel(x, p, g_mix, w_in, b_mgate, conv_qk, g_mlstm, hg_lb, g_hgrn, w_out, g_ffn, w_rg, b_rg, w_re, b_re,
           w_e_gate, w_e_up, w_e_down, g_pl, w_pl_gate, w_pl_proj, g_final):
    batch, seq, d = x.shape
    depth = p.shape[0]
    assert depth == 1, "the fused final norm assumes a single layer"
    lower_bounds = jnp.cumsum(jax.nn.softmax(hg_lb.astype(F32), axis=0), axis=0)
    i = 0
    out = _layer(x.reshape(batch * seq, d), p[i].reshape(batch * seq, PLE_DIM), batch, seq,
                 g_mix[i], w_in[i], b_mgate[i], conv_qk[i], g_mlstm[i], lower_bounds[i], g_hgrn[i], w_out[i],
                 g_ffn[i], w_rg[i], b_rg[i], w_re[i], b_re[i], w_e_gate[i], w_e_up[i], w_e_down[i],
                 g_pl[i], w_pl_gate[i], w_pl_proj[i], g_final)
    return out.reshape(batch, seq, d)
```

```python
import jax
import jax.numpy as jnp
from jax import lax
from jax.experimental import pallas as pl
from jax.experimental.pallas import tpu as pltpu

F32 = jnp.float32
BF16 = jnp.bfloat16
I32 = jnp.int32
U32 = jnp.uint32
EPS = 1e-6
HIGHEST = lax.Precision.HIGHEST

LANES = 128
D_MODEL = 1024
W_MIX = 512
N_HEADS = 4
HEAD_DIM = 128
N_GROUPS = 4
EXPERTS_PER_GROUP = 8
N_EXPERTS = N_GROUPS * EXPERTS_PER_GROUP
D_EXPERT = 512
PLE_DIM = 256
CONV_WIDTH = 4
CHUNK = 128
EXPERT_BLOCK = 256
PACK_ROWS = D_MODEL // 2 // LANES
OUT_ROWS = D_MODEL // LANES
ROUTER_ROWS = 48
DMA_UNROLL = 4
VMEM_LIMIT = 56 * 1024 * 1024


def _dot(a, b):
    return jnp.dot(a, b, preferred_element_type=F32)


def _dot_nt(a, b):
    return lax.dot_general(a, b, (((1,), (1,)), ((), ())), preferred_element_type=F32)


def _dot_tn(a, b):
    return lax.dot_general(a, b, (((0,), (0,)), ((), ())), preferred_element_type=F32)


def _rms(u, g):
    return u * lax.rsqrt(jnp.mean(u * u, axis=-1, keepdims=True) + EPS) * g


def _silu(u):
    return u * jax.nn.sigmoid(u)


def _log_sigmoid(u):
    return jnp.minimum(u, 0.0) - jnp.log1p(jnp.exp(-jnp.abs(u)))


def _params(*sem):
    return pltpu.CompilerParams(dimension_semantics=sem, vmem_limit_bytes=VMEM_LIMIT)


def _in_proj_kernel(x_ref, g_ref, wm_ref, wh_ref, wg_ref, wgt_ref, bg_ref, bgt_ref,
                    zm_ref, zh_ref, gate_ref, gate_t_ref):
    h = _rms(x_ref[...], g_ref[...]).astype(BF16)
    step = 512
    for c0 in range(0, 4 * W_MIX, step):
        zm_ref[:, c0:c0 + step] = _dot(h, wm_ref[:, c0:c0 + step]).astype(BF16)
        zh_ref[:, c0:c0 + step] = _dot(h, wh_ref[:, c0:c0 + step]).astype(BF16)
    gate_ref[...] = _dot(h, wg_ref[...]) + bg_ref[...]
    gate_t_ref[...] = _dot_nt(wgt_ref[...], h) + bgt_ref[...]


def _in_proj(x2, g_mix, w_m, w_h, w_g, w_gt, b_g, b_gt, tm):
    n = x2.shape[0]
    const = lambda i: (0, 0)
    return pl.pallas_call(
        _in_proj_kernel,
        grid=(n // tm,),
        in_specs=[
            pl.BlockSpec((tm, D_MODEL), lambda i: (i, 0)),
            pl.BlockSpec((1, D_MODEL), const),
            pl.BlockSpec((D_MODEL, 4 * W_MIX), const),
            pl.BlockSpec((D_MODEL, 4 * W_MIX), const),
            pl.BlockSpec((D_MODEL, LANES), const),
            pl.BlockSpec((8, D_MODEL), const),
            pl.BlockSpec((1, LANES), const),
            pl.BlockSpec((8, 1), const),
        ],
        out_specs=[
            pl.BlockSpec((tm, 4 * W_MIX), lambda i: (i, 0)),
            pl.BlockSpec((tm, 4 * W_MIX), lambda i: (i, 0)),
            pl.BlockSpec((tm, LANES), lambda i: (i, 0)),
            pl.BlockSpec((8, tm), lambda i: (0, i)),
        ],
        out_shape=[
            jax.ShapeDtypeStruct((n, 4 * W_MIX), BF16),
            jax.ShapeDtypeStruct((n, 4 * W_MIX), BF16),
            jax.ShapeDtypeStruct((n, LANES), F32),
            jax.ShapeDtypeStruct((8, n), F32),
        ],
        compiler_params=_params("parallel"),
        name="in_proj",
    )(x2, g_mix, w_m, w_h, w_g, w_gt, b_g, b_gt)


def _mlstm_chunk(zm_ref, gate_ref, gate_t_ref, conv_ref, gn_ref, y_ref, cbuf, c_ref, n_ref, m_ref):
    L = CHUNK

    cbuf[8:8 + L, :] = zm_ref[:, 0:2 * W_MIX].astype(F32)
    acc = cbuf[8:8 + L, :] * conv_ref[CONV_WIDTH - 1:CONV_WIDTH, :]
    for j in range(CONV_WIDTH - 1):
        acc = acc + cbuf[5 + j:5 + j + L, :] * conv_ref[j:j + 1, :]
    cbuf[0:8, :] = cbuf[L:L + 8, :]
    qk = _silu(acc)

    gate = gate_ref[...]
    gate_t = gate_t_ref[...]
    row = lax.broadcasted_iota(I32, (L, L), 0)
    col = lax.broadcasted_iota(I32, (L, L), 1)
    causal = col <= row
    tri = causal.astype(F32)
    bcum_c = jnp.dot(tri, _log_sigmoid(gate), precision=HIGHEST, preferred_element_type=F32)
    bcum_r = jnp.dot(_log_sigmoid(gate_t), (row <= col).astype(F32), precision=HIGHEST,
                     preferred_element_type=F32)

    for h in range(N_HEADS):
        hs = slice(h * HEAD_DIM, (h + 1) * HEAD_DIM)
        q = qk[:, h * HEAD_DIM:(h + 1) * HEAD_DIM]
        k = qk[:, W_MIX + h * HEAD_DIM:W_MIX + (h + 1) * HEAD_DIM] * (HEAD_DIM ** -0.5)
        v = zm_ref[:, 2 * W_MIX + h * HEAD_DIM:2 * W_MIX + (h + 1) * HEAD_DIM]
        o_pre = zm_ref[:, 3 * W_MIX + h * HEAD_DIM:3 * W_MIX + (h + 1) * HEAD_DIM].astype(F32)
        qb = q.astype(BF16)

        bc = bcum_c[:, N_HEADS + h:N_HEADS + h + 1]
        ic = gate[:, h:h + 1]
        br = bcum_r[N_HEADS + h:N_HEADS + h + 1, :]
        ir = gate_t[h:h + 1, :]
        m_prev = m_ref[h:h + 1, 0:1]
        b_last = bc[L - 1:L, :]

        log_d = jnp.where(causal, bc - br + ir, -jnp.inf)
        inter = bc + m_prev
        m_t = jnp.maximum(inter, jnp.max(log_d, axis=-1, keepdims=True))
        s = _dot_nt(qb, k.astype(BF16)) * jnp.exp(log_d - m_t)
        w_inter = jnp.exp(inter - m_t)
        num = _dot(s.astype(BF16), v) + w_inter * _dot(qb, c_ref[h].astype(BF16))
        den = jnp.sum(s, axis=-1, keepdims=True) + w_inter * jnp.sum(q * n_ref[h:h + 1, :], axis=-1, keepdims=True)
        hh = num * (1.0 / jnp.maximum(jnp.abs(den), jnp.exp(-m_t)))

        w_log = b_last - bc + ic
        m_new = jnp.maximum(b_last + m_prev, jnp.max(w_log, axis=0, keepdims=True))
        kw = k * jnp.exp(w_log - m_new)
        decay = jnp.exp(b_last + m_prev - m_new)
        c_ref[h] = decay * c_ref[h] + _dot_tn(kw.astype(BF16), v)
        n_ref[h:h + 1, :] = decay * n_ref[h:h + 1, :] + jnp.sum(kw, axis=0, keepdims=True)
        m_ref[h:h + 1, :] = jnp.broadcast_to(m_new, (1, LANES))

        hh = hh * jax.nn.sigmoid(o_pre)
        hh = hh * lax.rsqrt(jnp.mean(hh * hh, axis=-1, keepdims=True) + EPS) * gn_ref[:, hs]
        y_ref[:, hs] = hh.astype(BF16)


def _rows_bcast(ref, rows, span, hs):
    return jnp.concatenate([jnp.broadcast_to(ref[r:r + 1, hs], (span, HEAD_DIM)) for r in rows], axis=0)


def _hgrn_chunk(zh_ref, lb_ref, gn_ref, y_ref, st_ref, b_scr):
    L = CHUNK

    lb = lb_ref[...]
    f_pre = zh_ref[:, W_MIX:2 * W_MIX].astype(F32)
    lf = jnp.log(lb + (1.0 - lb) * jax.nn.sigmoid(f_pre))
    k_all = (1.0 - lb) * jax.nn.sigmoid(-f_pre)
    row = lax.broadcasted_iota(I32, (L, L), 0)
    col = lax.broadcasted_iota(I32, (L, L), 1)
    b_scr[...] = jnp.dot((col <= row).astype(F32), lf, precision=HIGHEST, preferred_element_type=F32)

    for h in range(N_HEADS):
        hs = slice(h * HEAD_DIM, (h + 1) * HEAD_DIM)
        b = b_scr[:, hs]
        q = _silu(zh_ref[:, hs].astype(F32))
        k = k_all[:, hs]
        v = zh_ref[:, 2 * W_MIX + h * HEAD_DIM:2 * W_MIX + (h + 1) * HEAD_DIM]
        b_last = b_scr[L - 1:L, hs]

        o = _dot_nt((q * jnp.exp(b)).astype(BF16), st_ref[h].astype(BF16))

        a = jnp.zeros((L, L), F32)
        span = L // 2
        while span >= 16:
            mids = range(span, L, 2 * span)
            r = _rows_bcast(b_scr, [m for m in mids for _ in (0, 1)], span, hs)
            lower = (row & span) != 0
            e = jnp.exp(jnp.where(lower, b - r, r - b))
            qt = jnp.where(lower, q * e, 0.0).astype(BF16)
            kt = jnp.where(lower, 0.0, k * e).astype(BF16)
            p = _dot_nt(qt, kt)
            same = (row & -(2 * span)) == (col & -(2 * span))
            a = a + jnp.where(same, p, 0.0)
            span //= 2
        r = _rows_bcast(b_scr, range(0, L, 16), 16, hs)
        p = _dot_nt((q * jnp.exp(b - r)).astype(BF16), (k * jnp.exp(r - b)).astype(BF16))
        a = a + jnp.where(((row & -16) == (col & -16)) & (col <= row), p, 0.0)

        o = o + _dot(a.astype(BF16), v)

        ke = (k * jnp.exp(b_last - b)).astype(BF16)
        st_ref[h] = jnp.exp(b_last) * st_ref[h] + _dot_tn(v, ke)

        o = o * lax.rsqrt(jnp.mean(o * o, axis=-1, keepdims=True) + EPS) * gn_ref[...]
        o = o * _silu(zh_ref[:, 3 * W_MIX + h * HEAD_DIM:3 * W_MIX + (h + 1) * HEAD_DIM].astype(F32))
        y_ref[:, W_MIX + h * HEAD_DIM:W_MIX + (h + 1) * HEAD_DIM] = o.astype(BF16)


def _mixer_kernel(zm_ref, zh_ref, gate_ref, gate_t_ref, conv_ref, gm_ref, lb_ref, gh_ref, y_ref,
                  cbuf, c_ref, n_ref, m_ref, st_ref, b_scr):
    @pl.when(pl.program_id(1) == 0)
    def _():
        cbuf[0:8, :] = jnp.zeros((8, 2 * W_MIX), F32)
        c_ref[...] = jnp.zeros_like(c_ref)
        n_ref[...] = jnp.zeros_like(n_ref)
        m_ref[...] = jnp.zeros_like(m_ref)
        st_ref[...] = jnp.zeros_like(st_ref)

    _mlstm_chunk(zm_ref, gate_ref, gate_t_ref, conv_ref, gm_ref, y_ref, cbuf, c_ref, n_ref, m_ref)
    _hgrn_chunk(zh_ref, lb_ref, gh_ref, y_ref, st_ref, b_scr)


def _mixer(zm, zh, gate, gate_t, conv_w, g_mlstm, lb, g_hgrn, batch, n_chunks):
    n = zm.shape[0]
    rows = lambda b, c: (b * n_chunks + c, 0)
    const = lambda b, c: (0, 0)
    return pl.pallas_call(
        _mixer_kernel,
        grid=(batch, n_chunks),
        in_specs=[
            pl.BlockSpec((CHUNK, 4 * W_MIX), rows),
            pl.BlockSpec((CHUNK, 4 * W_MIX), rows),
            pl.BlockSpec((CHUNK, LANES), rows),
            pl.BlockSpec((8, CHUNK), lambda b, c: (0, b * n_chunks + c)),
            pl.BlockSpec((CONV_WIDTH, 2 * W_MIX), const),
            pl.BlockSpec((1, W_MIX), const),
            pl.BlockSpec((1, W_MIX), const),
            pl.BlockSpec((1, HEAD_DIM), const),
        ],
        out_specs=pl.BlockSpec((CHUNK, 2 * W_MIX), rows),
        out_shape=jax.ShapeDtypeStruct((n, 2 * W_MIX), BF16),
        scratch_shapes=[
            pltpu.VMEM((CHUNK + 8, 2 * W_MIX), F32),
            pltpu.VMEM((N_HEADS, HEAD_DIM, HEAD_DIM), F32),
            pltpu.VMEM((8, HEAD_DIM), F32),
            pltpu.VMEM((8, LANES), F32),
            pltpu.VMEM((N_HEADS, HEAD_DIM, HEAD_DIM), F32),
            pltpu.VMEM((CHUNK, W_MIX), F32),
        ],
        compiler_params=_params("parallel", "arbitrary"),
        name="mixer",
    )(zm, zh, gate, gate_t, conv_w, g_mlstm, lb, g_hgrn)


def _pack_bf16_pair(lo, hi):
    lo_bits = pltpu.bitcast(lo.astype(BF16).astype(F32), U32)
    hi_bits = pltpu.bitcast(hi.astype(BF16).astype(F32), U32)
    return (hi_bits & jnp.uint32(0xFFFF0000)) | (lo_bits >> 16)


def _unpack_bf16_pair(w):
    lo = pltpu.bitcast(w << 16, F32).astype(BF16)
    hi = pltpu.bitcast(w & jnp.uint32(0xFFFF0000), F32).astype(BF16)
    return lo, hi


def _post_mix_kernel(x_ref, y_ref, wo_ref, g_ref, wrh_ref, wrl_ref, br_ref,
                     x1_ref, hn_ref, route_t_ref, comb_ref, count_ref, run_ref):
    tm = x_ref.shape[0]
    n_r = wrh_ref.shape[0]

    @pl.when(pl.program_id(0) == 0)
    def _():
        run_ref[...] = jnp.zeros_like(run_ref)

    x1 = x_ref[...] + _dot(y_ref[...], wo_ref[...])
    x1_ref[...] = x1
    hn = _rms(x1, g_ref[...])
    half = D_MODEL // 2
    packed = _pack_bf16_pair(hn[:, :half], hn[:, half:])
    for j in range(PACK_ROWS):
        hn_ref[pl.ds(j, tm, stride=PACK_ROWS), :] = packed[:, j * LANES:(j + 1) * LANES]

    hn_hi = hn.astype(BF16)
    hn_lo = (hn - hn_hi.astype(F32)).astype(BF16)
    logits = (_dot_nt(wrh_ref[...], hn_hi) + (_dot_nt(wrl_ref[...], hn_hi) + _dot_nt(wrh_ref[...], hn_lo))
              + br_ref[...])
    rix = lax.broadcasted_iota(I32, (n_r, tm), 0)
    neg = -jnp.inf
    g_l = jnp.where(rix < N_GROUPS, logits, neg)
    g_max = jnp.max(g_l, axis=0, keepdims=True)
    g_sel = jnp.min(jnp.where(g_l == g_max, rix, n_r), axis=0, keepdims=True)
    g_val = 1.0 / jnp.sum(jnp.exp(g_l - g_max), axis=0, keepdims=True)

    e_row = rix - N_GROUPS
    in_group = (e_row >= g_sel * EXPERTS_PER_GROUP) & (e_row < (g_sel + 1) * EXPERTS_PER_GROUP)
    e_l = jnp.where(in_group, logits, neg)
    v1 = jnp.max(e_l, axis=0, keepdims=True)
    i1 = jnp.min(jnp.where(e_l == v1, rix, n_r), axis=0, keepdims=True)
    e_l2 = jnp.where(rix == i1, neg, e_l)
    v2 = jnp.max(e_l2, axis=0, keepdims=True)
    i2 = jnp.min(jnp.where(e_l2 == v2, rix, n_r), axis=0, keepdims=True)
    t = jnp.exp(v2 - v1)
    c1 = g_val / (1.0 + t)
    c2 = g_val * t / (1.0 + t)

    hot1 = rix == i1
    hot2 = rix == i2
    hot = (hot1 | hot2).astype(F32)
    r_i = lax.broadcasted_iota(I32, (tm, tm), 0)
    c_i = lax.broadcasted_iota(I32, (tm, tm), 1)
    before = _dot(hot.astype(BF16), (r_i < c_i).astype(BF16)) + run_ref[:, 0:1]
    rank1 = jnp.sum(jnp.where(hot1, before, 0.0), axis=0, keepdims=True)
    rank2 = jnp.sum(jnp.where(hot2, before, 0.0), axis=0, keepdims=True)
    run_ref[...] = run_ref[...] + jnp.sum(hot, axis=1, keepdims=True)
    count_ref[...] = run_ref[...]

    r8 = lax.broadcasted_iota(I32, (8, tm), 0)
    out = jnp.where(r8 == 0, (i1 - N_GROUPS).astype(F32), 0.0)
    out = jnp.where(r8 == 1, (i2 - N_GROUPS).astype(F32), out)
    out = jnp.where(r8 == 2, rank1, out)
    out = jnp.where(r8 == 3, rank2, out)
    route_t_ref[...] = out

    r128 = lax.broadcasted_iota(I32, (LANES, tm), 0)
    slab = jnp.where(r128 == 0, c1, jnp.where(r128 == 1, c2, 0.0))
    for c in range(tm // LANES):
        comb_ref[c * LANES:(c + 1) * LANES, :] = slab[:, c * LANES:(c + 1) * LANES].T


def _post_mix(x2, y, w_out, g_ffn, wr_hi, wr_lo, b_r, tm):
    n = x2.shape[0]
    n_r = wr_hi.shape[0]
    const = lambda i: (0, 0)
    rows = lambda i: (i, 0)
    return pl.pallas_call(
        _post_mix_kernel,
        grid=(n // tm,),
        in_specs=[
            pl.BlockSpec((tm, D_MODEL), rows),
            pl.BlockSpec((tm, 2 * W_MIX), rows),
            pl.BlockSpec((2 * W_MIX, D_MODEL), const),
            pl.BlockSpec((1, D_MODEL), const),
            pl.BlockSpec((n_r, D_MODEL), const),
            pl.BlockSpec((n_r, D_MODEL), const),
            pl.BlockSpec((n_r, 1), const),
        ],
        out_specs=[
            pl.BlockSpec((tm, D_MODEL), rows),
            pl.BlockSpec((tm * PACK_ROWS, LANES), rows),
            pl.BlockSpec((8, tm), lambda i: (0, i)),
            pl.BlockSpec((tm, LANES), rows),
            pl.BlockSpec((n_r, LANES), const),
        ],
        out_shape=[
            jax.ShapeDtypeStruct((n, D_MODEL), F32),
            jax.ShapeDtypeStruct((n * PACK_ROWS, LANES), U32),
            jax.ShapeDtypeStruct((8, n), F32),
            jax.ShapeDtypeStruct((n, LANES), F32),
            jax.ShapeDtypeStruct((n_r, LANES), F32),
        ],
        scratch_shapes=[pltpu.VMEM((n_r, LANES), F32)],
        compiler_params=_params("arbitrary"),
        name="post_mix",
    )(x2, y, w_out, g_ffn, wr_hi, wr_lo, b_r)


def _dispatch_kernel(d0_ref, d1_ref, hn_ref, xb_init_ref, xb_ref, sem):
    del xb_init_ref
    ts = hn_ref.shape[0] // PACK_ROWS

    def start(t, carry):
        src = pl.multiple_of(t * PACK_ROWS, PACK_ROWS)
        for k, d_ref in enumerate((d0_ref, d1_ref)):
            dst = pl.multiple_of(d_ref[0, 0, t] * PACK_ROWS, PACK_ROWS)
            pltpu.make_async_copy(hn_ref.at[pl.ds(src, PACK_ROWS), :], xb_ref.at[pl.ds(dst, PACK_ROWS), :],
                                  sem.at[k]).start(priority=k)
        return carry

    lax.fori_loop(0, ts, start, 0, unroll=DMA_UNROLL)
    for k in range(2):
        pltpu.make_async_copy(hn_ref, xb_ref.at[pl.ds(0, ts * PACK_ROWS), :], sem.at[k]).wait()


def _dispatch(dest0, dest1, hn, n_rows, ts):
    n = hn.shape[0] // PACK_ROWS
    return pl.pallas_call(
        _dispatch_kernel,
        grid=(n // ts,),
        in_specs=[
            pl.BlockSpec((1, 1, ts), lambda i: (i, 0, 0), memory_space=pltpu.SMEM),
            pl.BlockSpec((1, 1, ts), lambda i: (i, 0, 0), memory_space=pltpu.SMEM),
            pl.BlockSpec((ts * PACK_ROWS, LANES), lambda i: (i, 0)),
            pl.BlockSpec(memory_space=pl.ANY),
        ],
        out_specs=pl.BlockSpec(memory_space=pl.ANY),
        out_shape=jax.ShapeDtypeStruct((n_rows * PACK_ROWS, LANES), U32),
        scratch_shapes=[pltpu.SemaphoreType.DMA((2,))],
        input_output_aliases={3: 0},
        compiler_params=_params("arbitrary"),
        name="dispatch",
    )(dest0, dest1, hn, jnp.zeros((n_rows * PACK_ROWS, LANES), U32))


def _experts_kernel(be_ref, nu_ref, xb_ref, wg_ref, wu_ref, wd_ref, yb_ref):
    i = pl.program_id(0)

    @pl.when(i < nu_ref[0])
    def _():
        pairs = [_unpack_bf16_pair(xb_ref[pl.ds(j, EXPERT_BLOCK, stride=PACK_ROWS), :]) for j in range(PACK_ROWS)]
        x = jnp.concatenate([lo for lo, _ in pairs] + [hi for _, hi in pairs], axis=1)
        a = (_silu(_dot(x, wg_ref[...])) * _dot(x, wu_ref[...])).astype(BF16)
        y = _dot(a, wd_ref[...])
        for j in range(OUT_ROWS):
            yb_ref[pl.ds(j, EXPERT_BLOCK, stride=OUT_ROWS), :] = y[:, j * LANES:(j + 1) * LANES]

    @pl.when(i >= nu_ref[0])
    def _():
        yb_ref[...] = jnp.zeros_like(yb_ref)


def _experts(block_e, n_used, xb, w_gate, w_up, w_down):
    n_rows = xb.shape[0] // PACK_ROWS
    n_blocks = n_rows // EXPERT_BLOCK
    xrow = lambda i, be, nu: (jnp.maximum(jnp.minimum(i, nu[0] - 1), 0), 0)
    wsel = lambda i, be, nu: (be[i], 0, 0)
    grid_spec = pltpu.PrefetchScalarGridSpec(
        num_scalar_prefetch=2,
        grid=(n_blocks,),
        in_specs=[
            pl.BlockSpec((EXPERT_BLOCK * PACK_ROWS, LANES), xrow),
            pl.BlockSpec((None, D_MODEL, D_EXPERT), wsel),
            pl.BlockSpec((None, D_MODEL, D_EXPERT), wsel),
            pl.BlockSpec((None, D_EXPERT, D_MODEL), wsel),
        ],
        out_specs=pl.BlockSpec((EXPERT_BLOCK * OUT_ROWS, LANES), lambda i, be, nu: (i, 0)),
    )
    return pl.pallas_call(
        _experts_kernel,
        grid_spec=grid_spec,
        out_shape=jax.ShapeDtypeStruct((n_rows * OUT_ROWS, LANES), F32),
        compiler_params=_params("arbitrary"),
        name="experts",
    )(block_e, n_used, xb, w_gate, w_up, w_down)


def _combine_kernel(d0_ref, d1_ref, d0n_ref, d1n_ref, x1_ref, p_ref, comb_ref, gpl_ref, wplg_ref, wplp_ref,
                    gfin_ref, yb_ref, out_ref, gbuf, sem):
    tf = x1_ref.shape[0]
    i = pl.program_id(0)
    n_steps = pl.num_programs(0)
    slot = i & 1

    def start_gathers(d_refs, s):
        def body(t, carry):
            row = pl.multiple_of(t * OUT_ROWS, OUT_ROWS)
            for k, d_ref in enumerate(d_refs):
                src = pl.multiple_of(d_ref[0, 0, t] * OUT_ROWS, OUT_ROWS)
                pltpu.make_async_copy(yb_ref.at[pl.ds(src, OUT_ROWS), :], gbuf.at[s, k, pl.ds(row, OUT_ROWS), :],
                                      sem.at[s, k]).start(priority=k)
            return carry

        lax.fori_loop(0, tf, body, 0, unroll=DMA_UNROLL)

    @pl.when(i == 0)
    def _():
        start_gathers((d0_ref, d1_ref), 0)

    @pl.when(i + 1 < n_steps)
    def _():
        start_gathers((d0n_ref, d1n_ref), 1 - slot)

    for k in range(2):
        pltpu.make_async_copy(yb_ref.at[pl.ds(0, tf * OUT_ROWS), :], gbuf.at[slot, k], sem.at[slot, k]).wait()

    comb = comb_ref[...]
    g1, g2 = (jnp.concatenate([gbuf[slot, k, pl.ds(j, tf, stride=OUT_ROWS), :] for j in range(OUT_ROWS)], axis=1)
              for k in range(2))
    y = comb[:, 0:1] * g1 + comb[:, 1:2] * g2
    x2 = x1_ref[...] + y
    gate = jax.nn.sigmoid(_dot(_rms(x2, gpl_ref[...]).astype(BF16), wplg_ref[...]))
    x3 = x2 + gate * _dot(p_ref[...].astype(BF16), wplp_ref[...])
    out_ref[...] = _rms(x3, gfin_ref[...])


def _combine(dest0, dest1, x1, p2, comb, g_pl, w_plg, w_plp, g_final, yb, tf):
    n = x1.shape[0]
    n_steps = n // tf
    const = lambda i: (0, 0)
    rows = lambda i: (i, 0)
    cur = lambda i: (i, 0, 0)
    nxt = lambda i: (jnp.minimum(i + 1, n_steps - 1), 0, 0)
    return pl.pallas_call(
        _combine_kernel,
        grid=(n_steps,),
        in_specs=[
            pl.BlockSpec((1, 1, tf), cur, memory_space=pltpu.SMEM),
            pl.BlockSpec((1, 1, tf), cur, memory_space=pltpu.SMEM),
            pl.BlockSpec((1, 1, tf), nxt, memory_space=pltpu.SMEM),
            pl.BlockSpec((1, 1, tf), nxt, memory_space=pltpu.SMEM),
            pl.BlockSpec((tf, D_MODEL), rows),
            pl.BlockSpec((tf, PLE_DIM), rows),
            pl.BlockSpec((tf, LANES), rows),
            pl.BlockSpec((1, D_MODEL), const),
            pl.BlockSpec((D_MODEL, D_MODEL), const),
            pl.BlockSpec((PLE_DIM, D_MODEL), const),
            pl.BlockSpec((1, D_MODEL), const),
            pl.BlockSpec(memory_space=pl.ANY),
        ],
        out_specs=pl.BlockSpec((tf, D_MODEL), rows),
        out_shape=jax.ShapeDtypeStruct((n, D_MODEL), F32),
        scratch_shapes=[pltpu.VMEM((2, 2, tf * OUT_ROWS, LANES), F32), pltpu.SemaphoreType.DMA((2, 2))],
        compiler_params=_params("arbitrary"),
        name="combine",
    )(dest0, dest1, dest0, dest1, x1, p2, comb, g_pl, w_plg, w_plp, g_final, yb)


def _layer(x2, p2, batch, seq, g_mix, w_in, b_mgate, conv_qk, g_mlstm, lb, g_hgrn, w_out, g_ffn,
           w_rg, b_rg, w_re, b_re, w_e_gate, w_e_up, w_e_down, g_pl, w_pl_gate, w_pl_proj, g_out):
    n = x2.shape[0]
    n_chunks = seq // CHUNK
    m_cols = 4 * W_MIX
    n_gate = 2 * N_HEADS

    w_in_b = w_in.astype(BF16)
    w_m = w_in_b[:, :m_cols]
    w_gcols = w_in_b[:, m_cols:m_cols + n_gate]
    w_h = w_in_b[:, m_cols + n_gate:]
    w_g = jnp.pad(w_gcols, ((0, 0), (0, LANES - n_gate)))
    b_g = jnp.pad(b_mgate.astype(F32), (0, LANES - n_gate))[None, :]
    zm, zh, gate, gate_t = _in_proj(x2, g_mix[None, :], w_m, w_h, w_g, w_gcols.T, b_g,
                                    b_mgate.astype(F32)[:, None], tm=min(512, n))

    y = _mixer(zm, zh, gate, gate_t, conv_qk, g_mlstm[None, :], lb[None, :], g_hgrn[None, :], batch, n_chunks)

    n_logit = N_GROUPS + N_EXPERTS
    w_rt = jnp.pad(jnp.concatenate([w_rg, w_re], axis=1).T, ((0, ROUTER_ROWS - n_logit), (0, 0)))
    wr_hi = w_rt.astype(BF16)
    wr_lo = (w_rt - wr_hi.astype(F32)).astype(BF16)
    b_r = jnp.pad(jnp.concatenate([b_rg, b_re]), (0, ROUTER_ROWS - n_logit))[:, None]
    x1, hn, route_t, comb, counts = _post_mix(x2, y, w_out.astype(BF16), g_ffn[None, :], wr_hi, wr_lo, b_r,
                                              tm=min(512, n))

    counts = counts[N_GROUPS:n_logit, 0].astype(I32)
    padded = (counts + EXPERT_BLOCK - 1) // EXPERT_BLOCK * EXPERT_BLOCK
    pend = jnp.cumsum(padded)
    pstart = pend - padded
    n_blocks = (2 * n) // EXPERT_BLOCK + N_EXPERTS
    n_rows = n_blocks * EXPERT_BLOCK
    n_used = (pend[-1] // EXPERT_BLOCK).astype(I32)
    block_row = jnp.minimum(jnp.arange(n_blocks, dtype=I32), n_used - 1) * EXPERT_BLOCK
    block_e = jnp.sum((pend[None, :] <= block_row[:, None]).astype(I32), axis=1)
    expert_id = route_t[0:2].astype(I32)
    hot = expert_id[:, :, None] == jnp.arange(N_EXPERTS, dtype=I32)
    dest = jnp.sum(jnp.where(hot, pstart, 0), axis=-1) + route_t[2:4].astype(I32)

    ts = min(512, n)
    xb = _dispatch(dest[0].reshape(n // ts, 1, ts), dest[1].reshape(n // ts, 1, ts), hn, n_rows, ts)
    yb = _experts(block_e, n_used[None], xb, w_e_gate.astype(BF16), w_e_up.astype(BF16), w_e_down.astype(BF16))
    tf = min(256, n)
    return _combine(dest[0].reshape(n // tf, 1, tf), dest[1].reshape(n // tf, 1, tf), x1, p2, comb, g_pl[None, :],
                    w_pl_gate.astype(BF16), w_pl_proj.astype(BF16), g_out[None, :], yb, tf)


def kernel(x, p, g_mix, w_in, b_mgate, conv_qk, g_mlstm, hg_lb, g_hgrn, w_out, g_ffn, w_rg, b_rg, w_re, b_re,
           w_e_gate, w_e_up, w_e_down, g_pl, w_pl_gate, w_pl_proj, g_final):
    batch, seq, d = x.shape
    depth = p.shape[0]
    assert depth == 1, "the fused final norm assumes a single layer"
    lower_bounds = jnp.cumsum(jax.nn.softmax(hg_lb.astype(F32), axis=0), axis=0)
    i = 0
    out = _layer(x.reshape(batch * seq, d), p[i].reshape(batch * seq, PLE_DIM), batch, seq,
                 g_mix[i], w_in[i], b_mgate[i], conv_qk[i], g_mlstm[i], lower_bounds[i], g_hgrn[i], w_out[i],
                 g_ffn[i], w_rg[i], b_rg[i], w_re[i], b_re[i], w_e_gate[i], w_e_up[i], w_e_down[i],
                 g_pl[i], w_pl_gate[i], w_pl_proj[i], g_final)
    return out.reshape(batch, seq, d)
```

```python
import jax
import jax.numpy as jnp
from jax import lax
from jax.experimental import pallas as pl
from jax.experimental.pallas import tpu as pltpu

F32 = jnp.float32
BF16 = jnp.bfloat16
I32 = jnp.int32
U32 = jnp.uint32
EPS = 1e-6
HIGHEST = lax.Precision.HIGHEST

LANES = 128
D_MODEL = 1024
W_MIX = 512
N_HEADS = 4
HEAD_DIM = 128
N_GROUPS = 4
EXPERTS_PER_GROUP = 8
N_EXPERTS = N_GROUPS * EXPERTS_PER_GROUP
D_EXPERT = 512
PLE_DIM = 256
CONV_WIDTH = 4
CHUNK = 128
EXPERT_BLOCK = 512
PACK_ROWS = D_MODEL // 2 // LANES
OUT_ROWS = D_MODEL // LANES
ROUTER_ROWS = 48
DMA_UNROLL = 4
COMBINE_COLS = 256
VMEM_LIMIT = 56 * 1024 * 1024


def _dot(a, b):
    return jnp.dot(a, b, preferred_element_type=F32)


def _dot_nt(a, b):
    return lax.dot_general(a, b, (((1,), (1,)), ((), ())), preferred_element_type=F32)


def _dot_tn(a, b):
    return lax.dot_general(a, b, (((0,), (0,)), ((), ())), preferred_element_type=F32)


def _rms(u, g):
    return u * lax.rsqrt(jnp.mean(u * u, axis=-1, keepdims=True) + EPS) * g


def _silu(u):
    return u * jax.nn.sigmoid(u)


def _log_sigmoid(u):
    return jnp.minimum(u, 0.0) - jnp.log1p(jnp.exp(-jnp.abs(u)))


def _params(*sem):
    return pltpu.CompilerParams(dimension_semantics=sem, vmem_limit_bytes=VMEM_LIMIT)


def _in_proj_kernel(x_ref, g_ref, wm_ref, wh_ref, wg_ref, wgt_ref, bg_ref, bgt_ref,
                    zm_ref, zh_ref, gate_ref, gate_t_ref):
    h = _rms(x_ref[...], g_ref[...]).astype(BF16)
    step = 512
    for c0 in range(0, 4 * W_MIX, step):
        zm_ref[:, c0:c0 + step] = _dot(h, wm_ref[:, c0:c0 + step]).astype(BF16)
        zh_ref[:, c0:c0 + step] = _dot(h, wh_ref[:, c0:c0 + step]).astype(BF16)
    gate_ref[...] = _dot(h, wg_ref[...]) + bg_ref[...]
    gate_t_ref[...] = _dot_nt(wgt_ref[...], h) + bgt_ref[...]


def _in_proj(x2, g_mix, w_m, w_h, w_g, w_gt, b_g, b_gt, tm):
    n = x2.shape[0]
    const = lambda i: (0, 0)
    return pl.pallas_call(
        _in_proj_kernel,
        grid=(n // tm,),
        in_specs=[
            pl.BlockSpec((tm, D_MODEL), lambda i: (i, 0)),
            pl.BlockSpec((1, D_MODEL), const),
            pl.BlockSpec((D_MODEL, 4 * W_MIX), const),
            pl.BlockSpec((D_MODEL, 4 * W_MIX), const),
            pl.BlockSpec((D_MODEL, LANES), const),
            pl.BlockSpec((8, D_MODEL), const),
            pl.BlockSpec((1, LANES), const),
            pl.BlockSpec((8, 1), const),
        ],
        out_specs=[
            pl.BlockSpec((tm, 4 * W_MIX), lambda i: (i, 0)),
            pl.BlockSpec((tm, 4 * W_MIX), lambda i: (i, 0)),
            pl.BlockSpec((tm, LANES), lambda i: (i, 0)),
            pl.BlockSpec((8, tm), lambda i: (0, i)),
        ],
        out_shape=[
            jax.ShapeDtypeStruct((n, 4 * W_MIX), BF16),
            jax.ShapeDtypeStruct((n, 4 * W_MIX), BF16),
            jax.ShapeDtypeStruct((n, LANES), F32),
            jax.ShapeDtypeStruct((8, n), F32),
        ],
        compiler_params=_params("parallel"),
        name="in_proj",
    )(x2, g_mix, w_m, w_h, w_g, w_gt, b_g, b_gt)


def _mlstm_chunk(zm_ref, gate_ref, gate_t_ref, conv_ref, gn_ref, y_ref, cbuf, c_ref, n_ref, m_ref):
    L = CHUNK

    cbuf[8:8 + L, :] = zm_ref[:, 0:2 * W_MIX].astype(F32)
    acc = cbuf[8:8 + L, :] * conv_ref[CONV_WIDTH - 1:CONV_WIDTH, :]
    for j in range(CONV_WIDTH - 1):
        acc = acc + cbuf[5 + j:5 + j + L, :] * conv_ref[j:j + 1, :]
    cbuf[0:8, :] = cbuf[L:L + 8, :]
    qk = _silu(acc)

    gate = gate_ref[...]
    gate_t = gate_t_ref[...]
    row = lax.broadcasted_iota(I32, (L, L), 0)
    col = lax.broadcasted_iota(I32, (L, L), 1)
    causal = col <= row
    tri = causal.astype(F32)
    bcum_c = jnp.dot(tri, _log_sigmoid(gate), precision=HIGHEST, preferred_element_type=F32)
    bcum_r = jnp.dot(_log_sigmoid(gate_t), (row <= col).astype(F32), precision=HIGHEST,
                     preferred_element_type=F32)

    for h in range(N_HEADS):
        hs = slice(h * HEAD_DIM, (h + 1) * HEAD_DIM)
        q = qk[:, h * HEAD_DIM:(h + 1) * HEAD_DIM]
        k = qk[:, W_MIX + h * HEAD_DIM:W_MIX + (h + 1) * HEAD_DIM] * (HEAD_DIM ** -0.5)
        v = zm_ref[:, 2 * W_MIX + h * HEAD_DIM:2 * W_MIX + (h + 1) * HEAD_DIM]
        o_pre = zm_ref[:, 3 * W_MIX + h * HEAD_DIM:3 * W_MIX + (h + 1) * HEAD_DIM].astype(F32)
        qb = q.astype(BF16)

        bc = bcum_c[:, N_HEADS + h:N_HEADS + h + 1]
        ic = gate[:, h:h + 1]
        br = bcum_r[N_HEADS + h:N_HEADS + h + 1, :]
        ir = gate_t[h:h + 1, :]
        m_prev = m_ref[h:h + 1, 0:1]
        b_last = bc[L - 1:L, :]

        log_d = jnp.where(causal, bc - br + ir, -jnp.inf)
        inter = bc + m_prev
        m_t = jnp.maximum(inter, jnp.max(log_d, axis=-1, keepdims=True))
        s = _dot_nt(qb, k.astype(BF16)) * jnp.exp(log_d - m_t)
        w_inter = jnp.exp(inter - m_t)
        num = _dot(s.astype(BF16), v) + w_inter * _dot(qb, c_ref[h].astype(BF16))
        den = jnp.sum(s, axis=-1, keepdims=True) + w_inter * jnp.sum(q * n_ref[h:h + 1, :], axis=-1, keepdims=True)
        hh = num * (1.0 / jnp.maximum(jnp.abs(den), jnp.exp(-m_t)))

        w_log = b_last - bc + ic
        m_new = jnp.maximum(b_last + m_prev, jnp.max(w_log, axis=0, keepdims=True))
        kw = k * jnp.exp(w_log - m_new)
        decay = jnp.exp(b_last + m_prev - m_new)
        c_ref[h] = decay * c_ref[h] + _dot_tn(kw.astype(BF16), v)
        n_ref[h:h + 1, :] = decay * n_ref[h:h + 1, :] + jnp.sum(kw, axis=0, keepdims=True)
        m_ref[h:h + 1, :] = jnp.broadcast_to(m_new, (1, LANES))

        hh = hh * jax.nn.sigmoid(o_pre)
        hh = hh * lax.rsqrt(jnp.mean(hh * hh, axis=-1, keepdims=True) + EPS) * gn_ref[:, hs]
        y_ref[:, hs] = hh.astype(BF16)


def _rows_bcast(ref, rows, span, hs):
    return jnp.concatenate([jnp.broadcast_to(ref[r:r + 1, hs], (span, HEAD_DIM)) for r in rows], axis=0)


def _hgrn_chunk(zh_ref, lb_ref, gn_ref, y_ref, st_ref, b_scr):
    L = CHUNK

    lb = lb_ref[...]
    f_pre = zh_ref[:, W_MIX:2 * W_MIX].astype(F32)
    lf = jnp.log(lb + (1.0 - lb) * jax.nn.sigmoid(f_pre))
    k_all = (1.0 - lb) * jax.nn.sigmoid(-f_pre)
    row = lax.broadcasted_iota(I32, (L, L), 0)
    col = lax.broadcasted_iota(I32, (L, L), 1)
    b_scr[...] = jnp.dot((col <= row).astype(F32), lf, precision=HIGHEST, preferred_element_type=F32)

    for h in range(N_HEADS):
        hs = slice(h * HEAD_DIM, (h + 1) * HEAD_DIM)
        b = b_scr[:, hs]
        q = _silu(zh_ref[:, hs].astype(F32))
        k = k_all[:, hs]
        v = zh_ref[:, 2 * W_MIX + h * HEAD_DIM:2 * W_MIX + (h + 1) * HEAD_DIM]
        b_last = b_scr[L - 1:L, hs]

        o = _dot_nt((q * jnp.exp(b)).astype(BF16), st_ref[h].astype(BF16))

        a = jnp.zeros((L, L), F32)
        span = L // 2
        while span >= 16:
            mids = range(span, L, 2 * span)
            r = _rows_bcast(b_scr, [m for m in mids for _ in (0, 1)], span, hs)
            lower = (row & span) != 0
            e = jnp.exp(jnp.where(lower, b - r, r - b))
            qt = jnp.where(lower, q * e, 0.0).astype(BF16)
            kt = jnp.where(lower, 0.0, k * e).astype(BF16)
            p = _dot_nt(qt, kt)
            same = (row & -(2 * span)) == (col & -(2 * span))
            a = a + jnp.where(same, p, 0.0)
            span //= 2
        r = _rows_bcast(b_scr, range(0, L, 16), 16, hs)
        p = _dot_nt((q * jnp.exp(b - r)).astype(BF16), (k * jnp.exp(r - b)).astype(BF16))
        a = a + jnp.where(((row & -16) == (col & -16)) & (col <= row), p, 0.0)

        o = o + _dot(a.astype(BF16), v)

        ke = (k * jnp.exp(b_last - b)).astype(BF16)
        st_ref[h] = jnp.exp(b_last) * st_ref[h] + _dot_tn(v, ke)

        o = o * lax.rsqrt(jnp.mean(o * o, axis=-1, keepdims=True) + EPS) * gn_ref[...]
        o = o * _silu(zh_ref[:, 3 * W_MIX + h * HEAD_DIM:3 * W_MIX + (h + 1) * HEAD_DIM].astype(F32))
        y_ref[:, W_MIX + h * HEAD_DIM:W_MIX + (h + 1) * HEAD_DIM] = o.astype(BF16)


def _mixer_kernel(zm_ref, zh_ref, gate_ref, gate_t_ref, conv_ref, gm_ref, lb_ref, gh_ref, y_ref,
                  cbuf, c_ref, n_ref, m_ref, st_ref, b_scr):
    @pl.when(pl.program_id(1) == 0)
    def _():
        cbuf[0:8, :] = jnp.zeros((8, 2 * W_MIX), F32)
        c_ref[...] = jnp.zeros_like(c_ref)
        n_ref[...] = jnp.zeros_like(n_ref)
        m_ref[...] = jnp.zeros_like(m_ref)
        st_ref[...] = jnp.zeros_like(st_ref)

    _mlstm_chunk(zm_ref, gate_ref, gate_t_ref, conv_ref, gm_ref, y_ref, cbuf, c_ref, n_ref, m_ref)
    _hgrn_chunk(zh_ref, lb_ref, gh_ref, y_ref, st_ref, b_scr)


def _mixer(zm, zh, gate, gate_t, conv_w, g_mlstm, lb, g_hgrn, batch, n_chunks):
    n = zm.shape[0]
    rows = lambda b, c: (b * n_chunks + c, 0)
    const = lambda b, c: (0, 0)
    return pl.pallas_call(
        _mixer_kernel,
        grid=(batch, n_chunks),
        in_specs=[
            pl.BlockSpec((CHUNK, 4 * W_MIX), rows),
            pl.BlockSpec((CHUNK, 4 * W_MIX), rows),
            pl.BlockSpec((CHUNK, LANES), rows),
            pl.BlockSpec((8, CHUNK), lambda b, c: (0, b * n_chunks + c)),
            pl.BlockSpec((CONV_WIDTH, 2 * W_MIX), const),
            pl.BlockSpec((1, W_MIX), const),
            pl.BlockSpec((1, W_MIX), const),
            pl.BlockSpec((1, HEAD_DIM), const),
        ],
        out_specs=pl.BlockSpec((CHUNK, 2 * W_MIX), rows),
        out_shape=jax.ShapeDtypeStruct((n, 2 * W_MIX), BF16),
        scratch_shapes=[
            pltpu.VMEM((CHUNK + 8, 2 * W_MIX), F32),
            pltpu.VMEM((N_HEADS, HEAD_DIM, HEAD_DIM), F32),
            pltpu.VMEM((8, HEAD_DIM), F32),
            pltpu.VMEM((8, LANES), F32),
            pltpu.VMEM((N_HEADS, HEAD_DIM, HEAD_DIM), F32),
            pltpu.VMEM((CHUNK, W_MIX), F32),
        ],
        compiler_params=_params("parallel", "arbitrary"),
        name="mixer",
    )(zm, zh, gate, gate_t, conv_w, g_mlstm, lb, g_hgrn)


def _pack_bf16_pair(lo, hi):
    lo_bits = pltpu.bitcast(lo.astype(BF16).astype(F32), U32)
    hi_bits = pltpu.bitcast(hi.astype(BF16).astype(F32), U32)
    return (hi_bits & jnp.uint32(0xFFFF0000)) | (lo_bits >> 16)


def _unpack_bf16_pair(w):
    lo = pltpu.bitcast(w << 16, F32).astype(BF16)
    hi = pltpu.bitcast(w & jnp.uint32(0xFFFF0000), F32).astype(BF16)
    return lo, hi


def _post_mix_kernel(x_ref, y_ref, wo_ref, g_ref, wrh_ref, wrl_ref, br_ref,
                     x1_ref, hn_ref, route_t_ref, comb_ref, count_ref, run_ref):
    tm = x_ref.shape[0]
    n_r = wrh_ref.shape[0]

    @pl.when(pl.program_id(0) == 0)
    def _():
        run_ref[...] = jnp.zeros_like(run_ref)

    x1 = x_ref[...] + _dot(y_ref[...], wo_ref[...])
    x1_ref[...] = x1
    hn = _rms(x1, g_ref[...])
    half = D_MODEL // 2
    packed = _pack_bf16_pair(hn[:, :half], hn[:, half:])
    for j in range(PACK_ROWS):
        hn_ref[pl.ds(j, tm, stride=PACK_ROWS), :] = packed[:, j * LANES:(j + 1) * LANES]

    hn_hi = hn.astype(BF16)
    hn_lo = (hn - hn_hi.astype(F32)).astype(BF16)
    logits = (_dot_nt(wrh_ref[...], hn_hi) + (_dot_nt(wrl_ref[...], hn_hi) + _dot_nt(wrh_ref[...], hn_lo))
              + br_ref[...])
    rix = lax.broadcasted_iota(I32, (n_r, tm), 0)
    neg = -jnp.inf
    g_l = jnp.where(rix < N_GROUPS, logits, neg)
    g_max = jnp.max(g_l, axis=0, keepdims=True)
    g_sel = jnp.min(jnp.where(g_l == g_max, rix, n_r), axis=0, keepdims=True)
    g_val = 1.0 / jnp.sum(jnp.exp(g_l - g_max), axis=0, keepdims=True)

    e_row = rix - N_GROUPS
    in_group = (e_row >= g_sel * EXPERTS_PER_GROUP) & (e_row < (g_sel + 1) * EXPERTS_PER_GROUP)
    e_l = jnp.where(in_group, logits, neg)
    v1 = jnp.max(e_l, axis=0, keepdims=True)
    i1 = jnp.min(jnp.where(e_l == v1, rix, n_r), axis=0, keepdims=True)
    e_l2 = jnp.where(rix == i1, neg, e_l)
    v2 = jnp.max(e_l2, axis=0, keepdims=True)
    i2 = jnp.min(jnp.where(e_l2 == v2, rix, n_r), axis=0, keepdims=True)
    t = jnp.exp(v2 - v1)
    c1 = g_val / (1.0 + t)
    c2 = g_val * t / (1.0 + t)

    hot1 = rix == i1
    hot2 = rix == i2
    hot = (hot1 | hot2).astype(F32)
    r_i = lax.broadcasted_iota(I32, (tm, tm), 0)
    c_i = lax.broadcasted_iota(I32, (tm, tm), 1)
    before = _dot(hot.astype(BF16), (r_i < c_i).astype(BF16)) + run_ref[:, 0:1]
    rank1 = jnp.sum(jnp.where(hot1, before, 0.0), axis=0, keepdims=True)
    rank2 = jnp.sum(jnp.where(hot2, before, 0.0), axis=0, keepdims=True)
    run_ref[...] = run_ref[...] + jnp.sum(hot, axis=1, keepdims=True)
    count_ref[...] = run_ref[...]

    r8 = lax.broadcasted_iota(I32, (8, tm), 0)
    out = jnp.where(r8 == 0, (i1 - N_GROUPS).astype(F32), 0.0)
    out = jnp.where(r8 == 1, (i2 - N_GROUPS).astype(F32), out)
    out = jnp.where(r8 == 2, rank1, out)
    out = jnp.where(r8 == 3, rank2, out)
    route_t_ref[...] = out

    r128 = lax.broadcasted_iota(I32, (LANES, tm), 0)
    slab = jnp.where(r128 == 0, c1, jnp.where(r128 == 1, c2, 0.0))
    for c in range(tm // LANES):
        comb_ref[c * LANES:(c + 1) * LANES, :] = slab[:, c * LANES:(c + 1) * LANES].T


def _post_mix(x2, y, w_out, g_ffn, wr_hi, wr_lo, b_r, tm):
    n = x2.shape[0]
    n_r = wr_hi.shape[0]
    const = lambda i: (0, 0)
    rows = lambda i: (i, 0)
    return pl.pallas_call(
        _post_mix_kernel,
        grid=(n // tm,),
        in_specs=[
            pl.BlockSpec((tm, D_MODEL), rows),
            pl.BlockSpec((tm, 2 * W_MIX), rows),
            pl.BlockSpec((2 * W_MIX, D_MODEL), const),
            pl.BlockSpec((1, D_MODEL), const),
            pl.BlockSpec((n_r, D_MODEL), const),
            pl.BlockSpec((n_r, D_MODEL), const),
            pl.BlockSpec((n_r, 1), const),
        ],
        out_specs=[
            pl.BlockSpec((tm, D_MODEL), rows),
            pl.BlockSpec((tm * PACK_ROWS, LANES), rows),
            pl.BlockSpec((8, tm), lambda i: (0, i)),
            pl.BlockSpec((tm, LANES), rows),
            pl.BlockSpec((n_r, LANES), const),
        ],
        out_shape=[
            jax.ShapeDtypeStruct((n, D_MODEL), F32),
            jax.ShapeDtypeStruct((n * PACK_ROWS, LANES), U32),
            jax.ShapeDtypeStruct((8, n), F32),
            jax.ShapeDtypeStruct((n, LANES), F32),
            jax.ShapeDtypeStruct((n_r, LANES), F32),
        ],
        scratch_shapes=[pltpu.VMEM((n_r, LANES), F32)],
        compiler_params=_params("arbitrary"),
        name="post_mix",
    )(x2, y, w_out, g_ffn, wr_hi, wr_lo, b_r)


def _dispatch_kernel(d0_ref, d1_ref, hn_ref, xb_init_ref, xb_ref, sem):
    del xb_init_ref
    ts = hn_ref.shape[0] // PACK_ROWS

    def start(t, carry):
        src = pl.multiple_of(t * PACK_ROWS, PACK_ROWS)
        for k, d_ref in enumerate((d0_ref, d1_ref)):
            dst = pl.multiple_of(d_ref[0, 0, t] * PACK_ROWS, PACK_ROWS)
            pltpu.make_async_copy(hn_ref.at[pl.ds(src, PACK_ROWS), :], xb_ref.at[pl.ds(dst, PACK_ROWS), :],
                                  sem.at[k]).start(priority=k)
        return carry

    lax.fori_loop(0, ts, start, 0, unroll=DMA_UNROLL)
    for k in range(2):
        pltpu.make_async_copy(hn_ref, xb_ref.at[pl.ds(0, ts * PACK_ROWS), :], sem.at[k]).wait()


def _dispatch(dest0, dest1, hn, n_rows, ts):
    n = hn.shape[0] // PACK_ROWS
    return pl.pallas_call(
        _dispatch_kernel,
        grid=(n // ts,),
        in_specs=[
            pl.BlockSpec((1, 1, ts), lambda i: (i, 0, 0), memory_space=pltpu.SMEM),
            pl.BlockSpec((1, 1, ts), lambda i: (i, 0, 0), memory_space=pltpu.SMEM),
            pl.BlockSpec((ts * PACK_ROWS, LANES), lambda i: (i, 0)),
            pl.BlockSpec(memory_space=pl.ANY),
        ],
        out_specs=pl.BlockSpec(memory_space=pl.ANY),
        out_shape=jax.ShapeDtypeStruct((n_rows * PACK_ROWS, LANES), U32),
        scratch_shapes=[pltpu.SemaphoreType.DMA((2,))],
        input_output_aliases={3: 0},
        compiler_params=_params("arbitrary"),
        name="dispatch",
    )(dest0, dest1, hn, jnp.zeros((n_rows * PACK_ROWS, LANES), U32))


def _experts_kernel(be_ref, bv_ref, nu_ref, xb_ref, wg_ref, wu_ref, wd_ref, yb_ref):
    valid = bv_ref[pl.program_id(0)]
    half = EXPERT_BLOCK // 2

    def mlp(rows):
        pairs = [_unpack_bf16_pair(xb_ref[pl.ds(j, rows, stride=PACK_ROWS), :]) for j in range(PACK_ROWS)]
        x = jnp.concatenate([lo for lo, _ in pairs] + [hi for _, hi in pairs], axis=1)
        a = (_silu(_dot(x, wg_ref[...])) * _dot(x, wu_ref[...])).astype(BF16)
        y = _dot(a, wd_ref[...])
        for j in range(OUT_ROWS):
            yb_ref[pl.ds(j, rows, stride=OUT_ROWS), :] = y[:, j * LANES:(j + 1) * LANES]

    @pl.when(valid > half)
    def _():
        mlp(EXPERT_BLOCK)

    @pl.when((valid > 0) & (valid <= half))
    def _():
        mlp(half)
        yb_ref[half * OUT_ROWS:, :] = jnp.zeros((half * OUT_ROWS, LANES), F32)

    @pl.when(valid == 0)
    def _():
        yb_ref[...] = jnp.zeros_like(yb_ref)


def _experts(block_e, block_valid, n_used, xb, w_gate, w_up, w_down):
    n_rows = xb.shape[0] // PACK_ROWS
    n_blocks = n_rows // EXPERT_BLOCK
    xrow = lambda i, be, bv, nu: (jnp.maximum(jnp.minimum(i, nu[0] - 1), 0), 0)
    wsel = lambda i, be, bv, nu: (be[i], 0, 0)
    grid_spec = pltpu.PrefetchScalarGridSpec(
        num_scalar_prefetch=3,
        grid=(n_blocks,),
        in_specs=[
            pl.BlockSpec((EXPERT_BLOCK * PACK_ROWS, LANES), xrow),
            pl.BlockSpec((None, D_MODEL, D_EXPERT), wsel),
            pl.BlockSpec((None, D_MODEL, D_EXPERT), wsel),
            pl.BlockSpec((None, D_EXPERT, D_MODEL), wsel),
        ],
        out_specs=pl.BlockSpec((EXPERT_BLOCK * OUT_ROWS, LANES), lambda i, be, bv, nu: (i, 0)),
    )
    return pl.pallas_call(
        _experts_kernel,
        grid_spec=grid_spec,
        out_shape=jax.ShapeDtypeStruct((n_rows * OUT_ROWS, LANES), F32),
        compiler_params=_params("arbitrary"),
        name="experts",
    )(block_e, block_valid, n_used, xb, w_gate, w_up, w_down)


def _combine_kernel(d0_ref, d1_ref, d0n_ref, d1n_ref, x1_ref, p_ref, comb_ref, gpl_ref, wplg_ref, wplp_ref,
                    gfin_ref, yb_ref, out_ref, gbuf, x3_ref, sem):
    tf = x1_ref.shape[0]
    i = pl.program_id(0)
    n_steps = pl.num_programs(0)
    slot = i & 1

    def start_gather(d_refs, s, t):
        row = pl.multiple_of(t * OUT_ROWS, OUT_ROWS)
        for k, d_ref in enumerate(d_refs):
            src = pl.multiple_of(d_ref[0, 0, t] * OUT_ROWS, OUT_ROWS)
            pltpu.make_async_copy(yb_ref.at[pl.ds(src, OUT_ROWS), :], gbuf.at[s, k, pl.ds(row, OUT_ROWS), :],
                                  sem.at[s, k]).start(priority=k)

    def wait_gathers(s):
        for k in range(2):
            pltpu.make_async_copy(yb_ref.at[pl.ds(0, tf * OUT_ROWS), :], gbuf.at[s, k], sem.at[s, k]).wait()

    @pl.when(i == 0)
    def _():
        lax.fori_loop(0, tf, lambda t, c: (start_gather((d0_ref, d1_ref), 0, t), c)[1], 0, unroll=DMA_UNROLL)

    wait_gathers(slot)

    comb = comb_ref[...]
    g1, g2 = (jnp.concatenate([gbuf[slot, k, pl.ds(j, tf, stride=OUT_ROWS), :] for j in range(OUT_ROWS)], axis=1)
              for k in range(2))
    y = comb[:, 0:1] * g1 + comb[:, 1:2] * g2
    x2 = x1_ref[...] + y
    hb = _rms(x2, gpl_ref[...]).astype(BF16)
    pb = p_ref[...].astype(BF16)
    n_col = D_MODEL // COMBINE_COLS
    for c in range(n_col):
        for t in range(c * tf // n_col, (c + 1) * tf // n_col):
            start_gather((d0n_ref, d1n_ref), 1 - slot, t)
        cs = slice(c * COMBINE_COLS, (c + 1) * COMBINE_COLS)
        gate = jax.nn.sigmoid(_dot(hb, wplg_ref[:, cs]))
        x3_ref[:, cs] = x2[:, cs] + gate * _dot(pb, wplp_ref[:, cs])
    x3 = x3_ref[...]
    out_ref[...] = _rms(x3, gfin_ref[...])

    @pl.when(i == n_steps - 1)
    def _():
        wait_gathers(1 - slot)


def _combine(dest0, dest1, x1, p2, comb, g_pl, w_plg, w_plp, g_final, yb, tf):
    n = x1.shape[0]
    n_steps = n // tf
    const = lambda i: (0, 0)
    rows = lambda i: (i, 0)
    cur = lambda i: (i, 0, 0)
    nxt = lambda i: (jnp.minimum(i + 1, n_steps - 1), 0, 0)
    return pl.pallas_call(
        _combine_kernel,
        grid=(n_steps,),
        in_specs=[
            pl.BlockSpec((1, 1, tf), cur, memory_space=pltpu.SMEM),
            pl.BlockSpec((1, 1, tf), cur, memory_space=pltpu.SMEM),
            pl.BlockSpec((1, 1, tf), nxt, memory_space=pltpu.SMEM),
            pl.BlockSpec((1, 1, tf), nxt, memory_space=pltpu.SMEM),
            pl.BlockSpec((tf, D_MODEL), rows),
            pl.BlockSpec((tf, PLE_DIM), rows),
            pl.BlockSpec((tf, LANES), rows),
            pl.BlockSpec((1, D_MODEL), const),
            pl.BlockSpec((D_MODEL, D_MODEL), const),
            pl.BlockSpec((PLE_DIM, D_MODEL), const),
            pl.BlockSpec((1, D_MODEL), const),
            pl.BlockSpec(memory_space=pl.ANY),
        ],
        out_specs=pl.BlockSpec((tf, D_MODEL), rows),
        out_shape=jax.ShapeDtypeStruct((n, D_MODEL), F32),
        scratch_shapes=[pltpu.VMEM((2, 2, tf * OUT_ROWS, LANES), F32), pltpu.VMEM((tf, D_MODEL), F32),
                        pltpu.SemaphoreType.DMA((2, 2))],
        compiler_params=_params("arbitrary"),
        name="combine",
    )(dest0, dest1, dest0, dest1, x1, p2, comb, g_pl, w_plg, w_plp, g_final, yb)


def _layer(x2, p2, batch, seq, g_mix, w_in, b_mgate, conv_qk, g_mlstm, lb, g_hgrn, w_out, g_ffn,
           w_rg, b_rg, w_re, b_re, w_e_gate, w_e_up, w_e_down, g_pl, w_pl_gate, w_pl_proj, g_out):
    n = x2.shape[0]
    n_chunks = seq // CHUNK
    m_cols = 4 * W_MIX
    n_gate = 2 * N_HEADS

    w_in_b = w_in.astype(BF16)
    w_m = w_in_b[:, :m_cols]
    w_gcols = w_in_b[:, m_cols:m_cols + n_gate]
    w_h = w_in_b[:, m_cols + n_gate:]
    w_g = jnp.pad(w_gcols, ((0, 0), (0, LANES - n_gate)))
    b_g = jnp.pad(b_mgate.astype(F32), (0, LANES - n_gate))[None, :]
    zm, zh, gate, gate_t = _in_proj(x2, g_mix[None, :], w_m, w_h, w_g, w_gcols.T, b_g,
                                    b_mgate.astype(F32)[:, None], tm=min(512, n))

    y = _mixer(zm, zh, gate, gate_t, conv_qk, g_mlstm[None, :], lb[None, :], g_hgrn[None, :], batch, n_chunks)

    n_logit = N_GROUPS + N_EXPERTS
    w_rt = jnp.pad(jnp.concatenate([w_rg, w_re], axis=1).T, ((0, ROUTER_ROWS - n_logit), (0, 0)))
    wr_hi = w_rt.astype(BF16)
    wr_lo = (w_rt - wr_hi.astype(F32)).astype(BF16)
    b_r = jnp.pad(jnp.concatenate([b_rg, b_re]), (0, ROUTER_ROWS - n_logit))[:, None]
    x1, hn, route_t, comb, counts = _post_mix(x2, y, w_out.astype(BF16), g_ffn[None, :], wr_hi, wr_lo, b_r,
                                              tm=min(512, n))

    counts = counts[N_GROUPS:n_logit, 0].astype(I32)
    padded = (counts + EXPERT_BLOCK - 1) // EXPERT_BLOCK * EXPERT_BLOCK
    pend = jnp.cumsum(padded)
    pstart = pend - padded
    n_blocks = (2 * n) // EXPERT_BLOCK + N_EXPERTS
    n_rows = n_blocks * EXPERT_BLOCK
    n_used = (pend[-1] // EXPERT_BLOCK).astype(I32)
    block_ids = jnp.arange(n_blocks, dtype=I32)
    block_row = jnp.minimum(block_ids, n_used - 1) * EXPERT_BLOCK
    block_e = jnp.sum((pend[None, :] <= block_row[:, None]).astype(I32), axis=1)
    token_end = jnp.sum(jnp.where(block_e[:, None] == jnp.arange(N_EXPERTS, dtype=I32), pstart + counts, 0), axis=1)
    block_valid = jnp.where(block_ids < n_used, jnp.clip(token_end - block_row, 0, EXPERT_BLOCK), 0).astype(I32)
    expert_id = route_t[0:2].astype(I32)
    hot = expert_id[:, :, None] == jnp.arange(N_EXPERTS, dtype=I32)
    dest = jnp.sum(jnp.where(hot, pstart, 0), axis=-1) + route_t[2:4].astype(I32)

    ts = min(512, n)
    xb = _dispatch(dest[0].reshape(n // ts, 1, ts), dest[1].reshape(n // ts, 1, ts), hn, n_rows, ts)
    yb = _experts(block_e, block_valid, n_used[None], xb, w_e_gate.astype(BF16), w_e_up.astype(BF16), w_e_down.astype(BF16))
    tf = min(256, n)
    return _combine(dest[0].reshape(n // tf, 1, tf), dest[1].reshape(n // tf, 1, tf), x1, p2, comb, g_pl[None, :],
                    w_pl_gate.astype(BF16), w_pl_proj.astype(BF16), g_out[None, :], yb, tf)


def kernel(x, p, g_mix, w_in, b_mgate, conv_qk, g_mlstm, hg_lb, g_hgrn, w_out, g_ffn, w_rg, b_rg, w_re, b_re,
           w_e_gate, w_e_up, w_e_down, g_pl, w_pl_gate, w_pl_proj, g_final):
    batch, seq, d = x.shape
    depth = p.shape[0]
    assert depth == 1, "the fused final norm assumes a single layer"
    lower_bounds = jnp.cumsum(jax.nn.softmax(hg_lb.astype(F32), axis=0), axis=0)
    i = 0
    out = _layer(x.reshape(batch * seq, d), p[i].reshape(batch * seq, PLE_DIM), batch, seq,
                 g_mix[i], w_in[i], b_mgate[i], conv_qk[i], g_mlstm[i], lower_bounds[i], g_hgrn[i], w_out[i],
                 g_ffn[i], w_rg[i], b_rg[i], w_re[i], b_re[i], w_e_gate[i], w_e_up[i], w_e_down[i],
                 g_pl[i], w_pl_gate[i], w_pl_proj[i], g_final)
    return out.reshape(batch, seq, d)
```

```python
import jax
import jax.numpy as jnp
from jax import lax
from jax.experimental import pallas as pl
from jax.experimental.pallas import tpu as pltpu

F32 = jnp.float32
BF16 = jnp.bfloat16
I32 = jnp.int32
U32 = jnp.uint32
EPS = 1e-6

LANES = 128
D_MODEL = 1024
W_MIX = 512
N_HEADS = 4
HEAD_DIM = 128
N_GROUPS = 4
EXPERTS_PER_GROUP = 8
N_EXPERTS = N_GROUPS * EXPERTS_PER_GROUP
D_EXPERT = 512
PLE_DIM = 256
CONV_WIDTH = 4
CHUNK = 128
EXPERT_BLOCK = 512
PACK_ROWS = D_MODEL // 2 // LANES
OUT_ROWS = D_MODEL // LANES
ROUTER_ROWS = 48
DMA_UNROLL = 4
COMBINE_COLS = 256
MIX_BATCH = 2
GATHER_BUFS = 3
VMEM_LIMIT = 56 * 1024 * 1024


def _dot(a, b):
    return jnp.dot(a, b, preferred_element_type=F32)


def _dot_nt(a, b):
    return lax.dot_general(a, b, (((1,), (1,)), ((), ())), preferred_element_type=F32)


def _dot_tn(a, b):
    return lax.dot_general(a, b, (((0,), (0,)), ((), ())), preferred_element_type=F32)


def _rms(u, g):
    return u * lax.rsqrt(jnp.mean(u * u, axis=-1, keepdims=True) + EPS) * g


def _silu(u):
    return u * jax.nn.sigmoid(u)


def _log_sigmoid(u):
    return jnp.minimum(u, 0.0) - jnp.log1p(jnp.exp(-jnp.abs(u)))


def _split_hi_lo(u):
    hi = u.astype(BF16)
    return hi, (u - hi.astype(F32)).astype(BF16)


def _params(*sem):
    return pltpu.CompilerParams(dimension_semantics=sem, vmem_limit_bytes=VMEM_LIMIT)


def _in_proj_kernel(x_ref, g_ref, wm_ref, wh_ref, wg_ref, wgt_ref, bg_ref, bgt_ref,
                    zm_ref, zh_ref, gate_ref, gate_t_ref):
    h = _rms(x_ref[...], g_ref[...]).astype(BF16)
    step = 512
    for c0 in range(0, 4 * W_MIX, step):
        zm_ref[:, c0:c0 + step] = _dot(h, wm_ref[:, c0:c0 + step]).astype(BF16)
        zh_ref[:, c0:c0 + step] = _dot(h, wh_ref[:, c0:c0 + step]).astype(BF16)
    gate_ref[...] = _dot(h, wg_ref[...]) + bg_ref[...]
    gate_t_ref[...] = _dot_nt(wgt_ref[...], h) + bgt_ref[...]


def _in_proj(x2, g_mix, w_m, w_h, w_g, w_gt, b_g, b_gt, tm):
    n = x2.shape[0]
    const = lambda i: (0, 0)
    return pl.pallas_call(
        _in_proj_kernel,
        grid=(n // tm,),
        in_specs=[
            pl.BlockSpec((tm, D_MODEL), lambda i: (i, 0)),
            pl.BlockSpec((1, D_MODEL), const),
            pl.BlockSpec((D_MODEL, 4 * W_MIX), const),
            pl.BlockSpec((D_MODEL, 4 * W_MIX), const),
            pl.BlockSpec((D_MODEL, LANES), const),
            pl.BlockSpec((8, D_MODEL), const),
            pl.BlockSpec((1, LANES), const),
            pl.BlockSpec((8, 1), const),
        ],
        out_specs=[
            pl.BlockSpec((tm, 4 * W_MIX), lambda i: (i, 0)),
            pl.BlockSpec((tm, 4 * W_MIX), lambda i: (i, 0)),
            pl.BlockSpec((tm, LANES), lambda i: (i, 0)),
            pl.BlockSpec((8, tm), lambda i: (0, i)),
        ],
        out_shape=[
            jax.ShapeDtypeStruct((n, 4 * W_MIX), BF16),
            jax.ShapeDtypeStruct((n, 4 * W_MIX), BF16),
            jax.ShapeDtypeStruct((n, LANES), F32),
            jax.ShapeDtypeStruct((8, n), F32),
        ],
        compiler_params=_params("parallel"),
        name="in_proj",
    )(x2, g_mix, w_m, w_h, w_g, w_gt, b_g, b_gt)


def _mlstm_chunk(zm_ref, gate_ref, gate_t_ref, conv_ref, gn_ref, y_ref, cbuf, c_ref, n_ref, m_ref):
    L = CHUNK

    cbuf[8:8 + L, :] = zm_ref[:, 0:2 * W_MIX].astype(F32)
    acc = cbuf[8:8 + L, :] * conv_ref[CONV_WIDTH - 1:CONV_WIDTH, :]
    for j in range(CONV_WIDTH - 1):
        acc = acc + cbuf[5 + j:5 + j + L, :] * conv_ref[j:j + 1, :]
    cbuf[0:8, :] = cbuf[L:L + 8, :]
    qk = _silu(acc)

    gate = gate_ref[...]
    gate_t = gate_t_ref[...]
    row = lax.broadcasted_iota(I32, (L, L), 0)
    col = lax.broadcasted_iota(I32, (L, L), 1)
    causal = col <= row
    lf_c, lf_r = _split_hi_lo(_log_sigmoid(gate)), _split_hi_lo(_log_sigmoid(gate_t))
    lower, upper = causal.astype(BF16), (row <= col).astype(BF16)
    bcum_c = _dot(lower, lf_c[0]) + _dot(lower, lf_c[1])
    bcum_r = _dot(lf_r[0], upper) + _dot(lf_r[1], upper)

    for h in range(N_HEADS):
        hs = slice(h * HEAD_DIM, (h + 1) * HEAD_DIM)
        q = qk[:, h * HEAD_DIM:(h + 1) * HEAD_DIM]
        k = qk[:, W_MIX + h * HEAD_DIM:W_MIX + (h + 1) * HEAD_DIM] * (HEAD_DIM ** -0.5)
        v = zm_ref[:, 2 * W_MIX + h * HEAD_DIM:2 * W_MIX + (h + 1) * HEAD_DIM]
        o_pre = zm_ref[:, 3 * W_MIX + h * HEAD_DIM:3 * W_MIX + (h + 1) * HEAD_DIM].astype(F32)
        qb = q.astype(BF16)

        bc = bcum_c[:, N_HEADS + h:N_HEADS + h + 1]
        ic = gate[:, h:h + 1]
        br = bcum_r[N_HEADS + h:N_HEADS + h + 1, :]
        ir = gate_t[h:h + 1, :]
        m_prev = m_ref[h:h + 1, 0:1]
        b_last = bc[L - 1:L, :]

        log_d = jnp.where(causal, bc - br + ir, -jnp.inf)
        inter = bc + m_prev
        m_t = jnp.maximum(inter, jnp.max(log_d, axis=-1, keepdims=True))
        s = _dot_nt(qb, k.astype(BF16)) * jnp.exp(log_d - m_t)
        w_inter = jnp.exp(inter - m_t)
        num = _dot(s.astype(BF16), v) + w_inter * _dot(qb, c_ref[h].astype(BF16))
        den = jnp.sum(s, axis=-1, keepdims=True) + w_inter * jnp.sum(q * n_ref[h:h + 1, :], axis=-1, keepdims=True)
        hh = num * (1.0 / jnp.maximum(jnp.abs(den), jnp.exp(-m_t)))

        w_log = b_last - bc + ic
        m_new = jnp.maximum(b_last + m_prev, jnp.max(w_log, axis=0, keepdims=True))
        kw = k * jnp.exp(w_log - m_new)
        decay = jnp.exp(b_last + m_prev - m_new)
        c_ref[h] = decay * c_ref[h] + _dot_tn(kw.astype(BF16), v)
        n_ref[h:h + 1, :] = decay * n_ref[h:h + 1, :] + jnp.sum(kw, axis=0, keepdims=True)
        m_ref[h:h + 1, :] = jnp.broadcast_to(m_new, (1, LANES))

        hh = hh * jax.nn.sigmoid(o_pre)
        hh = hh * lax.rsqrt(jnp.mean(hh * hh, axis=-1, keepdims=True) + EPS) * gn_ref[:, hs]
        y_ref[:, hs] = hh.astype(BF16)


def _rows_bcast(ref, rows, span, hs):
    return jnp.concatenate([jnp.broadcast_to(ref[r:r + 1, hs], (span, HEAD_DIM)) for r in rows], axis=0)


def _hgrn_chunk(zh_ref, lb_ref, gn_ref, y_ref, st_ref, b_scr):
    L = CHUNK

    lb = lb_ref[...]
    f_pre = zh_ref[:, W_MIX:2 * W_MIX].astype(F32)
    lf = jnp.log(lb + (1.0 - lb) * jax.nn.sigmoid(f_pre))
    k_all = (1.0 - lb) * jax.nn.sigmoid(-f_pre)
    row = lax.broadcasted_iota(I32, (L, L), 0)
    col = lax.broadcasted_iota(I32, (L, L), 1)
    lower = (col <= row).astype(BF16)
    lf_hi, lf_lo = _split_hi_lo(lf)
    b_scr[...] = _dot(lower, lf_hi) + _dot(lower, lf_lo)

    for h in range(N_HEADS):
        hs = slice(h * HEAD_DIM, (h + 1) * HEAD_DIM)
        b = b_scr[:, hs]
        q = _silu(zh_ref[:, hs].astype(F32))
        k = k_all[:, hs]
        v = zh_ref[:, 2 * W_MIX + h * HEAD_DIM:2 * W_MIX + (h + 1) * HEAD_DIM]
        b_last = b_scr[L - 1:L, hs]

        o = _dot_nt((q * jnp.exp(b)).astype(BF16), st_ref[h].astype(BF16))

        a = jnp.zeros((L, L), F32)
        span = L // 2
        while span >= 16:
            mids = range(span, L, 2 * span)
            r = _rows_bcast(b_scr, [m for m in mids for _ in (0, 1)], span, hs)
            lower = (row & span) != 0
            e = jnp.exp(jnp.where(lower, b - r, r - b))
            qt = jnp.where(lower, q * e, 0.0).astype(BF16)
            kt = jnp.where(lower, 0.0, k * e).astype(BF16)
            p = _dot_nt(qt, kt)
            same = (row & -(2 * span)) == (col & -(2 * span))
            a = a + jnp.where(same, p, 0.0)
            span //= 2
        r = _rows_bcast(b_scr, range(0, L, 16), 16, hs)
        p = _dot_nt((q * jnp.exp(b - r)).astype(BF16), (k * jnp.exp(r - b)).astype(BF16))
        a = a + jnp.where(((row & -16) == (col & -16)) & (col <= row), p, 0.0)

        o = o + _dot(a.astype(BF16), v)

        ke = (k * jnp.exp(b_last - b)).astype(BF16)
        st_ref[h] = jnp.exp(b_last) * st_ref[h] + _dot_tn(v, ke)

        o = o * lax.rsqrt(jnp.mean(o * o, axis=-1, keepdims=True) + EPS) * gn_ref[...]
        o = o * _silu(zh_ref[:, 3 * W_MIX + h * HEAD_DIM:3 * W_MIX + (h + 1) * HEAD_DIM].astype(F32))
        y_ref[:, W_MIX + h * HEAD_DIM:W_MIX + (h + 1) * HEAD_DIM] = o.astype(BF16)


def _mixer_kernel(zm_ref, zh_ref, gate_ref, gt0_ref, gt1_ref, conv_ref, gm_ref, lb_ref, gh_ref, y_ref,
                  cbuf, c_ref, n_ref, m_ref, st_ref, b_scr):
    @pl.when(pl.program_id(1) == 0)
    def _():
        cbuf[:, 0:8, :] = jnp.zeros((MIX_BATCH, 8, 2 * W_MIX), F32)
        c_ref[...] = jnp.zeros_like(c_ref)
        n_ref[...] = jnp.zeros_like(n_ref)
        m_ref[...] = jnp.zeros_like(m_ref)
        st_ref[...] = jnp.zeros_like(st_ref)

    for bi, gt_ref in enumerate((gt0_ref, gt1_ref)):
        _mlstm_chunk(zm_ref.at[bi], gate_ref.at[bi], gt_ref, conv_ref, gm_ref, y_ref.at[bi],
                     cbuf.at[bi], c_ref.at[bi], n_ref.at[bi], m_ref.at[bi])
        _hgrn_chunk(zh_ref.at[bi], lb_ref, gh_ref, y_ref.at[bi], st_ref.at[bi], b_scr.at[bi])


def _mixer(zm, zh, gate, gate_t, conv_w, g_mlstm, lb, g_hgrn, batch, n_chunks):
    n = zm.shape[0]
    seq = n // batch
    assert MIX_BATCH == 2 and batch % MIX_BATCH == 0
    blk = lambda b, c: (b, c, 0)
    const = lambda b, c: (0, 0)
    return pl.pallas_call(
        _mixer_kernel,
        grid=(batch // MIX_BATCH, n_chunks),
        in_specs=[
            pl.BlockSpec((MIX_BATCH, CHUNK, 4 * W_MIX), blk),
            pl.BlockSpec((MIX_BATCH, CHUNK, 4 * W_MIX), blk),
            pl.BlockSpec((MIX_BATCH, CHUNK, LANES), blk),
            pl.BlockSpec((8, CHUNK), lambda b, c: (0, (MIX_BATCH * b) * n_chunks + c)),
            pl.BlockSpec((8, CHUNK), lambda b, c: (0, (MIX_BATCH * b + 1) * n_chunks + c)),
            pl.BlockSpec((CONV_WIDTH, 2 * W_MIX), const),
            pl.BlockSpec((1, W_MIX), const),
            pl.BlockSpec((1, W_MIX), const),
            pl.BlockSpec((1, HEAD_DIM), const),
        ],
        out_specs=pl.BlockSpec((MIX_BATCH, CHUNK, 2 * W_MIX), blk),
        out_shape=jax.ShapeDtypeStruct((batch, seq, 2 * W_MIX), BF16),
        scratch_shapes=[
            pltpu.VMEM((MIX_BATCH, CHUNK + 8, 2 * W_MIX), F32),
            pltpu.VMEM((MIX_BATCH, N_HEADS, HEAD_DIM, HEAD_DIM), F32),
            pltpu.VMEM((MIX_BATCH, 8, HEAD_DIM), F32),
            pltpu.VMEM((MIX_BATCH, 8, LANES), F32),
            pltpu.VMEM((MIX_BATCH, N_HEADS, HEAD_DIM, HEAD_DIM), F32),
            pltpu.VMEM((MIX_BATCH, CHUNK, W_MIX), F32),
        ],
        compiler_params=_params("parallel", "arbitrary"),
        name="mixer",
    )(zm.reshape(batch, seq, 4 * W_MIX), zh.reshape(batch, seq, 4 * W_MIX), gate.reshape(batch, seq, LANES),
      gate_t, gate_t, conv_w, g_mlstm, lb, g_hgrn).reshape(n, 2 * W_MIX)


def _pack_bf16_pair(lo, hi):
    lo_bits = pltpu.bitcast(lo.astype(BF16).astype(F32), U32)
    hi_bits = pltpu.bitcast(hi.astype(BF16).astype(F32), U32)
    return (hi_bits & jnp.uint32(0xFFFF0000)) | (lo_bits >> 16)


def _unpack_bf16_pair(w):
    lo = pltpu.bitcast(w << 16, F32).astype(BF16)
    hi = pltpu.bitcast(w & jnp.uint32(0xFFFF0000), F32).astype(BF16)
    return lo, hi


def _post_mix_kernel(x_ref, y_ref, wo_ref, g_ref, wrh_ref, wrl_ref, br_ref,
                     x1_ref, hn_ref, route_t_ref, comb_ref, count_ref, run_ref):
    tm = x_ref.shape[0]
    n_r = wrh_ref.shape[0]

    @pl.when(pl.program_id(0) == 0)
    def _():
        run_ref[...] = jnp.zeros_like(run_ref)

    x1 = x_ref[...] + _dot(y_ref[...], wo_ref[...])
    x1_ref[...] = x1
    hn = _rms(x1, g_ref[...])
    half = D_MODEL // 2
    packed = _pack_bf16_pair(hn[:, :half], hn[:, half:])
    for j in range(PACK_ROWS):
        hn_ref[pl.ds(j, tm, stride=PACK_ROWS), :] = packed[:, j * LANES:(j + 1) * LANES]

    hn_hi = hn.astype(BF16)
    hn_lo = (hn - hn_hi.astype(F32)).astype(BF16)
    logits = (_dot_nt(wrh_ref[...], hn_hi) + (_dot_nt(wrl_ref[...], hn_hi) + _dot_nt(wrh_ref[...], hn_lo))
              + br_ref[...])
    rix = lax.broadcasted_iota(I32, (n_r, tm), 0)
    neg = -jnp.inf
    g_l = jnp.where(rix < N_GROUPS, logits, neg)
    g_max = jnp.max(g_l, axis=0, keepdims=True)
    g_sel = jnp.min(jnp.where(g_l == g_max, rix, n_r), axis=0, keepdims=True)
    g_val = 1.0 / jnp.sum(jnp.exp(g_l - g_max), axis=0, keepdims=True)

    e_row = rix - N_GROUPS
    in_group = (e_row >= g_sel * EXPERTS_PER_GROUP) & (e_row < (g_sel + 1) * EXPERTS_PER_GROUP)
    e_l = jnp.where(in_group, logits, neg)
    v1 = jnp.max(e_l, axis=0, keepdims=True)
    i1 = jnp.min(jnp.where(e_l == v1, rix, n_r), axis=0, keepdims=True)
    e_l2 = jnp.where(rix == i1, neg, e_l)
    v2 = jnp.max(e_l2, axis=0, keepdims=True)
    i2 = jnp.min(jnp.where(e_l2 == v2, rix, n_r), axis=0, keepdims=True)
    t = jnp.exp(v2 - v1)
    c1 = g_val / (1.0 + t)
    c2 = g_val * t / (1.0 + t)

    hot1 = rix == i1
    hot2 = rix == i2
    hot = (hot1 | hot2).astype(F32)
    r_i = lax.broadcasted_iota(I32, (tm, tm), 0)
    c_i = lax.broadcasted_iota(I32, (tm, tm), 1)
    before = _dot(hot.astype(BF16), (r_i < c_i).astype(BF16)) + run_ref[:, 0:1]
    rank1 = jnp.sum(jnp.where(hot1, before, 0.0), axis=0, keepdims=True)
    rank2 = jnp.sum(jnp.where(hot2, before, 0.0), axis=0, keepdims=True)
    run_ref[...] = run_ref[...] + jnp.sum(hot, axis=1, keepdims=True)
    count_ref[...] = run_ref[...]

    r8 = lax.broadcasted_iota(I32, (8, tm), 0)
    out = jnp.where(r8 == 0, (i1 - N_GROUPS).astype(F32), 0.0)
    out = jnp.where(r8 == 1, (i2 - N_GROUPS).astype(F32), out)
    out = jnp.where(r8 == 2, rank1, out)
    out = jnp.where(r8 == 3, rank2, out)
    route_t_ref[...] = out

    r128 = lax.broadcasted_iota(I32, (LANES, tm), 0)
    slab = jnp.where(r128 == 0, c1, jnp.where(r128 == 1, c2, 0.0))
    for c in range(tm // LANES):
        comb_ref[c * LANES:(c + 1) * LANES, :] = slab[:, c * LANES:(c + 1) * LANES].T


def _post_mix(x2, y, w_out, g_ffn, wr_hi, wr_lo, b_r, tm):
    n = x2.shape[0]
    n_r = wr_hi.shape[0]
    const = lambda i: (0, 0)
    rows = lambda i: (i, 0)
    return pl.pallas_call(
        _post_mix_kernel,
        grid=(n // tm,),
        in_specs=[
            pl.BlockSpec((tm, D_MODEL), rows),
            pl.BlockSpec((tm, 2 * W_MIX), rows),
            pl.BlockSpec((2 * W_MIX, D_MODEL), const),
            pl.BlockSpec((1, D_MODEL), const),
            pl.BlockSpec((n_r, D_MODEL), const),
            pl.BlockSpec((n_r, D_MODEL), const),
            pl.BlockSpec((n_r, 1), const),
        ],
        out_specs=[
            pl.BlockSpec((tm, D_MODEL), rows),
            pl.BlockSpec((tm * PACK_ROWS, LANES), rows),
            pl.BlockSpec((8, tm), lambda i: (0, i)),
            pl.BlockSpec((tm, LANES), rows),
            pl.BlockSpec((n_r, LANES), const),
        ],
        out_shape=[
            jax.ShapeDtypeStruct((n, D_MODEL), F32),
            jax.ShapeDtypeStruct((n * PACK_ROWS, LANES), U32),
            jax.ShapeDtypeStruct((8, n), F32),
            jax.ShapeDtypeStruct((n, LANES), F32),
            jax.ShapeDtypeStruct((n_r, LANES), F32),
        ],
        scratch_shapes=[pltpu.VMEM((n_r, LANES), F32)],
        compiler_params=_params("arbitrary"),
        name="post_mix",
    )(x2, y, w_out, g_ffn, wr_hi, wr_lo, b_r)


def _dispatch_kernel(d0_ref, d1_ref, hn_ref, xb_init_ref, xb_ref, sem):
    del xb_init_ref
    ts = hn_ref.shape[0] // PACK_ROWS

    def start(t, carry):
        src = pl.multiple_of(t * PACK_ROWS, PACK_ROWS)
        for k, d_ref in enumerate((d0_ref, d1_ref)):
            dst = pl.multiple_of(d_ref[0, 0, t] * PACK_ROWS, PACK_ROWS)
            pltpu.make_async_copy(hn_ref.at[pl.ds(src, PACK_ROWS), :], xb_ref.at[pl.ds(dst, PACK_ROWS), :],
                                  sem.at[k]).start(priority=k)
        return carry

    lax.fori_loop(0, ts, start, 0, unroll=DMA_UNROLL)
    for k in range(2):
        pltpu.make_async_copy(hn_ref, xb_ref.at[pl.ds(0, ts * PACK_ROWS), :], sem.at[k]).wait()


def _dispatch(dest0, dest1, hn, n_rows, ts):
    n = hn.shape[0] // PACK_ROWS
    return pl.pallas_call(
        _dispatch_kernel,
        grid=(n // ts,),
        in_specs=[
            pl.BlockSpec((1, 1, ts), lambda i: (i, 0, 0), memory_space=pltpu.SMEM),
            pl.BlockSpec((1, 1, ts), lambda i: (i, 0, 0), memory_space=pltpu.SMEM),
            pl.BlockSpec((ts * PACK_ROWS, LANES), lambda i: (i, 0)),
            pl.BlockSpec(memory_space=pl.ANY),
        ],
        out_specs=pl.BlockSpec(memory_space=pl.ANY),
        out_shape=jax.ShapeDtypeStruct((n_rows * PACK_ROWS, LANES), U32),
        scratch_shapes=[pltpu.SemaphoreType.DMA((2,))],
        input_output_aliases={3: 0},
        compiler_params=_params("arbitrary"),
        name="dispatch",
    )(dest0, dest1, hn, jnp.zeros((n_rows * PACK_ROWS, LANES), U32))


def _experts_kernel(be_ref, bv_ref, nu_ref, xb_ref, wg_ref, wu_ref, wd_ref, yb_ref):
    valid = bv_ref[pl.program_id(0)]
    half = EXPERT_BLOCK // 2

    def mlp(rows):
        pairs = [_unpack_bf16_pair(xb_ref[pl.ds(j, rows, stride=PACK_ROWS), :]) for j in range(PACK_ROWS)]
        x = jnp.concatenate([lo for lo, _ in pairs] + [hi for _, hi in pairs], axis=1)
        a = (_silu(_dot(x, wg_ref[...])) * _dot(x, wu_ref[...])).astype(BF16)
        y = _dot(a, wd_ref[...])
        for j in range(OUT_ROWS):
            yb_ref[pl.ds(j, rows, stride=OUT_ROWS), :] = y[:, j * LANES:(j + 1) * LANES]

    @pl.when(valid > half)
    def _():
        mlp(EXPERT_BLOCK)

    @pl.when((valid > 0) & (valid <= half))
    def _():
        mlp(half)
        yb_ref[half * OUT_ROWS:, :] = jnp.zeros((half * OUT_ROWS, LANES), F32)

    @pl.when(valid == 0)
    def _():
        yb_ref[...] = jnp.zeros_like(yb_ref)


def _experts(block_e, block_valid, n_used, xb, w_gate, w_up, w_down):
    n_rows = xb.shape[0] // PACK_ROWS
    n_blocks = n_rows // EXPERT_BLOCK
    xrow = lambda i, be, bv, nu: (jnp.maximum(jnp.minimum(i, nu[0] - 1), 0), 0)
    wsel = lambda i, be, bv, nu: (be[i], 0, 0)
    grid_spec = pltpu.PrefetchScalarGridSpec(
        num_scalar_prefetch=3,
        grid=(n_blocks,),
        in_specs=[
            pl.BlockSpec((EXPERT_BLOCK * PACK_ROWS, LANES), xrow),
            pl.BlockSpec((None, D_MODEL, D_EXPERT), wsel),
            pl.BlockSpec((None, D_MODEL, D_EXPERT), wsel),
            pl.BlockSpec((None, D_EXPERT, D_MODEL), wsel),
        ],
        out_specs=pl.BlockSpec((EXPERT_BLOCK * OUT_ROWS, LANES), lambda i, be, bv, nu: (i, 0)),
    )
    return pl.pallas_call(
        _experts_kernel,
        grid_spec=grid_spec,
        out_shape=jax.ShapeDtypeStruct((n_rows * OUT_ROWS, LANES), F32),
        compiler_params=_params("arbitrary"),
        name="experts",
    )(block_e, block_valid, n_used, xb, w_gate, w_up, w_down)


def _combine_kernel(d0_ref, d1_ref, d0a_ref, d1a_ref, d0b_ref, d1b_ref, x1_ref, p_ref, comb_ref, gpl_ref, wplg_ref,
                    wplp_ref, gfin_ref, yb_ref, out_ref, gbuf, x3_ref, sem):
    tf = x1_ref.shape[0]
    i = pl.program_id(0)
    n_steps = pl.num_programs(0)
    slot = lax.rem(i, GATHER_BUFS)
    slot_a = lax.rem(i + 1, GATHER_BUFS)
    slot_b = lax.rem(i + 2, GATHER_BUFS)

    def start_gather(d_refs, s, t):
        row = pl.multiple_of(t * OUT_ROWS, OUT_ROWS)
        for k, d_ref in enumerate(d_refs):
            src = pl.multiple_of(d_ref[0, 0, t] * OUT_ROWS, OUT_ROWS)
            pltpu.make_async_copy(yb_ref.at[pl.ds(src, OUT_ROWS), :], gbuf.at[s, k, pl.ds(row, OUT_ROWS), :],
                                  sem.at[s, k]).start(priority=k)

    def wait_gathers(s):
        for k in range(2):
            pltpu.make_async_copy(yb_ref.at[pl.ds(0, tf * OUT_ROWS), :], gbuf.at[s, k], sem.at[s, k]).wait()

    @pl.when(i == 0)
    def _():
        lax.fori_loop(0, tf, lambda t, c: (start_gather((d0_ref, d1_ref), 0, t), c)[1], 0, unroll=DMA_UNROLL)
        lax.fori_loop(0, tf, lambda t, c: (start_gather((d0a_ref, d1a_ref), 1, t), c)[1], 0, unroll=DMA_UNROLL)

    wait_gathers(slot)

    comb = comb_ref[...]
    g1, g2 = (jnp.concatenate([gbuf[slot, k, pl.ds(j, tf, stride=OUT_ROWS), :] for j in range(OUT_ROWS)], axis=1)
              for k in range(2))
    y = comb[:, 0:1] * g1 + comb[:, 1:2] * g2
    x2 = x1_ref[...] + y
    hb = _rms(x2, gpl_ref[...]).astype(BF16)
    pb = p_ref[...].astype(BF16)
    n_col = D_MODEL // COMBINE_COLS
    for c in range(n_col):
        for t in range(c * tf // n_col, (c + 1) * tf // n_col):
            start_gather((d0b_ref, d1b_ref), slot_b, t)
        cs = slice(c * COMBINE_COLS, (c + 1) * COMBINE_COLS)
        gate = jax.nn.sigmoid(_dot(hb, wplg_ref[:, cs]))
        x3_ref[:, cs] = x2[:, cs] + gate * _dot(pb, wplp_ref[:, cs])
    x3 = x3_ref[...]
    out_ref[...] = _rms(x3, gfin_ref[...])

    @pl.when(i == n_steps - 1)
    def _():
        wait_gathers(slot_a)
        wait_gathers(slot_b)


def _combine(dest0, dest1, x1, p2, comb, g_pl, w_plg, w_plp, g_final, yb, tf):
    n = x1.shape[0]
    n_steps = n // tf
    assert n_steps >= GATHER_BUFS - 1
    const = lambda i: (0, 0)
    rows = lambda i: (i, 0)
    ahead = lambda a: (lambda i: (jnp.minimum(i + a, n_steps - 1), 0, 0))
    dest_specs = [pl.BlockSpec((1, 1, tf), ahead(a), memory_space=pltpu.SMEM) for a in range(GATHER_BUFS)
                  for _ in range(2)]
    return pl.pallas_call(
        _combine_kernel,
        grid=(n_steps,),
        in_specs=dest_specs + [
            pl.BlockSpec((tf, D_MODEL), rows),
            pl.BlockSpec((tf, PLE_DIM), rows),
            pl.BlockSpec((tf, LANES), rows),
            pl.BlockSpec((1, D_MODEL), const),
            pl.BlockSpec((D_MODEL, D_MODEL), const),
            pl.BlockSpec((PLE_DIM, D_MODEL), const),
            pl.BlockSpec((1, D_MODEL), const),
            pl.BlockSpec(memory_space=pl.ANY),
        ],
        out_specs=pl.BlockSpec((tf, D_MODEL), rows),
        out_shape=jax.ShapeDtypeStruct((n, D_MODEL), F32),
        scratch_shapes=[pltpu.VMEM((GATHER_BUFS, 2, tf * OUT_ROWS, LANES), F32), pltpu.VMEM((tf, D_MODEL), F32),
                        pltpu.SemaphoreType.DMA((GATHER_BUFS, 2))],
        compiler_params=_params("arbitrary"),
        name="combine",
    )(dest0, dest1, dest0, dest1, dest0, dest1, x1, p2, comb, g_pl, w_plg, w_plp, g_final, yb)


def _layer(x2, p2, batch, seq, g_mix, w_in, b_mgate, conv_qk, g_mlstm, lb, g_hgrn, w_out, g_ffn,
           w_rg, b_rg, w_re, b_re, w_e_gate, w_e_up, w_e_down, g_pl, w_pl_gate, w_pl_proj, g_out):
    n = x2.shape[0]
    n_chunks = seq // CHUNK
    m_cols = 4 * W_MIX
    n_gate = 2 * N_HEADS

    w_in_b = w_in.astype(BF16)
    w_m = w_in_b[:, :m_cols]
    w_gcols = w_in_b[:, m_cols:m_cols + n_gate]
    w_h = w_in_b[:, m_cols + n_gate:]
    w_g = jnp.pad(w_gcols, ((0, 0), (0, LANES - n_gate)))
    b_g = jnp.pad(b_mgate.astype(F32), (0, LANES - n_gate))[None, :]
    zm, zh, gate, gate_t = _in_proj(x2, g_mix[None, :], w_m, w_h, w_g, w_gcols.T, b_g,
                                    b_mgate.astype(F32)[:, None], tm=min(512, n))

    y = _mixer(zm, zh, gate, gate_t, conv_qk, g_mlstm[None, :], lb[None, :], g_hgrn[None, :], batch, n_chunks)

    n_logit = N_GROUPS + N_EXPERTS
    w_rt = jnp.pad(jnp.concatenate([w_rg, w_re], axis=1).T, ((0, ROUTER_ROWS - n_logit), (0, 0)))
    wr_hi = w_rt.astype(BF16)
    wr_lo = (w_rt - wr_hi.astype(F32)).astype(BF16)
    b_r = jnp.pad(jnp.concatenate([b_rg, b_re]), (0, ROUTER_ROWS - n_logit))[:, None]
    x1, hn, route_t, comb, counts = _post_mix(x2, y, w_out.astype(BF16), g_ffn[None, :], wr_hi, wr_lo, b_r,
                                              tm=min(512, n))

    counts = counts[N_GROUPS:n_logit, 0].astype(I32)
    padded = (counts + EXPERT_BLOCK - 1) // EXPERT_BLOCK * EXPERT_BLOCK
    pend = jnp.cumsum(padded)
    pstart = pend - padded
    n_blocks = (2 * n) // EXPERT_BLOCK + N_EXPERTS
    n_rows = n_blocks * EXPERT_BLOCK
    n_used = (pend[-1] // EXPERT_BLOCK).astype(I32)
    block_ids = jnp.arange(n_blocks, dtype=I32)
    block_row = jnp.minimum(block_ids, n_used - 1) * EXPERT_BLOCK
    block_e = jnp.sum((pend[None, :] <= block_row[:, None]).astype(I32), axis=1)
    token_end = jnp.sum(jnp.where(block_e[:, None] == jnp.arange(N_EXPERTS, dtype=I32), pstart + counts, 0), axis=1)
    block_valid = jnp.where(block_ids < n_used, jnp.clip(token_end - block_row, 0, EXPERT_BLOCK), 0).astype(I32)
    expert_id = route_t[0:2].astype(I32)
    hot = expert_id[:, :, None] == jnp.arange(N_EXPERTS, dtype=I32)
    dest = jnp.sum(jnp.where(hot, pstart, 0), axis=-1) + route_t[2:4].astype(I32)

    ts = min(512, n)
    xb = _dispatch(dest[0].reshape(n // ts, 1, ts), dest[1].reshape(n // ts, 1, ts), hn, n_rows, ts)
    yb = _experts(block_e, block_valid, n_used[None], xb, w_e_gate.astype(BF16), w_e_up.astype(BF16), w_e_down.astype(BF16))
    tf = min(256, n)
    return _combine(dest[0].reshape(n // tf, 1, tf), dest[1].reshape(n // tf, 1, tf), x1, p2, comb, g_pl[None, :],
                    w_pl_gate.astype(BF16), w_pl_proj.astype(BF16), g_out[None, :], yb, tf)


def kernel(x, p, g_mix, w_in, b_mgate, conv_qk, g_mlstm, hg_lb, g_hgrn, w_out, g_ffn, w_rg, b_rg, w_re, b_re,
           w_e_gate, w_e_up, w_e_down, g_pl, w_pl_gate, w_pl_proj, g_final):
    batch, seq, d = x.shape
    depth = p.shape[0]
    assert depth == 1, "the fused final norm assumes a single layer"
    lower_bounds = jnp.cumsum(jax.nn.softmax(hg_lb.astype(F32), axis=0), axis=0)
    i = 0
    out = _layer(x.reshape(batch * seq, d), p[i].reshape(batch * seq, PLE_DIM), batch, seq,
                 g_mix[i], w_in[i], b_mgate[i], conv_qk[i], g_mlstm[i], lower_bounds[i], g_hgrn[i], w_out[i],
                 g_ffn[i], w_rg[i], b_rg[i], w_re[i], b_re[i], w_e_gate[i], w_e_up[i], w_e_down[i],
                 g_pl[i], w_pl_gate[i], w_pl_proj[i], g_final)
    return out.reshape(batch, seq, d)
```

```python
import functools

import jax
import jax.numpy as jnp
from jax import lax
from jax.experimental import pallas as pl
from jax.experimental.pallas import tpu as pltpu

F32 = jnp.float32
BF16 = jnp.bfloat16
I32 = jnp.int32
U32 = jnp.uint32
EPS = 1e-6

LANES = 128
D_MODEL = 1024
W_MIX = 512
N_HEADS = 4
HEAD_DIM = 128
N_GROUPS = 4
EXPERTS_PER_GROUP = 8
N_EXPERTS = N_GROUPS * EXPERTS_PER_GROUP
D_EXPERT = 512
PLE_DIM = 256
CONV_WIDTH = 4
CHUNK = 128
EXPERT_BLOCK = 512
PACK_ROWS = D_MODEL // 2 // LANES
OUT_ROWS = D_MODEL // LANES
ROUTER_ROWS = 48
DMA_UNROLL = 4
COMBINE_COLS = 256
MIX_BATCH = 2
IN_PROJ_COLS = 256
GATHER_BUFS = 3
VMEM_LIMIT = 56 * 1024 * 1024


def _dot(a, b):
    return jnp.dot(a, b, preferred_element_type=F32)


def _dot_nt(a, b):
    return lax.dot_general(a, b, (((1,), (1,)), ((), ())), preferred_element_type=F32)


def _dot_tn(a, b):
    return lax.dot_general(a, b, (((0,), (0,)), ((), ())), preferred_element_type=F32)


def _rms(u, g):
    return u * lax.rsqrt(jnp.mean(u * u, axis=-1, keepdims=True) + EPS) * g


def _silu(u):
    return u * jax.nn.sigmoid(u)


def _log_sigmoid(u):
    return jnp.minimum(u, 0.0) - jnp.log1p(jnp.exp(-jnp.abs(u)))


def _split_hi_lo(u):
    hi = u.astype(BF16)
    return hi, (u - hi.astype(F32)).astype(BF16)


def _params(*sem):
    return pltpu.CompilerParams(dimension_semantics=sem, vmem_limit_bytes=VMEM_LIMIT)


def _in_proj_kernel(tiles_per_seq, x_ref, g_ref, wm_ref, wh_ref, wg_ref, wgt_ref, bg_ref, bgt_ref, conv_ref, lb_ref,
                    zm_ref, zh_ref, gate_ref, gate_t_ref, cbuf):
    tm = x_ref.shape[0]
    W = W_MIX
    CB = IN_PROJ_COLS

    @pl.when(pl.program_id(0) % tiles_per_seq == 0)
    def _():
        cbuf[0:8, :] = jnp.zeros((8, 2 * W), F32)

    h = _rms(x_ref[...], g_ref[...]).astype(BF16)

    def conv_finish(col0, scale):
        def finish(z):
            cs = slice(col0, col0 + CB)
            cbuf[8:8 + tm, cs] = z
            acc = z * conv_ref[CONV_WIDTH - 1:CONV_WIDTH, cs]
            for j in range(CONV_WIDTH - 1):
                acc = acc + cbuf[5 + j:5 + j + tm, cs] * conv_ref[j:j + 1, cs]
            cbuf[0:8, cs] = cbuf[tm:tm + 8, cs]
            zm_ref[:, cs] = (_silu(acc) * scale).astype(BF16)
        return finish

    def store(ref, col0, fn):
        def finish(z):
            ref[:, col0:col0 + CB] = fn(z).astype(BF16)
        return finish

    def forget_finish(col0):
        def finish(z):
            lb = lb_ref[:, col0:col0 + CB]
            zh_ref[:, W + col0:W + col0 + CB] = ((1.0 - lb) * jax.nn.sigmoid(-z)).astype(BF16)
            lf_hi, lf_lo = _split_hi_lo(jnp.log(lb + (1.0 - lb) * jax.nn.sigmoid(z)))
            zh_ref[:, 4 * W + col0:4 * W + col0 + CB] = lf_hi
            zh_ref[:, 5 * W + col0:5 * W + col0 + CB] = lf_lo
        return finish

    ident = lambda z: z
    plan = []
    for half in range(W // CB):
        c0 = half * CB
        plan += [
            (wm_ref, c0, conv_finish(c0, 1.0)),
            (wh_ref, 2 * W + c0, store(zh_ref, 2 * W + c0, ident)),
            (wm_ref, W + c0, conv_finish(W + c0, HEAD_DIM ** -0.5)),
            (wm_ref, 2 * W + c0, store(zm_ref, 2 * W + c0, ident)),
            (wh_ref, W + c0, forget_finish(c0)),
            (wm_ref, 3 * W + c0, store(zm_ref, 3 * W + c0, jax.nn.sigmoid)),
            (wh_ref, c0, store(zh_ref, c0, _silu)),
            (wh_ref, 3 * W + c0, store(zh_ref, 3 * W + c0, _silu)),
        ]

    project = lambda i: _dot(h, plan[i][0][:, plan[i][1]:plan[i][1] + CB])
    z_next = project(0)
    for i in range(len(plan)):
        z = z_next
        if i + 1 < len(plan):
            z_next = project(i + 1)
        plan[i][2](z)

    gate_ref[...] = _dot(h, wg_ref[...]) + bg_ref[...]
    gate_t_ref[...] = _dot_nt(wgt_ref[...], h) + bgt_ref[...]


def _in_proj(x2, g_mix, w_m, w_h, w_g, w_gt, b_g, b_gt, conv_w, lb, seq, tm):
    n = x2.shape[0]
    assert seq % tm == 0
    const = lambda i: (0, 0)
    rows = lambda i: (i, 0)
    return pl.pallas_call(
        functools.partial(_in_proj_kernel, seq // tm),
        grid=(n // tm,),
        in_specs=[
            pl.BlockSpec((tm, D_MODEL), rows),
            pl.BlockSpec((1, D_MODEL), const),
            pl.BlockSpec((D_MODEL, 4 * W_MIX), const),
            pl.BlockSpec((D_MODEL, 4 * W_MIX), const),
            pl.BlockSpec((D_MODEL, LANES), const),
            pl.BlockSpec((8, D_MODEL), const),
            pl.BlockSpec((1, LANES), const),
            pl.BlockSpec((8, 1), const),
            pl.BlockSpec((CONV_WIDTH, 2 * W_MIX), const),
            pl.BlockSpec((1, W_MIX), const),
        ],
        out_specs=[
            pl.BlockSpec((tm, 4 * W_MIX), rows),
            pl.BlockSpec((tm, 6 * W_MIX), rows),
            pl.BlockSpec((tm, LANES), rows),
            pl.BlockSpec((8, tm), lambda i: (0, i)),
        ],
        out_shape=[
            jax.ShapeDtypeStruct((n, 4 * W_MIX), BF16),
            jax.ShapeDtypeStruct((n, 6 * W_MIX), BF16),
            jax.ShapeDtypeStruct((n, LANES), F32),
            jax.ShapeDtypeStruct((8, n), F32),
        ],
        scratch_shapes=[pltpu.VMEM((tm + 8, 2 * W_MIX), F32)],
        compiler_params=_params("arbitrary"),
        name="in_proj",
    )(x2, g_mix, w_m, w_h, w_g, w_gt, b_g, b_gt, conv_w, lb)


def _rows_bcast(ref, rows, span, hs):
    return jnp.concatenate([jnp.broadcast_to(ref[r:r + 1, hs], (span, HEAD_DIM)) for r in rows], axis=0)


def _head_cols(base, h):
    return slice(base * W_MIX + h * HEAD_DIM, base * W_MIX + (h + 1) * HEAD_DIM)


def _mixer_kernel(zm_ref, zh_ref, gate_ref, gt0_ref, gt1_ref, gm_ref, gh_ref, y_ref,
                  c_ref, n_ref, m_ref, st_ref, b_scr):
    L = CHUNK

    @pl.when(pl.program_id(1) == 0)
    def _():
        c_ref[...] = jnp.zeros_like(c_ref)
        n_ref[...] = jnp.zeros_like(n_ref)
        m_ref[...] = jnp.zeros_like(m_ref)
        st_ref[...] = jnp.zeros_like(st_ref)

    units = [(bi, h) for bi in range(MIX_BATCH) for h in range(N_HEADS)]
    gt_refs = (gt0_ref, gt1_ref)
    row = lax.broadcasted_iota(I32, (L, L), 0)
    col = lax.broadcasted_iota(I32, (L, L), 1)
    causal = col <= row
    lower, upper = causal.astype(BF16), (row <= col).astype(BF16)

    gate, gate_t, bcum_c, bcum_r = [], [], [], []
    for bi in range(MIX_BATCH):
        g = gate_ref[bi]
        g_t = gt_refs[bi][...]
        lf_c, lf_r = _split_hi_lo(_log_sigmoid(g)), _split_hi_lo(_log_sigmoid(g_t))
        gate.append(g)
        gate_t.append(g_t)
        bcum_c.append(_dot(lower, lf_c[0]) + _dot(lower, lf_c[1]))
        bcum_r.append(_dot(lf_r[0], upper) + _dot(lf_r[1], upper))
        b_scr[bi] = (_dot(lower, zh_ref[bi, :, 4 * W_MIX:5 * W_MIX])
                     + _dot(lower, zh_ref[bi, :, 5 * W_MIX:6 * W_MIX]))

    qk, qc = {}, {}
    for u in units:
        bi, h = u
        qb = zm_ref[bi, :, _head_cols(0, h)]
        qk[u] = _dot_nt(qb, zm_ref[bi, :, _head_cols(1, h)])
        qc[u] = _dot(qb, c_ref[bi, h].astype(BF16))

    oi, blocks = {}, {}
    for u in units:
        bi, h = u
        hs = _head_cols(0, h)
        b = b_scr[bi, :, hs]
        q = zh_ref[bi, :, hs].astype(F32)
        k = zh_ref[bi, :, _head_cols(1, h)].astype(F32)
        oi[u] = _dot_nt((q * jnp.exp(b)).astype(BF16), st_ref[bi, h].astype(BF16))
        parts = []
        span = L // 2
        while span >= 16:
            mids = range(span, L, 2 * span)
            r = _rows_bcast(b_scr.at[bi], [m for m in mids for _ in (0, 1)], span, hs)
            low = (row & span) != 0
            z = (jnp.where(low, q, k) * jnp.exp(jnp.where(low, b - r, r - b))).astype(BF16)
            keep = ((row & -(2 * span)) == (col & -(2 * span))) & low & ((col & span) == 0)
            parts.append((keep, _dot_nt(z, z)))
            span //= 2
        r = _rows_bcast(b_scr.at[bi], range(0, L, 16), 16, hs)
        p = _dot_nt((q * jnp.exp(b - r)).astype(BF16), (k * jnp.exp(r - b)).astype(BF16))
        parts.append((((row & -16) == (col & -16)) & causal, p))
        blocks[u] = parts

    sv, den, w_inter, m_t_all = {}, {}, {}, {}
    for u in units:
        bi, h = u
        bc = bcum_c[bi][:, N_HEADS + h:N_HEADS + h + 1]
        br = bcum_r[bi][N_HEADS + h:N_HEADS + h + 1, :]
        ir = gate_t[bi][h:h + 1, :]
        m_prev = m_ref[bi, h:h + 1, 0:1]
        log_d = jnp.where(causal, bc - br + ir, -jnp.inf)
        inter = bc + m_prev
        m_t = jnp.maximum(inter, jnp.max(log_d, axis=-1, keepdims=True))
        s = qk[u] * jnp.exp(log_d - m_t)
        w_inter[u] = jnp.exp(inter - m_t)
        m_t_all[u] = m_t
        q = zm_ref[bi, :, _head_cols(0, h)].astype(F32)
        den[u] = (jnp.sum(s, axis=-1, keepdims=True)
                  + w_inter[u] * jnp.sum(q * n_ref[bi, h:h + 1, :], axis=-1, keepdims=True))
        sv[u] = _dot(s.astype(BF16), zm_ref[bi, :, _head_cols(2, h)])

    av, sk = {}, {}
    for u in units:
        bi, h = u
        hs = _head_cols(0, h)
        a = None
        for keep, p in blocks[u]:
            term = jnp.where(keep, p, 0.0)
            a = term if a is None else a + term
        v = zh_ref[bi, :, _head_cols(2, h)]
        av[u] = _dot(a.astype(BF16), v)
        k = zh_ref[bi, :, _head_cols(1, h)].astype(F32)
        ke = (k * jnp.exp(b_scr[bi, L - 1:L, hs] - b_scr[bi, :, hs])).astype(BF16)
        sk[u] = _dot_tn(v, ke)

    for u in units:
        bi, h = u
        bc = bcum_c[bi][:, N_HEADS + h:N_HEADS + h + 1]
        ic = gate[bi][:, h:h + 1]
        m_prev = m_ref[bi, h:h + 1, 0:1]
        b_last = bc[L - 1:L, :]
        w_log = b_last - bc + ic
        m_new = jnp.maximum(b_last + m_prev, jnp.max(w_log, axis=0, keepdims=True))
        k = zm_ref[bi, :, _head_cols(1, h)].astype(F32)
        kw = k * jnp.exp(w_log - m_new)
        decay = jnp.exp(b_last + m_prev - m_new)
        c_ref[bi, h] = decay * c_ref[bi, h] + _dot_tn(kw.astype(BF16), zm_ref[bi, :, _head_cols(2, h)])
        n_ref[bi, h:h + 1, :] = decay * n_ref[bi, h:h + 1, :] + jnp.sum(kw, axis=0, keepdims=True)
        m_ref[bi, h:h + 1, :] = jnp.broadcast_to(m_new, (1, LANES))

    for u in units:
        bi, h = u
        hs = _head_cols(0, h)
        num = sv[u] + w_inter[u] * qc[u]
        hh = num * (1.0 / jnp.maximum(jnp.abs(den[u]), jnp.exp(-m_t_all[u])))
        hh = hh * zm_ref[bi, :, _head_cols(3, h)].astype(F32)
        hh = hh * lax.rsqrt(jnp.mean(hh * hh, axis=-1, keepdims=True) + EPS) * gm_ref[:, hs]
        y_ref[bi, :, hs] = hh.astype(BF16)

        st_ref[bi, h] = jnp.exp(b_scr[bi, L - 1:L, hs]) * st_ref[bi, h] + sk[u]
        o = oi[u] + av[u]
        o = o * lax.rsqrt(jnp.mean(o * o, axis=-1, keepdims=True) + EPS) * gh_ref[...]
        o = o * zh_ref[bi, :, _head_cols(3, h)].astype(F32)
        y_ref[bi, :, _head_cols(1, h)] = o.astype(BF16)


def _mixer(zm, zh, gate, gate_t, g_mlstm, g_hgrn, batch, n_chunks):
    n = zm.shape[0]
    seq = n // batch
    assert MIX_BATCH == 2 and batch % MIX_BATCH == 0
    blk = lambda b, c: (b, c, 0)
    const = lambda b, c: (0, 0)
    return pl.pallas_call(
        _mixer_kernel,
        grid=(batch // MIX_BATCH, n_chunks),
        in_specs=[
            pl.BlockSpec((MIX_BATCH, CHUNK, 4 * W_MIX), blk),
            pl.BlockSpec((MIX_BATCH, CHUNK, 6 * W_MIX), blk),
            pl.BlockSpec((MIX_BATCH, CHUNK, LANES), blk),
            pl.BlockSpec((8, CHUNK), lambda b, c: (0, (MIX_BATCH * b) * n_chunks + c)),
            pl.BlockSpec((8, CHUNK), lambda b, c: (0, (MIX_BATCH * b + 1) * n_chunks + c)),
            pl.BlockSpec((1, W_MIX), const),
            pl.BlockSpec((1, HEAD_DIM), const),
        ],
        out_specs=pl.BlockSpec((MIX_BATCH, CHUNK, 2 * W_MIX), blk),
        out_shape=jax.ShapeDtypeStruct((batch, seq, 2 * W_MIX), BF16),
        scratch_shapes=[
            pltpu.VMEM((MIX_BATCH, N_HEADS, HEAD_DIM, HEAD_DIM), F32),
            pltpu.VMEM((MIX_BATCH, 8, HEAD_DIM), F32),
            pltpu.VMEM((MIX_BATCH, 8, LANES), F32),
            pltpu.VMEM((MIX_BATCH, N_HEADS, HEAD_DIM, HEAD_DIM), F32),
            pltpu.VMEM((MIX_BATCH, CHUNK, W_MIX), F32),
        ],
        compiler_params=_params("parallel", "arbitrary"),
        name="mixer",
    )(zm.reshape(batch, seq, 4 * W_MIX), zh.reshape(batch, seq, 6 * W_MIX), gate.reshape(batch, seq, LANES),
      gate_t, gate_t, g_mlstm, g_hgrn).reshape(n, 2 * W_MIX)


def _pack_bf16_pair(lo, hi):
    lo_bits = pltpu.bitcast(lo.astype(BF16).astype(F32), U32)
    hi_bits = pltpu.bitcast(hi.astype(BF16).astype(F32), U32)
    return (hi_bits & jnp.uint32(0xFFFF0000)) | (lo_bits >> 16)


def _unpack_bf16_pair(w):
    lo = pltpu.bitcast(w << 16, F32).astype(BF16)
    hi = pltpu.bitcast(w & jnp.uint32(0xFFFF0000), F32).astype(BF16)
    return lo, hi


def _post_mix_kernel(x_ref, y_ref, wo_ref, g_ref, wrh_ref, wrl_ref, br_ref,
                     x1_ref, hn_ref, route_t_ref, comb_ref, count_ref, run_ref):
    tm = x_ref.shape[0]
    n_r = wrh_ref.shape[0]

    @pl.when(pl.program_id(0) == 0)
    def _():
        run_ref[...] = jnp.zeros_like(run_ref)

    x1 = x_ref[...] + _dot(y_ref[...], wo_ref[...])
    x1_ref[...] = x1
    hn = _rms(x1, g_ref[...])
    half = D_MODEL // 2
    packed = _pack_bf16_pair(hn[:, :half], hn[:, half:])
    for j in range(PACK_ROWS):
        hn_ref[pl.ds(j, tm, stride=PACK_ROWS), :] = packed[:, j * LANES:(j + 1) * LANES]

    hn_hi = hn.astype(BF16)
    hn_lo = (hn - hn_hi.astype(F32)).astype(BF16)
    logits = (_dot_nt(wrh_ref[...], hn_hi) + (_dot_nt(wrl_ref[...], hn_hi) + _dot_nt(wrh_ref[...], hn_lo))
              + br_ref[...])
    rix = lax.broadcasted_iota(I32, (n_r, tm), 0)
    neg = -jnp.inf
    g_l = jnp.where(rix < N_GROUPS, logits, neg)
    g_max = jnp.max(g_l, axis=0, keepdims=True)
    g_sel = jnp.min(jnp.where(g_l == g_max, rix, n_r), axis=0, keepdims=True)
    g_val = 1.0 / jnp.sum(jnp.exp(g_l - g_max), axis=0, keepdims=True)

    e_row = rix - N_GROUPS
    in_group = (e_row >= g_sel * EXPERTS_PER_GROUP) & (e_row < (g_sel + 1) * EXPERTS_PER_GROUP)
    e_l = jnp.where(in_group, logits, neg)
    v1 = jnp.max(e_l, axis=0, keepdims=True)
    i1 = jnp.min(jnp.where(e_l == v1, rix, n_r), axis=0, keepdims=True)
    e_l2 = jnp.where(rix == i1, neg, e_l)
    v2 = jnp.max(e_l2, axis=0, keepdims=True)
    i2 = jnp.min(jnp.where(e_l2 == v2, rix, n_r), axis=0, keepdims=True)
    t = jnp.exp(v2 - v1)
    c1 = g_val / (1.0 + t)
    c2 = g_val * t / (1.0 + t)

    hot1 = rix == i1
    hot2 = rix == i2
    hot = (hot1 | hot2).astype(F32)
    r_i = lax.broadcasted_iota(I32, (tm, tm), 0)
    c_i = lax.broadcasted_iota(I32, (tm, tm), 1)
    before = _dot(hot.astype(BF16), (r_i < c_i).astype(BF16)) + run_ref[:, 0:1]
    rank1 = jnp.sum(jnp.where(hot1, before, 0.0), axis=0, keepdims=True)
    rank2 = jnp.sum(jnp.where(hot2, before, 0.0), axis=0, keepdims=True)
    run_ref[...] = run_ref[...] + jnp.sum(hot, axis=1, keepdims=True)
    count_ref[...] = run_ref[...]

    r8 = lax.broadcasted_iota(I32, (8, tm), 0)
    out = jnp.where(r8 == 0, (i1 - N_GROUPS).astype(F32), 0.0)
    out = jnp.where(r8 == 1, (i2 - N_GROUPS).astype(F32), out)
    out = jnp.where(r8 == 2, rank1, out)
    out = jnp.where(r8 == 3, rank2, out)
    route_t_ref[...] = out

    r128 = lax.broadcasted_iota(I32, (LANES, tm), 0)
    slab = jnp.where(r128 == 0, c1, jnp.where(r128 == 1, c2, 0.0))
    for c in range(tm // LANES):
        comb_ref[c * LANES:(c + 1) * LANES, :] = slab[:, c * LANES:(c + 1) * LANES].T


def _post_mix(x2, y, w_out, g_ffn, wr_hi, wr_lo, b_r, tm):
    n = x2.shape[0]
    n_r = wr_hi.shape[0]
    const = lambda i: (0, 0)
    rows = lambda i: (i, 0)
    return pl.pallas_call(
        _post_mix_kernel,
        grid=(n // tm,),
        in_specs=[
            pl.BlockSpec((tm, D_MODEL), rows),
            pl.BlockSpec((tm, 2 * W_MIX), rows),
            pl.BlockSpec((2 * W_MIX, D_MODEL), const),
            pl.BlockSpec((1, D_MODEL), const),
            pl.BlockSpec((n_r, D_MODEL), const),
            pl.BlockSpec((n_r, D_MODEL), const),
            pl.BlockSpec((n_r, 1), const),
        ],
        out_specs=[
            pl.BlockSpec((tm, D_MODEL), rows),
            pl.BlockSpec((tm * PACK_ROWS, LANES), rows),
            pl.BlockSpec((8, tm), lambda i: (0, i)),
            pl.BlockSpec((tm, LANES), rows),
            pl.BlockSpec((n_r, LANES), const),
        ],
        out_shape=[
            jax.ShapeDtypeStruct((n, D_MODEL), F32),
            jax.ShapeDtypeStruct((n * PACK_ROWS, LANES), U32),
            jax.ShapeDtypeStruct((8, n), F32),
            jax.ShapeDtypeStruct((n, LANES), F32),
            jax.ShapeDtypeStruct((n_r, LANES), F32),
        ],
        scratch_shapes=[pltpu.VMEM((n_r, LANES), F32)],
        compiler_params=_params("arbitrary"),
        name="post_mix",
    )(x2, y, w_out, g_ffn, wr_hi, wr_lo, b_r)


def _dispatch_kernel(d0_ref, d1_ref, hn_ref, xb_init_ref, xb_ref, sem):
    del xb_init_ref
    ts = hn_ref.shape[0] // PACK_ROWS

    def start(t, carry):
        src = pl.multiple_of(t * PACK_ROWS, PACK_ROWS)
        for k, d_ref in enumerate((d0_ref, d1_ref)):
            dst = pl.multiple_of(d_ref[0, 0, t] * PACK_ROWS, PACK_ROWS)
            pltpu.make_async_copy(hn_ref.at[pl.ds(src, PACK_ROWS), :], xb_ref.at[pl.ds(dst, PACK_ROWS), :],
                                  sem.at[k]).start(priority=k)
        return carry

    lax.fori_loop(0, ts, start, 0, unroll=DMA_UNROLL)
    for k in range(2):
        pltpu.make_async_copy(hn_ref, xb_ref.at[pl.ds(0, ts * PACK_ROWS), :], sem.at[k]).wait()


def _dispatch(dest0, dest1, hn, n_rows, ts):
    n = hn.shape[0] // PACK_ROWS
    return pl.pallas_call(
        _dispatch_kernel,
        grid=(n // ts,),
        in_specs=[
            pl.BlockSpec((1, 1, ts), lambda i: (i, 0, 0), memory_space=pltpu.SMEM),
            pl.BlockSpec((1, 1, ts), lambda i: (i, 0, 0), memory_space=pltpu.SMEM),
            pl.BlockSpec((ts * PACK_ROWS, LANES), lambda i: (i, 0)),
            pl.BlockSpec(memory_space=pl.ANY),
        ],
        out_specs=pl.BlockSpec(memory_space=pl.ANY),
        out_shape=jax.ShapeDtypeStruct((n_rows * PACK_ROWS, LANES), U32),
        scratch_shapes=[pltpu.SemaphoreType.DMA((2,))],
        input_output_aliases={3: 0},
        compiler_params=_params("arbitrary"),
        name="dispatch",
    )(dest0, dest1, hn, jnp.zeros((n_rows * PACK_ROWS, LANES), U32))


def _experts_kernel(be_ref, bv_ref, nu_ref, xb_ref, wg_ref, wu_ref, wd_ref, yb_ref):
    valid = bv_ref[pl.program_id(0)]
    half = EXPERT_BLOCK // 2

    def mlp(rows):
        pairs = [_unpack_bf16_pair(xb_ref[pl.ds(j, rows, stride=PACK_ROWS), :]) for j in range(PACK_ROWS)]
        x = jnp.concatenate([lo for lo, _ in pairs] + [hi for _, hi in pairs], axis=1)
        a = (_silu(_dot(x, wg_ref[...])) * _dot(x, wu_ref[...])).astype(BF16)
        y = _dot(a, wd_ref[...])
        for j in range(OUT_ROWS):
            yb_ref[pl.ds(j, rows, stride=OUT_ROWS), :] = y[:, j * LANES:(j + 1) * LANES]

    @pl.when(valid > half)
    def _():
        mlp(EXPERT_BLOCK)

    @pl.when((valid > 0) & (valid <= half))
    def _():
        mlp(half)
        yb_ref[half * OUT_ROWS:, :] = jnp.zeros((half * OUT_ROWS, LANES), F32)

    @pl.when(valid == 0)
    def _():
        yb_ref[...] = jnp.zeros_like(yb_ref)


def _experts(block_e, block_valid, n_used, xb, w_gate, w_up, w_down):
    n_rows = xb.shape[0] // PACK_ROWS
    n_blocks = n_rows // EXPERT_BLOCK
    xrow = lambda i, be, bv, nu: (jnp.maximum(jnp.minimum(i, nu[0] - 1), 0), 0)
    wsel = lambda i, be, bv, nu: (be[i], 0, 0)
    grid_spec = pltpu.PrefetchScalarGridSpec(
        num_scalar_prefetch=3,
        grid=(n_blocks,),
        in_specs=[
            pl.BlockSpec((EXPERT_BLOCK * PACK_ROWS, LANES), xrow),
            pl.BlockSpec((None, D_MODEL, D_EXPERT), wsel),
            pl.BlockSpec((None, D_MODEL, D_EXPERT), wsel),
            pl.BlockSpec((None, D_EXPERT, D_MODEL), wsel),
        ],
        out_specs=pl.BlockSpec((EXPERT_BLOCK * OUT_ROWS, LANES), lambda i, be, bv, nu: (i, 0)),
    )
    return pl.pallas_call(
        _experts_kernel,
        grid_spec=grid_spec,
        out_shape=jax.ShapeDtypeStruct((n_rows * OUT_ROWS, LANES), F32),
        compiler_params=_params("arbitrary"),
        name="experts",
    )(block_e, block_valid, n_used, xb, w_gate, w_up, w_down)


def _combine_kernel(d0_ref, d1_ref, d0a_ref, d1a_ref, d0b_ref, d1b_ref, x1_ref, p_ref, comb_ref, gpl_ref, wplg_ref,
                    wplp_ref, gfin_ref, yb_ref, out_ref, gbuf, x3_ref, sem):
    tf = x1_ref.shape[0]
    i = pl.program_id(0)
    n_steps = pl.num_programs(0)
    slot = lax.rem(i, GATHER_BUFS)
    slot_a = lax.rem(i + 1, GATHER_BUFS)
    slot_b = lax.rem(i + 2, GATHER_BUFS)

    def start_gather(d_refs, s, t):
        row = pl.multiple_of(t * OUT_ROWS, OUT_ROWS)
        for k, d_ref in enumerate(d_refs):
            src = pl.multiple_of(d_ref[0, 0, t] * OUT_ROWS, OUT_ROWS)
            pltpu.make_async_copy(yb_ref.at[pl.ds(src, OUT_ROWS), :], gbuf.at[s, k, pl.ds(row, OUT_ROWS), :],
                                  sem.at[s, k]).start(priority=k)

    def wait_gathers(s):
        for k in range(2):
            pltpu.make_async_copy(yb_ref.at[pl.ds(0, tf * OUT_ROWS), :], gbuf.at[s, k], sem.at[s, k]).wait()

    @pl.when(i == 0)
    def _():
        lax.fori_loop(0, tf, lambda t, c: (start_gather((d0_ref, d1_ref), 0, t), c)[1], 0, unroll=DMA_UNROLL)
        lax.fori_loop(0, tf, lambda t, c: (start_gather((d0a_ref, d1a_ref), 1, t), c)[1], 0, unroll=DMA_UNROLL)

    wait_gathers(slot)

    comb = comb_ref[...]
    g1, g2 = (jnp.concatenate([gbuf[slot, k, pl.ds(j, tf, stride=OUT_ROWS), :] for j in range(OUT_ROWS)], axis=1)
              for k in range(2))
    y = comb[:, 0:1] * g1 + comb[:, 1:2] * g2
    x2 = x1_ref[...] + y
    hb = _rms(x2, gpl_ref[...]).astype(BF16)
    pb = p_ref[...].astype(BF16)
    n_col = D_MODEL // COMBINE_COLS
    for c in range(n_col):
        for t in range(c * tf // n_col, (c + 1) * tf // n_col):
            start_gather((d0b_ref, d1b_ref), slot_b, t)
        cs = slice(c * COMBINE_COLS, (c + 1) * COMBINE_COLS)
        gate = jax.nn.sigmoid(_dot(hb, wplg_ref[:, cs]))
        x3_ref[:, cs] = x2[:, cs] + gate * _dot(pb, wplp_ref[:, cs])
    x3 = x3_ref[...]
    out_ref[...] = _rms(x3, gfin_ref[...])

    @pl.when(i == n_steps - 1)
    def _():
        wait_gathers(slot_a)
        wait_gathers(slot_b)


def _combine(dest0, dest1, x1, p2, comb, g_pl, w_plg, w_plp, g_final, yb, tf):
    n = x1.shape[0]
    n_steps = n // tf
    assert n_steps >= GATHER_BUFS - 1
    const = lambda i: (0, 0)
    rows = lambda i: (i, 0)
    ahead = lambda a: (lambda i: (jnp.minimum(i + a, n_steps - 1), 0, 0))
    dest_specs = [pl.BlockSpec((1, 1, tf), ahead(a), memory_space=pltpu.SMEM) for a in range(GATHER_BUFS)
                  for _ in range(2)]
    return pl.pallas_call(
        _combine_kernel,
        grid=(n_steps,),
        in_specs=dest_specs + [
            pl.BlockSpec((tf, D_MODEL), rows),
            pl.BlockSpec((tf, PLE_DIM), rows),
            pl.BlockSpec((tf, LANES), rows),
            pl.BlockSpec((1, D_MODEL), const),
            pl.BlockSpec((D_MODEL, D_MODEL), const),
            pl.BlockSpec((PLE_DIM, D_MODEL), const),
            pl.BlockSpec((1, D_MODEL), const),
            pl.BlockSpec(memory_space=pl.ANY),
        ],
        out_specs=pl.BlockSpec((tf, D_MODEL), rows),
        out_shape=jax.ShapeDtypeStruct((n, D_MODEL), F32),
        scratch_shapes=[pltpu.VMEM((GATHER_BUFS, 2, tf * OUT_ROWS, LANES), F32), pltpu.VMEM((tf, D_MODEL), F32),
                        pltpu.SemaphoreType.DMA((GATHER_BUFS, 2))],
        compiler_params=_params("arbitrary"),
        name="combine",
    )(dest0, dest1, dest0, dest1, dest0, dest1, x1, p2, comb, g_pl, w_plg, w_plp, g_final, yb)


def _layer(x2, p2, batch, seq, g_mix, w_in, b_mgate, conv_qk, g_mlstm, lb, g_hgrn, w_out, g_ffn,
           w_rg, b_rg, w_re, b_re, w_e_gate, w_e_up, w_e_down, g_pl, w_pl_gate, w_pl_proj, g_out):
    n = x2.shape[0]
    n_chunks = seq // CHUNK
    m_cols = 4 * W_MIX
    n_gate = 2 * N_HEADS

    w_in_b = w_in.astype(BF16)
    w_m = w_in_b[:, :m_cols]
    w_gcols = w_in_b[:, m_cols:m_cols + n_gate]
    w_h = w_in_b[:, m_cols + n_gate:]
    w_g = jnp.pad(w_gcols, ((0, 0), (0, LANES - n_gate)))
    b_g = jnp.pad(b_mgate.astype(F32), (0, LANES - n_gate))[None, :]
    zm, zh, gate, gate_t = _in_proj(x2, g_mix[None, :], w_m, w_h, w_g, w_gcols.T, b_g,
                                    b_mgate.astype(F32)[:, None], conv_qk, lb[None, :], seq, tm=min(512, seq))

    y = _mixer(zm, zh, gate, gate_t, g_mlstm[None, :], g_hgrn[None, :], batch, n_chunks)

    n_logit = N_GROUPS + N_EXPERTS
    w_rt = jnp.pad(jnp.concatenate([w_rg, w_re], axis=1).T, ((0, ROUTER_ROWS - n_logit), (0, 0)))
    wr_hi = w_rt.astype(BF16)
    wr_lo = (w_rt - wr_hi.astype(F32)).astype(BF16)
    b_r = jnp.pad(jnp.concatenate([b_rg, b_re]), (0, ROUTER_ROWS - n_logit))[:, None]
    x1, hn, route_t, comb, counts = _post_mix(x2, y, w_out.astype(BF16), g_ffn[None, :], wr_hi, wr_lo, b_r,
                                              tm=min(512, n))

    counts = counts[N_GROUPS:n_logit, 0].astype(I32)
    padded = (counts + EXPERT_BLOCK - 1) // EXPERT_BLOCK * EXPERT_BLOCK
    pend = jnp.cumsum(padded)
    pstart = pend - padded
    n_blocks = (2 * n) // EXPERT_BLOCK + N_EXPERTS
    n_rows = n_blocks * EXPERT_BLOCK
    n_used = (pend[-1] // EXPERT_BLOCK).astype(I32)
    block_ids = jnp.arange(n_blocks, dtype=I32)
    block_row = jnp.minimum(block_ids, n_used - 1) * EXPERT_BLOCK
    block_e = jnp.sum((pend[None, :] <= block_row[:, None]).astype(I32), axis=1)
    token_end = jnp.sum(jnp.where(block_e[:, None] == jnp.arange(N_EXPERTS, dtype=I32), pstart + counts, 0), axis=1)
    block_valid = jnp.where(block_ids < n_used, jnp.clip(token_end - block_row, 0, EXPERT_BLOCK), 0).astype(I32)
    expert_id = route_t[0:2].astype(I32)
    hot = expert_id[:, :, None] == jnp.arange(N_EXPERTS, dtype=I32)
    dest = jnp.sum(jnp.where(hot, pstart, 0), axis=-1) + route_t[2:4].astype(I32)

    ts = min(512, n)
    xb = _dispatch(dest[0].reshape(n // ts, 1, ts), dest[1].reshape(n // ts, 1, ts), hn, n_rows, ts)
    yb = _experts(block_e, block_valid, n_used[None], xb, w_e_gate.astype(BF16), w_e_up.astype(BF16),
                  w_e_down.astype(BF16))
    tf = min(256, n)
    return _combine(dest[0].reshape(n // tf, 1, tf), dest[1].reshape(n // tf, 1, tf), x1, p2, comb, g_pl[None, :],
                    w_pl_gate.astype(BF16), w_pl_proj.astype(BF16), g_out[None, :], yb, tf)


def kernel(x, p, g_mix, w_in, b_mgate, conv_qk, g_mlstm, hg_lb, g_hgrn, w_out, g_ffn, w_rg, b_rg, w_re, b_re,
           w_e_gate, w_e_up, w_e_down, g_pl, w_pl_gate, w_pl_proj, g_final):
    batch, seq, d = x.shape
    depth = p.shape[0]
    assert depth == 1, "the fused final norm assumes a single layer"
    lower_bounds = jnp.cumsum(jax.nn.softmax(hg_lb.astype(F32), axis=0), axis=0)
    i = 0
    out = _layer(x.reshape(batch * seq, d), p[i].reshape(batch * seq, PLE_DIM), batch, seq,
                 g_mix[i], w_in[i], b_mgate[i], conv_qk[i], g_mlstm[i], lower_bounds[i], g_hgrn[i], w_out[i],
                 g_ffn[i], w_rg[i], b_rg[i], w_re[i], b_re[i], w_e_gate[i], w_e_up[i], w_e_down[i],
                 g_pl[i], w_pl_gate[i], w_pl_proj[i], g_final)
    return out.reshape(batch, seq, d)
```

```python
import functools

import jax
import jax.numpy as jnp
from jax import lax
from jax.experimental import pallas as pl
from jax.experimental.pallas import tpu as pltpu

F32 = jnp.float32
BF16 = jnp.bfloat16
I32 = jnp.int32
U32 = jnp.uint32
EPS = 1e-6

LANES = 128
D_MODEL = 1024
W_MIX = 512
N_HEADS = 4
HEAD_DIM = 128
N_GROUPS = 4
EXPERTS_PER_GROUP = 8
N_EXPERTS = N_GROUPS * EXPERTS_PER_GROUP
D_EXPERT = 512
PLE_DIM = 256
CONV_WIDTH = 4
CHUNK = 128
EXPERT_BLOCK = 512
PACK_ROWS = D_MODEL // 2 // LANES
ROUTER_ROWS = 48
DMA_UNROLL = 4
COMBINE_COLS = 256
MIX_BATCH = 2
IN_PROJ_COLS = 256
IN_PROJ_AHEAD = 1
GATHER_BUFS = 3
VMEM_LIMIT = 56 * 1024 * 1024


def _dot(a, b):
    return jnp.dot(a, b, preferred_element_type=F32)


def _dot_nt(a, b):
    return lax.dot_general(a, b, (((1,), (1,)), ((), ())), preferred_element_type=F32)


def _dot_tn(a, b):
    return lax.dot_general(a, b, (((0,), (0,)), ((), ())), preferred_element_type=F32)


def _rms(u, g):
    return u * lax.rsqrt(jnp.mean(u * u, axis=-1, keepdims=True) + EPS) * g


def _silu(u):
    return u * jax.nn.sigmoid(u)


def _log_sigmoid(u):
    return jnp.minimum(u, 0.0) - jnp.log1p(jnp.exp(-jnp.abs(u)))


def _split_hi_lo(u):
    hi = u.astype(BF16)
    return hi, (u - hi.astype(F32)).astype(BF16)


def _params(*sem):
    return pltpu.CompilerParams(dimension_semantics=sem, vmem_limit_bytes=VMEM_LIMIT)


def _in_proj_kernel(tiles_per_seq, x_ref, g_ref, wm_ref, wh_ref, wg_ref, wgt_ref, bg_ref, bgt_ref, conv_ref, lb_ref,
                    zm_ref, zh_ref, gate_ref, gate_t_ref, cbuf):
    tm = x_ref.shape[0]
    W = W_MIX
    CB = IN_PROJ_COLS

    @pl.when(pl.program_id(0) % tiles_per_seq == 0)
    def _():
        cbuf[0:8, :] = jnp.zeros((8, 2 * W), F32)

    h = _rms(x_ref[...], g_ref[...]).astype(BF16)

    def conv_finish(col0, scale):
        def finish(z):
            cs = slice(col0, col0 + CB)
            cbuf[8:8 + tm, cs] = z
            acc = z * conv_ref[CONV_WIDTH - 1:CONV_WIDTH, cs]
            for j in range(CONV_WIDTH - 1):
                acc = acc + cbuf[5 + j:5 + j + tm, cs] * conv_ref[j:j + 1, cs]
            cbuf[0:8, cs] = cbuf[tm:tm + 8, cs]
            zm_ref[:, cs] = (_silu(acc) * scale).astype(BF16)
        return finish

    def store(ref, col0, fn):
        def finish(z):
            ref[:, col0:col0 + CB] = fn(z).astype(BF16)
        return finish

    def forget_finish(col0):
        def finish(z):
            lb = lb_ref[:, col0:col0 + CB]
            zh_ref[:, W + col0:W + col0 + CB] = ((1.0 - lb) * jax.nn.sigmoid(-z)).astype(BF16)
            lf_hi, lf_lo = _split_hi_lo(jnp.log(lb + (1.0 - lb) * jax.nn.sigmoid(z)))
            zh_ref[:, 4 * W + col0:4 * W + col0 + CB] = lf_hi
            zh_ref[:, 5 * W + col0:5 * W + col0 + CB] = lf_lo
        return finish

    ident = lambda z: z
    plan = []
    for half in range(W // CB):
        c0 = half * CB
        plan += [
            (wm_ref, c0, conv_finish(c0, 1.0)),
            (wh_ref, 2 * W + c0, store(zh_ref, 2 * W + c0, ident)),
            (wm_ref, W + c0, conv_finish(W + c0, HEAD_DIM ** -0.5)),
            (wm_ref, 2 * W + c0, store(zm_ref, 2 * W + c0, ident)),
            (wh_ref, W + c0, forget_finish(c0)),
            (wm_ref, 3 * W + c0, store(zm_ref, 3 * W + c0, jax.nn.sigmoid)),
            (wh_ref, c0, store(zh_ref, c0, _silu)),
            (wh_ref, 3 * W + c0, store(zh_ref, 3 * W + c0, _silu)),
        ]

    project = lambda i: _dot(h, plan[i][0][:, plan[i][1]:plan[i][1] + CB])
    zs = [project(i) for i in range(IN_PROJ_AHEAD)]
    for i in range(len(plan)):
        if i + IN_PROJ_AHEAD < len(plan):
            zs.append(project(i + IN_PROJ_AHEAD))
        plan[i][2](zs[i])

    gate_ref[...] = _dot(h, wg_ref[...]) + bg_ref[...]
    gate_t_ref[...] = _dot_nt(wgt_ref[...], h) + bgt_ref[...]


def _in_proj(x2, g_mix, w_m, w_h, w_g, w_gt, b_g, b_gt, conv_w, lb, seq, tm):
    n = x2.shape[0]
    assert seq % tm == 0
    const = lambda i: (0, 0)
    rows = lambda i: (i, 0)
    return pl.pallas_call(
        functools.partial(_in_proj_kernel, seq // tm),
        grid=(n // tm,),
        in_specs=[
            pl.BlockSpec((tm, D_MODEL), rows),
            pl.BlockSpec((1, D_MODEL), const),
            pl.BlockSpec((D_MODEL, 4 * W_MIX), const),
            pl.BlockSpec((D_MODEL, 4 * W_MIX), const),
            pl.BlockSpec((D_MODEL, LANES), const),
            pl.BlockSpec((8, D_MODEL), const),
            pl.BlockSpec((1, LANES), const),
            pl.BlockSpec((8, 1), const),
            pl.BlockSpec((CONV_WIDTH, 2 * W_MIX), const),
            pl.BlockSpec((1, W_MIX), const),
        ],
        out_specs=[
            pl.BlockSpec((tm, 4 * W_MIX), rows),
            pl.BlockSpec((tm, 6 * W_MIX), rows),
            pl.BlockSpec((tm, LANES), rows),
            pl.BlockSpec((8, tm), lambda i: (0, i)),
        ],
        out_shape=[
            jax.ShapeDtypeStruct((n, 4 * W_MIX), BF16),
            jax.ShapeDtypeStruct((n, 6 * W_MIX), BF16),
            jax.ShapeDtypeStruct((n, LANES), F32),
            jax.ShapeDtypeStruct((8, n), F32),
        ],
        scratch_shapes=[pltpu.VMEM((tm + 8, 2 * W_MIX), F32)],
        compiler_params=_params("arbitrary"),
        name="in_proj",
    )(x2, g_mix, w_m, w_h, w_g, w_gt, b_g, b_gt, conv_w, lb)


def _rows_bcast(ref, rows, span, hs):
    return jnp.concatenate([jnp.broadcast_to(ref[r:r + 1, hs], (span, HEAD_DIM)) for r in rows], axis=0)


def _head_cols(base, h):
    return slice(base * W_MIX + h * HEAD_DIM, base * W_MIX + (h + 1) * HEAD_DIM)


def _mixer_kernel(zm_ref, zh_ref, gate_ref, gt0_ref, gt1_ref, gm_ref, gh_ref, y_ref,
                  c_ref, n_ref, m_ref, st_ref, b_scr):
    L = CHUNK

    @pl.when(pl.program_id(1) == 0)
    def _():
        c_ref[...] = jnp.zeros_like(c_ref)
        n_ref[...] = jnp.zeros_like(n_ref)
        m_ref[...] = jnp.zeros_like(m_ref)
        st_ref[...] = jnp.zeros_like(st_ref)

    units = [(bi, h) for bi in range(MIX_BATCH) for h in range(N_HEADS)]
    gt_refs = (gt0_ref, gt1_ref)
    row = lax.broadcasted_iota(I32, (L, L), 0)
    col = lax.broadcasted_iota(I32, (L, L), 1)
    causal = col <= row
    lower, upper = causal.astype(BF16), (row <= col).astype(BF16)

    gate, gate_t, bcum_c, bcum_r = [], [], [], []
    for bi in range(MIX_BATCH):
        g = gate_ref[bi]
        g_t = gt_refs[bi][...]
        lf_c, lf_r = _split_hi_lo(_log_sigmoid(g)), _split_hi_lo(_log_sigmoid(g_t))
        gate.append(g)
        gate_t.append(g_t)
        bcum_c.append(_dot(lower, lf_c[0]) + _dot(lower, lf_c[1]))
        bcum_r.append(_dot(lf_r[0], upper) + _dot(lf_r[1], upper))
        b_scr[bi] = (_dot(lower, zh_ref[bi, :, 4 * W_MIX:5 * W_MIX])
                     + _dot(lower, zh_ref[bi, :, 5 * W_MIX:6 * W_MIX]))

    qk, qc = {}, {}
    for u in units:
        bi, h = u
        qb = zm_ref[bi, :, _head_cols(0, h)]
        qk[u] = _dot_nt(qb, zm_ref[bi, :, _head_cols(1, h)])
        qc[u] = _dot(qb, c_ref[bi, h].astype(BF16))

    oi, blocks = {}, {}
    for u in units:
        bi, h = u
        hs = _head_cols(0, h)
        b = b_scr[bi, :, hs]
        q = zh_ref[bi, :, hs].astype(F32)
        k = zh_ref[bi, :, _head_cols(1, h)].astype(F32)
        oi[u] = _dot_nt((q * jnp.exp(b)).astype(BF16), st_ref[bi, h].astype(BF16))
        parts = []
        span = L // 2
        while span >= 16:
            mids = range(span, L, 2 * span)
            r = _rows_bcast(b_scr.at[bi], [m for m in mids for _ in (0, 1)], span, hs)
            low = (row & span) != 0
            z = (jnp.where(low, q, k) * jnp.exp(jnp.where(low, b - r, r - b))).astype(BF16)
            keep = ((row & -(2 * span)) == (col & -(2 * span))) & low & ((col & span) == 0)
            parts.append((keep, _dot_nt(z, z)))
            span //= 2
        r = _rows_bcast(b_scr.at[bi], range(0, L, 16), 16, hs)
        p = _dot_nt((q * jnp.exp(b - r)).astype(BF16), (k * jnp.exp(r - b)).astype(BF16))
        parts.append((((row & -16) == (col & -16)) & causal, p))
        blocks[u] = parts

    sv, den, w_inter, m_t_all = {}, {}, {}, {}
    for u in units:
        bi, h = u
        bc = bcum_c[bi][:, N_HEADS + h:N_HEADS + h + 1]
        br = bcum_r[bi][N_HEADS + h:N_HEADS + h + 1, :]
        ir = gate_t[bi][h:h + 1, :]
        m_prev = m_ref[bi, h:h + 1, 0:1]
        log_d = jnp.where(causal, bc - br + ir, -jnp.inf)
        inter = bc + m_prev
        m_t = jnp.maximum(inter, jnp.max(log_d, axis=-1, keepdims=True))
        s = qk[u] * jnp.exp(log_d - m_t)
        w_inter[u] = jnp.exp(inter - m_t)
        m_t_all[u] = m_t
        q = zm_ref[bi, :, _head_cols(0, h)].astype(F32)
        den[u] = (jnp.sum(s, axis=-1, keepdims=True)
                  + w_inter[u] * jnp.sum(q * n_ref[bi, h:h + 1, :], axis=-1, keepdims=True))
        sv[u] = _dot(s.astype(BF16), zm_ref[bi, :, _head_cols(2, h)])

    av, sk = {}, {}
    for u in units:
        bi, h = u
        hs = _head_cols(0, h)
        a = None
        for keep, p in blocks[u]:
            term = jnp.where(keep, p, 0.0)
            a = term if a is None else a + term
        v = zh_ref[bi, :, _head_cols(2, h)]
        av[u] = _dot(a.astype(BF16), v)
        k = zh_ref[bi, :, _head_cols(1, h)].astype(F32)
        ke = (k * jnp.exp(b_scr[bi, L - 1:L, hs] - b_scr[bi, :, hs])).astype(BF16)
        sk[u] = _dot_tn(v, ke)

    for u in units:
        bi, h = u
        bc = bcum_c[bi][:, N_HEADS + h:N_HEADS + h + 1]
        ic = gate[bi][:, h:h + 1]
        m_prev = m_ref[bi, h:h + 1, 0:1]
        b_last = bc[L - 1:L, :]
        w_log = b_last - bc + ic
        m_new = jnp.maximum(b_last + m_prev, jnp.max(w_log, axis=0, keepdims=True))
        k = zm_ref[bi, :, _head_cols(1, h)].astype(F32)
        kw = k * jnp.exp(w_log - m_new)
        decay = jnp.exp(b_last + m_prev - m_new)
        c_ref[bi, h] = decay * c_ref[bi, h] + _dot_tn(kw.astype(BF16), zm_ref[bi, :, _head_cols(2, h)])
        n_ref[bi, h:h + 1, :] = decay * n_ref[bi, h:h + 1, :] + jnp.sum(kw, axis=0, keepdims=True)
        m_ref[bi, h:h + 1, :] = jnp.broadcast_to(m_new, (1, LANES))

    for u in units:
        bi, h = u
        hs = _head_cols(0, h)
        num = sv[u] + w_inter[u] * qc[u]
        hh = num * (1.0 / jnp.maximum(jnp.abs(den[u]), jnp.exp(-m_t_all[u])))
        hh = hh * zm_ref[bi, :, _head_cols(3, h)].astype(F32)
        hh = hh * lax.rsqrt(jnp.mean(hh * hh, axis=-1, keepdims=True) + EPS) * gm_ref[:, hs]
        y_ref[bi, :, hs] = hh.astype(BF16)

        st_ref[bi, h] = jnp.exp(b_scr[bi, L - 1:L, hs]) * st_ref[bi, h] + sk[u]
        o = oi[u] + av[u]
        o = o * lax.rsqrt(jnp.mean(o * o, axis=-1, keepdims=True) + EPS) * gh_ref[...]
        o = o * zh_ref[bi, :, _head_cols(3, h)].astype(F32)
        y_ref[bi, :, _head_cols(1, h)] = o.astype(BF16)


def _mixer(zm, zh, gate, gate_t, g_mlstm, g_hgrn, batch, n_chunks):
    n = zm.shape[0]
    seq = n // batch
    assert MIX_BATCH == 2 and batch % MIX_BATCH == 0
    blk = lambda b, c: (b, c, 0)
    const = lambda b, c: (0, 0)
    return pl.pallas_call(
        _mixer_kernel,
        grid=(batch // MIX_BATCH, n_chunks),
        in_specs=[
            pl.BlockSpec((MIX_BATCH, CHUNK, 4 * W_MIX), blk),
            pl.BlockSpec((MIX_BATCH, CHUNK, 6 * W_MIX), blk),
            pl.BlockSpec((MIX_BATCH, CHUNK, LANES), blk),
            pl.BlockSpec((8, CHUNK), lambda b, c: (0, (MIX_BATCH * b) * n_chunks + c)),
            pl.BlockSpec((8, CHUNK), lambda b, c: (0, (MIX_BATCH * b + 1) * n_chunks + c)),
            pl.BlockSpec((1, W_MIX), const),
            pl.BlockSpec((1, HEAD_DIM), const),
        ],
        out_specs=pl.BlockSpec((MIX_BATCH, CHUNK, 2 * W_MIX), blk),
        out_shape=jax.ShapeDtypeStruct((batch, seq, 2 * W_MIX), BF16),
        scratch_shapes=[
            pltpu.VMEM((MIX_BATCH, N_HEADS, HEAD_DIM, HEAD_DIM), F32),
            pltpu.VMEM((MIX_BATCH, 8, HEAD_DIM), F32),
            pltpu.VMEM((MIX_BATCH, 8, LANES), F32),
            pltpu.VMEM((MIX_BATCH, N_HEADS, HEAD_DIM, HEAD_DIM), F32),
            pltpu.VMEM((MIX_BATCH, CHUNK, W_MIX), F32),
        ],
        compiler_params=_params("parallel", "arbitrary"),
        name="mixer",
    )(zm.reshape(batch, seq, 4 * W_MIX), zh.reshape(batch, seq, 6 * W_MIX), gate.reshape(batch, seq, LANES),
      gate_t, gate_t, g_mlstm, g_hgrn).reshape(n, 2 * W_MIX)


def _pack_bf16_pair(lo, hi):
    lo_bits = pltpu.bitcast(lo.astype(BF16).astype(F32), U32)
    hi_bits = pltpu.bitcast(hi.astype(BF16).astype(F32), U32)
    return (hi_bits & jnp.uint32(0xFFFF0000)) | (lo_bits >> 16)


def _unpack_bf16_pair(w):
    lo = pltpu.bitcast(w << 16, F32).astype(BF16)
    hi = pltpu.bitcast(w & jnp.uint32(0xFFFF0000), F32).astype(BF16)
    return lo, hi


def _post_mix_kernel(x_ref, y_ref, wo_ref, g_ref, wrh_ref, wrl_ref, br_ref,
                     x1_ref, hn_ref, route_t_ref, comb_ref, count_ref, run_ref):
    tm = x_ref.shape[0]
    n_r = wrh_ref.shape[0]

    @pl.when(pl.program_id(0) == 0)
    def _():
        run_ref[...] = jnp.zeros_like(run_ref)

    x1 = x_ref[...] + _dot(y_ref[...], wo_ref[...])
    x1_ref[...] = x1
    hn = _rms(x1, g_ref[...])
    half = D_MODEL // 2
    packed = _pack_bf16_pair(hn[:, :half], hn[:, half:])
    for j in range(PACK_ROWS):
        hn_ref[pl.ds(j, tm, stride=PACK_ROWS), :] = packed[:, j * LANES:(j + 1) * LANES]

    hn_hi = hn.astype(BF16)
    hn_lo = (hn - hn_hi.astype(F32)).astype(BF16)
    logits = (_dot_nt(wrh_ref[...], hn_hi) + (_dot_nt(wrl_ref[...], hn_hi) + _dot_nt(wrh_ref[...], hn_lo))
              + br_ref[...])
    rix = lax.broadcasted_iota(I32, (n_r, tm), 0)
    neg = -jnp.inf
    g_l = jnp.where(rix < N_GROUPS, logits, neg)
    g_max = jnp.max(g_l, axis=0, keepdims=True)
    g_sel = jnp.min(jnp.where(g_l == g_max, rix, n_r), axis=0, keepdims=True)
    g_val = 1.0 / jnp.sum(jnp.exp(g_l - g_max), axis=0, keepdims=True)

    e_row = rix - N_GROUPS
    in_group = (e_row >= g_sel * EXPERTS_PER_GROUP) & (e_row < (g_sel + 1) * EXPERTS_PER_GROUP)
    e_l = jnp.where(in_group, logits, neg)
    v1 = jnp.max(e_l, axis=0, keepdims=True)
    i1 = jnp.min(jnp.where(e_l == v1, rix, n_r), axis=0, keepdims=True)
    e_l2 = jnp.where(rix == i1, neg, e_l)
    v2 = jnp.max(e_l2, axis=0, keepdims=True)
    i2 = jnp.min(jnp.where(e_l2 == v2, rix, n_r), axis=0, keepdims=True)
    t = jnp.exp(v2 - v1)
    c1 = g_val / (1.0 + t)
    c2 = g_val * t / (1.0 + t)

    hot1 = rix == i1
    hot2 = rix == i2
    hot = (hot1 | hot2).astype(F32)
    r_i = lax.broadcasted_iota(I32, (tm, tm), 0)
    c_i = lax.broadcasted_iota(I32, (tm, tm), 1)
    before = _dot(hot.astype(BF16), (r_i < c_i).astype(BF16)) + run_ref[:, 0:1]
    rank1 = jnp.sum(jnp.where(hot1, before, 0.0), axis=0, keepdims=True)
    rank2 = jnp.sum(jnp.where(hot2, before, 0.0), axis=0, keepdims=True)
    run_ref[...] = run_ref[...] + jnp.sum(hot, axis=1, keepdims=True)
    count_ref[...] = run_ref[...]

    r8 = lax.broadcasted_iota(I32, (8, tm), 0)
    out = jnp.where(r8 == 0, (i1 - N_GROUPS).astype(F32), 0.0)
    out = jnp.where(r8 == 1, (i2 - N_GROUPS).astype(F32), out)
    out = jnp.where(r8 == 2, rank1, out)
    out = jnp.where(r8 == 3, rank2, out)
    route_t_ref[...] = out

    r128 = lax.broadcasted_iota(I32, (LANES, tm), 0)
    slab = jnp.where(r128 == 0, c1, jnp.where(r128 == 1, c2, 0.0))
    for c in range(tm // LANES):
        comb_ref[c * LANES:(c + 1) * LANES, :] = slab[:, c * LANES:(c + 1) * LANES].T


def _post_mix(x2, y, w_out, g_ffn, wr_hi, wr_lo, b_r, tm):
    n = x2.shape[0]
    n_r = wr_hi.shape[0]
    const = lambda i: (0, 0)
    rows = lambda i: (i, 0)
    return pl.pallas_call(
        _post_mix_kernel,
        grid=(n // tm,),
        in_specs=[
            pl.BlockSpec((tm, D_MODEL), rows),
            pl.BlockSpec((tm, 2 * W_MIX), rows),
            pl.BlockSpec((2 * W_MIX, D_MODEL), const),
            pl.BlockSpec((1, D_MODEL), const),
            pl.BlockSpec((n_r, D_MODEL), const),
            pl.BlockSpec((n_r, D_MODEL), const),
            pl.BlockSpec((n_r, 1), const),
        ],
        out_specs=[
            pl.BlockSpec((tm, D_MODEL), rows),
            pl.BlockSpec((tm * PACK_ROWS, LANES), rows),
            pl.BlockSpec((8, tm), lambda i: (0, i)),
            pl.BlockSpec((tm, LANES), rows),
            pl.BlockSpec((n_r, LANES), const),
        ],
        out_shape=[
            jax.ShapeDtypeStruct((n, D_MODEL), F32),
            jax.ShapeDtypeStruct((n * PACK_ROWS, LANES), U32),
            jax.ShapeDtypeStruct((8, n), F32),
            jax.ShapeDtypeStruct((n, LANES), F32),
            jax.ShapeDtypeStruct((n_r, LANES), F32),
        ],
        scratch_shapes=[pltpu.VMEM((n_r, LANES), F32)],
        compiler_params=_params("arbitrary"),
        name="post_mix",
    )(x2, y, w_out, g_ffn, wr_hi, wr_lo, b_r)


def _dispatch_kernel(d0_ref, d1_ref, hn_ref, xb_init_ref, wg_ref, wu_ref, wd_ref,
                     xb_ref, wgb_ref, wub_ref, wdb_ref, sem):
    del xb_init_ref
    ts = hn_ref.shape[0] // PACK_ROWS

    def start(t, carry):
        src = pl.multiple_of(t * PACK_ROWS, PACK_ROWS)
        for k, d_ref in enumerate((d0_ref, d1_ref)):
            dst = pl.multiple_of(d_ref[0, 0, t] * PACK_ROWS, PACK_ROWS)
            pltpu.make_async_copy(hn_ref.at[pl.ds(src, PACK_ROWS), :], xb_ref.at[pl.ds(dst, PACK_ROWS), :],
                                  sem.at[k]).start(priority=k)
        return carry

    lax.fori_loop(0, ts, start, 0, unroll=DMA_UNROLL)
    wgb_ref[...] = wg_ref[...].astype(BF16)
    wub_ref[...] = wu_ref[...].astype(BF16)
    wdb_ref[...] = wd_ref[...].astype(BF16)

    for k in range(2):
        pltpu.make_async_copy(hn_ref, xb_ref.at[pl.ds(0, ts * PACK_ROWS), :], sem.at[k]).wait()


def _dispatch(dest0, dest1, hn, w_gate, w_up, w_down, n_rows, ts):
    n = hn.shape[0] // PACK_ROWS
    steps = n // ts
    if steps >= N_EXPERTS:
        parts = steps // N_EXPERTS
        assert steps == parts * N_EXPERTS
        w_spec = lambda rows, cols: pl.BlockSpec((None, rows // parts, cols), lambda i: (i // parts, i % parts, 0))
    else:
        per_step = N_EXPERTS // steps
        assert N_EXPERTS == per_step * steps
        w_spec = lambda rows, cols: pl.BlockSpec((per_step, rows, cols), lambda i: (i, 0, 0))
    w_specs = [w_spec(D_MODEL, D_EXPERT), w_spec(D_MODEL, D_EXPERT), w_spec(D_EXPERT, D_MODEL)]
    return pl.pallas_call(
        _dispatch_kernel,
        grid=(steps,),
        in_specs=[
            pl.BlockSpec((1, 1, ts), lambda i: (i, 0, 0), memory_space=pltpu.SMEM),
            pl.BlockSpec((1, 1, ts), lambda i: (i, 0, 0), memory_space=pltpu.SMEM),
            pl.BlockSpec((ts * PACK_ROWS, LANES), lambda i: (i, 0)),
            pl.BlockSpec(memory_space=pl.ANY),
        ] + w_specs,
        out_specs=[pl.BlockSpec(memory_space=pl.ANY)] + w_specs,
        out_shape=[jax.ShapeDtypeStruct((n_rows * PACK_ROWS, LANES), U32)]
        + [jax.ShapeDtypeStruct(w.shape, BF16) for w in (w_gate, w_up, w_down)],
        scratch_shapes=[pltpu.SemaphoreType.DMA((2,))],
        input_output_aliases={3: 0},
        compiler_params=_params("arbitrary"),
        name="dispatch",
    )(dest0, dest1, hn, jnp.zeros((n_rows * PACK_ROWS, LANES), U32), w_gate, w_up, w_down)


def _experts_kernel(be_ref, bv_ref, nu_ref, xb_ref, wg_ref, wu_ref, wd_ref, yb_ref):
    valid = bv_ref[pl.program_id(0)]
    half = EXPERT_BLOCK // 2

    def mlp(rows):
        pairs = [_unpack_bf16_pair(xb_ref[pl.ds(j, rows, stride=PACK_ROWS), :]) for j in range(PACK_ROWS)]
        x = jnp.concatenate([lo for lo, _ in pairs] + [hi for _, hi in pairs], axis=1)
        a = (_silu(_dot(x, wg_ref[...])) * _dot(x, wu_ref[...])).astype(BF16)
        y = _dot(a, wd_ref[...])
        half_d = D_MODEL // 2
        packed = _pack_bf16_pair(y[:, :half_d], y[:, half_d:])
        for j in range(PACK_ROWS):
            yb_ref[pl.ds(j, rows, stride=PACK_ROWS), :] = packed[:, j * LANES:(j + 1) * LANES]

    @pl.when(valid > half)
    def _():
        mlp(EXPERT_BLOCK)

    @pl.when((valid > 0) & (valid <= half))
    def _():
        mlp(half)
        yb_ref[half * PACK_ROWS:, :] = jnp.zeros((half * PACK_ROWS, LANES), U32)

    @pl.when(valid == 0)
    def _():
        yb_ref[...] = jnp.zeros_like(yb_ref)


def _experts(block_e, block_valid, n_used, xb, w_gate, w_up, w_down):
    n_rows = xb.shape[0] // PACK_ROWS
    n_blocks = n_rows // EXPERT_BLOCK
    xrow = lambda i, be, bv, nu: (jnp.maximum(jnp.minimum(i, nu[0] - 1), 0), 0)
    wsel = lambda i, be, bv, nu: (be[i], 0, 0)
    grid_spec = pltpu.PrefetchScalarGridSpec(
        num_scalar_prefetch=3,
        grid=(n_blocks,),
        in_specs=[
            pl.BlockSpec((EXPERT_BLOCK * PACK_ROWS, LANES), xrow),
            pl.BlockSpec((None, D_MODEL, D_EXPERT), wsel),
            pl.BlockSpec((None, D_MODEL, D_EXPERT), wsel),
            pl.BlockSpec((None, D_EXPERT, D_MODEL), wsel),
        ],
        out_specs=pl.BlockSpec((EXPERT_BLOCK * PACK_ROWS, LANES), lambda i, be, bv, nu: (i, 0)),
    )
    return pl.pallas_call(
        _experts_kernel,
        grid_spec=grid_spec,
        out_shape=jax.ShapeDtypeStruct((n_rows * PACK_ROWS, LANES), U32),
        compiler_params=_params("arbitrary"),
        name="experts",
    )(block_e, block_valid, n_used, xb, w_gate, w_up, w_down)


def _combine_kernel(d0_ref, d1_ref, d0a_ref, d1a_ref, d0b_ref, d1b_ref, x1_ref, p_ref, comb_ref, gpl_ref, wplg_ref,
                    wplp_ref, gfin_ref, yb_ref, out_ref, gbuf, x3_ref, sem):
    tf = x1_ref.shape[0]
    i = pl.program_id(0)
    n_steps = pl.num_programs(0)
    slot = lax.rem(i, GATHER_BUFS)
    slot_a = lax.rem(i + 1, GATHER_BUFS)
    slot_b = lax.rem(i + 2, GATHER_BUFS)

    def start_gather(d_refs, s, t):
        row = pl.multiple_of(t * PACK_ROWS, PACK_ROWS)
        for k, d_ref in enumerate(d_refs):
            src = pl.multiple_of(d_ref[0, 0, t] * PACK_ROWS, PACK_ROWS)
            pltpu.make_async_copy(yb_ref.at[pl.ds(src, PACK_ROWS), :], gbuf.at[s, k, pl.ds(row, PACK_ROWS), :],
                                  sem.at[s, k]).start(priority=k)

    def wait_gathers(s):
        for k in range(2):
            pltpu.make_async_copy(yb_ref.at[pl.ds(0, tf * PACK_ROWS), :], gbuf.at[s, k], sem.at[s, k]).wait()

    @pl.when(i == 0)
    def _():
        lax.fori_loop(0, tf, lambda t, c: (start_gather((d0_ref, d1_ref), 0, t), c)[1], 0, unroll=DMA_UNROLL)
        lax.fori_loop(0, tf, lambda t, c: (start_gather((d0a_ref, d1a_ref), 1, t), c)[1], 0, unroll=DMA_UNROLL)

    wait_gathers(slot)

    comb = comb_ref[...]
    def gathered(k):
        words = [gbuf[slot, k, pl.ds(j, tf, stride=PACK_ROWS), :] for j in range(PACK_ROWS)]
        return jnp.concatenate([pltpu.bitcast(w << 16, F32) for w in words]
                               + [pltpu.bitcast(w & jnp.uint32(0xFFFF0000), F32) for w in words], axis=1)

    g1, g2 = gathered(0), gathered(1)
    y = comb[:, 0:1] * g1 + comb[:, 1:2] * g2
    x2 = x1_ref[...] + y
    hb = _rms(x2, gpl_ref[...]).astype(BF16)
    pb = p_ref[...].astype(BF16)
    n_col = D_MODEL // COMBINE_COLS
    for c in range(n_col):
        for t in range(c * tf // n_col, (c + 1) * tf // n_col):
            start_gather((d0b_ref, d1b_ref), slot_b, t)
        cs = slice(c * COMBINE_COLS, (c + 1) * COMBINE_COLS)
        gate = jax.nn.sigmoid(_dot(hb, wplg_ref[:, cs]))
        x3_ref[:, cs] = x2[:, cs] + gate * _dot(pb, wplp_ref[:, cs])
    x3 = x3_ref[...]
    out_ref[...] = _rms(x3, gfin_ref[...])

    @pl.when(i == n_steps - 1)
    def _():
        wait_gathers(slot_a)
        wait_gathers(slot_b)


def _combine(dest0, dest1, x1, p2, comb, g_pl, w_plg, w_plp, g_final, yb, tf):
    n = x1.shape[0]
    n_steps = n // tf
    assert n_steps >= GATHER_BUFS - 1
    const = lambda i: (0, 0)
    rows = lambda i: (i, 0)
    ahead = lambda a: (lambda i: (jnp.minimum(i + a, n_steps - 1), 0, 0))
    dest_specs = [pl.BlockSpec((1, 1, tf), ahead(a), memory_space=pltpu.SMEM) for a in range(GATHER_BUFS)
                  for _ in range(2)]
    return pl.pallas_call(
        _combine_kernel,
        grid=(n_steps,),
        in_specs=dest_specs + [
            pl.BlockSpec((tf, D_MODEL), rows),
            pl.BlockSpec((tf, PLE_DIM), rows),
            pl.BlockSpec((tf, LANES), rows),
            pl.BlockSpec((1, D_MODEL), const),
            pl.BlockSpec((D_MODEL, D_MODEL), const),
            pl.BlockSpec((PLE_DIM, D_MODEL), const),
            pl.BlockSpec((1, D_MODEL), const),
            pl.BlockSpec(memory_space=pl.ANY),
        ],
        out_specs=pl.BlockSpec((tf, D_MODEL), rows),
        out_shape=jax.ShapeDtypeStruct((n, D_MODEL), F32),
        scratch_shapes=[pltpu.VMEM((GATHER_BUFS, 2, tf * PACK_ROWS, LANES), U32), pltpu.VMEM((tf, D_MODEL), F32),
                        pltpu.SemaphoreType.DMA((GATHER_BUFS, 2))],
        compiler_params=_params("arbitrary"),
        name="combine",
    )(dest0, dest1, dest0, dest1, dest0, dest1, x1, p2, comb, g_pl, w_plg, w_plp, g_final, yb)


def _layer(x2, p2, batch, seq, g_mix, w_in, b_mgate, conv_qk, g_mlstm, lb, g_hgrn, w_out, g_ffn,
           w_rg, b_rg, w_re, b_re, w_e_gate, w_e_up, w_e_down, g_pl, w_pl_gate, w_pl_proj, g_out):
    n = x2.shape[0]
    n_chunks = seq // CHUNK
    m_cols = 4 * W_MIX
    n_gate = 2 * N_HEADS

    w_in_b = w_in.astype(BF16)
    w_m = w_in_b[:, :m_cols]
    w_gcols = w_in_b[:, m_cols:m_cols + n_gate]
    w_h = w_in_b[:, m_cols + n_gate:]
    w_g = jnp.pad(w_gcols, ((0, 0), (0, LANES - n_gate)))
    b_g = jnp.pad(b_mgate.astype(F32), (0, LANES - n_gate))[None, :]
    zm, zh, gate, gate_t = _in_proj(x2, g_mix[None, :], w_m, w_h, w_g, w_gcols.T, b_g,
                                    b_mgate.astype(F32)[:, None], conv_qk, lb[None, :], seq, tm=min(512, seq))

    y = _mixer(zm, zh, gate, gate_t, g_mlstm[None, :], g_hgrn[None, :], batch, n_chunks)

    n_logit = N_GROUPS + N_EXPERTS
    w_rt = jnp.pad(jnp.concatenate([w_rg, w_re], axis=1).T, ((0, ROUTER_ROWS - n_logit), (0, 0)))
    wr_hi = w_rt.astype(BF16)
    wr_lo = (w_rt - wr_hi.astype(F32)).astype(BF16)
    b_r = jnp.pad(jnp.concatenate([b_rg, b_re]), (0, ROUTER_ROWS - n_logit))[:, None]
    x1, hn, route_t, comb, counts = _post_mix(x2, y, w_out.astype(BF16), g_ffn[None, :], wr_hi, wr_lo, b_r,
                                              tm=min(512, n))

    counts = counts[N_GROUPS:n_logit, 0].astype(I32)
    padded = (counts + EXPERT_BLOCK - 1) // EXPERT_BLOCK * EXPERT_BLOCK
    pend = jnp.cumsum(padded)
    pstart = pend - padded
    n_blocks = (2 * n) // EXPERT_BLOCK + N_EXPERTS
    n_rows = n_blocks * EXPERT_BLOCK
    n_used = (pend[-1] // EXPERT_BLOCK).astype(I32)
    block_ids = jnp.arange(n_blocks, dtype=I32)
    block_row = jnp.minimum(block_ids, n_used - 1) * EXPERT_BLOCK
    block_e = jnp.sum((pend[None, :] <= block_row[:, None]).astype(I32), axis=1)
    token_end = jnp.sum(jnp.where(block_e[:, None] == jnp.arange(N_EXPERTS, dtype=I32), pstart + counts, 0), axis=1)
    block_valid = jnp.where(block_ids < n_used, jnp.clip(token_end - block_row, 0, EXPERT_BLOCK), 0).astype(I32)
    expert_id = route_t[0:2].astype(I32)
    hot = expert_id[:, :, None] == jnp.arange(N_EXPERTS, dtype=I32)
    dest = jnp.sum(jnp.where(hot, pstart, 0), axis=-1) + route_t[2:4].astype(I32)

    ts = min(512, n)
    xb, wg_b, wu_b, wd_b = _dispatch(dest[0].reshape(n // ts, 1, ts), dest[1].reshape(n // ts, 1, ts), hn,
                                     w_e_gate, w_e_up, w_e_down, n_rows, ts)
    yb = _experts(block_e, block_valid, n_used[None], xb, wg_b, wu_b, wd_b)
    tf = min(256, n)
    return _combine(dest[0].reshape(n // tf, 1, tf), dest[1].reshape(n // tf, 1, tf), x1, p2, comb, g_pl[None, :],
                    w_pl_gate.astype(BF16), w_pl_proj.astype(BF16), g_out[None, :], yb, tf)


def kernel(x, p, g_mix, w_in, b_mgate, conv_qk, g_mlstm, hg_lb, g_hgrn, w_out, g_ffn, w_rg, b_rg, w_re, b_re,
           w_e_gate, w_e_up, w_e_down, g_pl, w_pl_gate, w_pl_proj, g_final):
    batch, seq, d = x.shape
    depth = p.shape[0]
    assert depth == 1, "the fused final norm assumes a single layer"
    lower_bounds = jnp.cumsum(jax.nn.softmax(hg_lb.astype(F32), axis=0), axis=0)
    i = 0
    out = _layer(x.reshape(batch * seq, d), p[i].reshape(batch * seq, PLE_DIM), batch, seq,
                 g_mix[i], w_in[i], b_mgate[i], conv_qk[i], g_mlstm[i], lower_bounds[i], g_hgrn[i], w_out[i],
                 g_ffn[i], w_rg[i], b_rg[i], w_re[i], b_re[i], w_e_gate[i], w_e_up[i], w_e_down[i],
                 g_pl[i], w_pl_gate[i], w_pl_proj[i], g_final)
    return out.reshape(batch, seq, d)
```

```python
import functools

import jax
import jax.numpy as jnp
from jax import lax
from jax.experimental import pallas as pl
from jax.experimental.pallas import tpu as pltpu

F32 = jnp.float32
BF16 = jnp.bfloat16
I32 = jnp.int32
U32 = jnp.uint32
EPS = 1e-6

LANES = 128
D_MODEL = 1024
W_MIX = 512
N_HEADS = 4
HEAD_DIM = 128
N_GROUPS = 4
EXPERTS_PER_GROUP = 8
N_EXPERTS = N_GROUPS * EXPERTS_PER_GROUP
D_EXPERT = 512
PLE_DIM = 256
CONV_WIDTH = 4
CHUNK = 128
EXPERT_BLOCK = 512
PACK_ROWS = D_MODEL // 2 // LANES
ROUTER_ROWS = 48
DMA_UNROLL = 4
COMBINE_COLS = 256
MIX_BATCH = 2
IN_PROJ_COLS = 256
IN_PROJ_AHEAD = 1
GATHER_BUFS = 3
VMEM_LIMIT = 56 * 1024 * 1024


def _dot(a, b):
    return jnp.dot(a, b, preferred_element_type=F32)


def _dot_nt(a, b):
    return lax.dot_general(a, b, (((1,), (1,)), ((), ())), preferred_element_type=F32)


def _dot_tn(a, b):
    return lax.dot_general(a, b, (((0,), (0,)), ((), ())), preferred_element_type=F32)


def _rms(u, g):
    return u * lax.rsqrt(jnp.mean(u * u, axis=-1, keepdims=True) + EPS) * g


def _silu(u):
    return u * jax.nn.sigmoid(u)


def _log_sigmoid(u):
    return jnp.minimum(u, 0.0) - jnp.log1p(jnp.exp(-jnp.abs(u)))


def _split_hi_lo(u):
    hi = u.astype(BF16)
    return hi, (u - hi.astype(F32)).astype(BF16)


def _params(*sem):
    return pltpu.CompilerParams(dimension_semantics=sem, vmem_limit_bytes=VMEM_LIMIT)


def _in_proj_kernel(tiles_per_seq, x_ref, g_ref, wm_ref, wh_ref, wg_ref, wgt_ref, bg_ref, bgt_ref, conv_ref, lb_ref,
                    zm_ref, zh_ref, gate_ref, gate_t_ref, cbuf):
    tm = x_ref.shape[0]
    W = W_MIX
    CB = IN_PROJ_COLS

    @pl.when(pl.program_id(0) % tiles_per_seq == 0)
    def _():
        cbuf[0:8, :] = jnp.zeros((8, 2 * W), F32)

    h = _rms(x_ref[...], g_ref[...]).astype(BF16)

    def conv_finish(col0, scale):
        def finish(z):
            cs = slice(col0, col0 + CB)
            cbuf[8:8 + tm, cs] = z
            acc = z * conv_ref[CONV_WIDTH - 1:CONV_WIDTH, cs]
            for j in range(CONV_WIDTH - 1):
                acc = acc + cbuf[5 + j:5 + j + tm, cs] * conv_ref[j:j + 1, cs]
            cbuf[0:8, cs] = cbuf[tm:tm + 8, cs]
            zm_ref[:, cs] = (_silu(acc) * scale).astype(BF16)
        return finish

    def store(ref, col0, fn):
        def finish(z):
            ref[:, col0:col0 + CB] = fn(z).astype(BF16)
        return finish

    def forget_finish(col0):
        def finish(z):
            lb = lb_ref[:, col0:col0 + CB]
            zh_ref[:, W + col0:W + col0 + CB] = ((1.0 - lb) * jax.nn.sigmoid(-z)).astype(BF16)
            lf_hi, lf_lo = _split_hi_lo(jnp.log(lb + (1.0 - lb) * jax.nn.sigmoid(z)))
            zh_ref[:, 4 * W + col0:4 * W + col0 + CB] = lf_hi
            zh_ref[:, 5 * W + col0:5 * W + col0 + CB] = lf_lo
        return finish

    ident = lambda z: z
    plan = []
    for half in range(W // CB):
        c0 = half * CB
        plan += [
            (wm_ref, c0, conv_finish(c0, 1.0)),
            (wh_ref, 2 * W + c0, store(zh_ref, 2 * W + c0, ident)),
            (wm_ref, W + c0, conv_finish(W + c0, HEAD_DIM ** -0.5)),
            (wm_ref, 2 * W + c0, store(zm_ref, 2 * W + c0, ident)),
            (wh_ref, W + c0, forget_finish(c0)),
            (wm_ref, 3 * W + c0, store(zm_ref, 3 * W + c0, jax.nn.sigmoid)),
            (wh_ref, c0, store(zh_ref, c0, _silu)),
            (wh_ref, 3 * W + c0, store(zh_ref, 3 * W + c0, _silu)),
        ]

    project = lambda i: _dot(h, plan[i][0][:, plan[i][1]:plan[i][1] + CB])
    zs = [project(i) for i in range(IN_PROJ_AHEAD)]
    for i in range(len(plan)):
        if i + IN_PROJ_AHEAD < len(plan):
            zs.append(project(i + IN_PROJ_AHEAD))
        plan[i][2](zs[i])

    gate_ref[...] = _dot(h, wg_ref[...]) + bg_ref[...]
    gate_t_ref[...] = _dot_nt(wgt_ref[...], h) + bgt_ref[...]


def _in_proj(x2, g_mix, w_m, w_h, w_g, w_gt, b_g, b_gt, conv_w, lb, seq, tm):
    n = x2.shape[0]
    assert seq % tm == 0
    const = lambda i: (0, 0)
    rows = lambda i: (i, 0)
    return pl.pallas_call(
        functools.partial(_in_proj_kernel, seq // tm),
        grid=(n // tm,),
        in_specs=[
            pl.BlockSpec((tm, D_MODEL), rows),
            pl.BlockSpec((1, D_MODEL), const),
            pl.BlockSpec((D_MODEL, 4 * W_MIX), const),
            pl.BlockSpec((D_MODEL, 4 * W_MIX), const),
            pl.BlockSpec((D_MODEL, LANES), const),
            pl.BlockSpec((8, D_MODEL), const),
            pl.BlockSpec((1, LANES), const),
            pl.BlockSpec((8, 1), const),
            pl.BlockSpec((CONV_WIDTH, 2 * W_MIX), const),
            pl.BlockSpec((1, W_MIX), const),
        ],
        out_specs=[
            pl.BlockSpec((tm, 4 * W_MIX), rows),
            pl.BlockSpec((tm, 6 * W_MIX), rows),
            pl.BlockSpec((tm, LANES), rows),
            pl.BlockSpec((8, tm), lambda i: (0, i)),
        ],
        out_shape=[
            jax.ShapeDtypeStruct((n, 4 * W_MIX), BF16),
            jax.ShapeDtypeStruct((n, 6 * W_MIX), BF16),
            jax.ShapeDtypeStruct((n, LANES), F32),
            jax.ShapeDtypeStruct((8, n), F32),
        ],
        scratch_shapes=[pltpu.VMEM((tm + 8, 2 * W_MIX), F32)],
        compiler_params=_params("arbitrary"),
        name="in_proj",
    )(x2, g_mix, w_m, w_h, w_g, w_gt, b_g, b_gt, conv_w, lb)


def _rows_bcast(ref, rows, span, hs):
    return jnp.concatenate([jnp.broadcast_to(ref[r:r + 1, hs], (span, HEAD_DIM)) for r in rows], axis=0)


def _head_cols(base, h):
    return slice(base * W_MIX + h * HEAD_DIM, base * W_MIX + (h + 1) * HEAD_DIM)


def _lane_block(j):
    return slice(j * LANES, (j + 1) * LANES)


def _mixer_kernel(zm_ref, zh_ref, gate_ref, gt0_ref, gt1_ref, gm_ref, gh_ref, sel_ref, y_ref,
                  c_ref, n_ref, m_ref, st_ref, b_scr):
    L = CHUNK

    @pl.when(pl.program_id(1) == 0)
    def _():
        c_ref[...] = jnp.zeros_like(c_ref)
        n_ref[...] = jnp.zeros_like(n_ref)
        m_ref[...] = jnp.zeros_like(m_ref)
        st_ref[...] = jnp.zeros_like(st_ref)

    units = [(bi, h) for bi in range(MIX_BATCH) for h in range(N_HEADS)]
    gt_refs = (gt0_ref, gt1_ref)
    row = lax.broadcasted_iota(I32, (L, L), 0)
    col = lax.broadcasted_iota(I32, (L, L), 1)
    causal = col <= row
    lower, upper = causal.astype(BF16), (row <= col).astype(BF16)

    gate_t, cols, bcum_r = [], [], []
    for bi in range(MIX_BATCH):
        g = gate_ref[bi]
        g_t = gt_refs[bi][...]
        lf_c, lf_r = _split_hi_lo(_log_sigmoid(g)), _split_hi_lo(_log_sigmoid(g_t))
        gate_t.append(g_t)
        bcum_c = _dot(lower, lf_c[0]) + _dot(lower, lf_c[1])
        bcum_r.append(_dot(lf_r[0], upper) + _dot(lf_r[1], upper))
        mixed = _split_hi_lo(jnp.where(col < N_HEADS, g, bcum_c))
        cols.append(_dot(mixed[0], sel_ref[...]) + _dot(mixed[1], sel_ref[...]))
        b_scr[bi] = (_dot(lower, zh_ref[bi, :, 4 * W_MIX:5 * W_MIX])
                     + _dot(lower, zh_ref[bi, :, 5 * W_MIX:6 * W_MIX]))

    qk, qc = {}, {}
    for u in units:
        bi, h = u
        qb = zm_ref[bi, :, _head_cols(0, h)]
        qk[u] = _dot_nt(qb, zm_ref[bi, :, _head_cols(1, h)])
        qc[u] = _dot(qb, c_ref[bi, h].astype(BF16))

    oi, blocks = {}, {}
    for u in units:
        bi, h = u
        hs = _head_cols(0, h)
        b = b_scr[bi, :, hs]
        q = zh_ref[bi, :, hs].astype(F32)
        k = zh_ref[bi, :, _head_cols(1, h)].astype(F32)
        oi[u] = _dot_nt((q * jnp.exp(b)).astype(BF16), st_ref[bi, h].astype(BF16))
        parts = []
        span = L // 2
        while span >= 16:
            mids = range(span, L, 2 * span)
            r = _rows_bcast(b_scr.at[bi], [m for m in mids for _ in (0, 1)], span, hs)
            low = (row & span) != 0
            z = (jnp.where(low, q, k) * jnp.exp(jnp.where(low, b - r, r - b))).astype(BF16)
            keep = ((row & -(2 * span)) == (col & -(2 * span))) & low & ((col & span) == 0)
            parts.append((keep, _dot_nt(z, z)))
            span //= 2
        r = _rows_bcast(b_scr.at[bi], range(0, L, 16), 16, hs)
        p = _dot_nt((q * jnp.exp(b - r)).astype(BF16), (k * jnp.exp(r - b)).astype(BF16))
        parts.append((((row & -16) == (col & -16)) & causal, p))
        blocks[u] = parts

    log_d, inter, row_max, qn = {}, {}, {}, {}
    for u in units:
        bi, h = u
        bc = cols[bi][:, _lane_block(N_HEADS + h)]
        br = bcum_r[bi][N_HEADS + h:N_HEADS + h + 1, :]
        ir = gate_t[bi][h:h + 1, :]
        log_d[u] = jnp.where(causal, bc - br + ir, -jnp.inf)
        inter[u] = bc + m_ref[bi, h:h + 1, :]
        row_max[u] = jnp.max(log_d[u], axis=-1, keepdims=True)
        q = zm_ref[bi, :, _head_cols(0, h)].astype(F32)
        qn[u] = jnp.sum(q * n_ref[bi, h:h + 1, :], axis=-1, keepdims=True)

    sv, row_sum, w_inter, m_t_all = {}, {}, {}, {}
    for u in units:
        bi, h = u
        m_t = jnp.maximum(inter[u], row_max[u])
        s = qk[u] * jnp.exp(log_d[u] - m_t)
        w_inter[u] = jnp.exp(inter[u] - m_t)
        m_t_all[u] = m_t
        row_sum[u] = jnp.sum(s, axis=-1, keepdims=True)
        sv[u] = _dot(s.astype(BF16), zm_ref[bi, :, _head_cols(2, h)])

    av, sk = {}, {}
    for u in units:
        bi, h = u
        hs = _head_cols(0, h)
        a = None
        for keep, p in blocks[u]:
            term = jnp.where(keep, p, 0.0)
            a = term if a is None else a + term
        v = zh_ref[bi, :, _head_cols(2, h)]
        av[u] = _dot(a.astype(BF16), v)
        k = zh_ref[bi, :, _head_cols(1, h)].astype(F32)
        ke = (k * jnp.exp(b_scr[bi, L - 1:L, hs] - b_scr[bi, :, hs])).astype(BF16)
        sk[u] = _dot_tn(v, ke)

    for u in units:
        bi, h = u
        bc = cols[bi][:, _lane_block(N_HEADS + h)]
        ic = cols[bi][:, _lane_block(h)]
        m_prev = m_ref[bi, h:h + 1, :]
        b_last = bc[L - 1:L, :]
        w_log = b_last - bc + ic
        m_new = jnp.maximum(b_last + m_prev, jnp.max(w_log, axis=0, keepdims=True))
        k = zm_ref[bi, :, _head_cols(1, h)].astype(F32)
        kw = k * jnp.exp(w_log - m_new)
        decay = jnp.exp(b_last + m_prev - m_new)
        c_ref[bi, h] = decay * c_ref[bi, h] + _dot_tn(kw.astype(BF16), zm_ref[bi, :, _head_cols(2, h)])
        n_ref[bi, h:h + 1, :] = decay * n_ref[bi, h:h + 1, :] + jnp.sum(kw, axis=0, keepdims=True)
        m_ref[bi, h:h + 1, :] = m_new

    hh_all, o_all, hh_ms, o_ms = {}, {}, {}, {}
    for u in units:
        bi, h = u
        hs = _head_cols(0, h)
        den = row_sum[u] + w_inter[u] * qn[u]
        num = sv[u] + w_inter[u] * qc[u]
        hh = num * (1.0 / jnp.maximum(jnp.abs(den), jnp.exp(-m_t_all[u])))
        hh = hh * zm_ref[bi, :, _head_cols(3, h)].astype(F32)
        hh_all[u] = hh
        hh_ms[u] = jnp.mean(hh * hh, axis=-1, keepdims=True)

        st_ref[bi, h] = jnp.exp(b_scr[bi, L - 1:L, hs]) * st_ref[bi, h] + sk[u]
        o = oi[u] + av[u]
        o_all[u] = o
        o_ms[u] = jnp.mean(o * o, axis=-1, keepdims=True)

    for u in units:
        bi, h = u
        hs = _head_cols(0, h)
        y_ref[bi, :, hs] = (hh_all[u] * lax.rsqrt(hh_ms[u] + EPS) * gm_ref[:, hs]).astype(BF16)
        o = o_all[u] * lax.rsqrt(o_ms[u] + EPS) * gh_ref[...]
        o = o * zh_ref[bi, :, _head_cols(3, h)].astype(F32)
        y_ref[bi, :, _head_cols(1, h)] = o.astype(BF16)


def _mixer(zm, zh, gate, gate_t, g_mlstm, g_hgrn, batch, n_chunks):
    n = zm.shape[0]
    seq = n // batch
    assert MIX_BATCH == 2 and batch % MIX_BATCH == 0
    blk = lambda b, c: (b, c, 0)
    const = lambda b, c: (0, 0)
    sel = (jnp.arange(LANES)[:, None] == jnp.arange(2 * N_HEADS * LANES)[None, :] // LANES).astype(BF16)
    return pl.pallas_call(
        _mixer_kernel,
        grid=(batch // MIX_BATCH, n_chunks),
        in_specs=[
            pl.BlockSpec((MIX_BATCH, CHUNK, 4 * W_MIX), blk),
            pl.BlockSpec((MIX_BATCH, CHUNK, 6 * W_MIX), blk),
            pl.BlockSpec((MIX_BATCH, CHUNK, LANES), blk),
            pl.BlockSpec((8, CHUNK), lambda b, c: (0, (MIX_BATCH * b) * n_chunks + c)),
            pl.BlockSpec((8, CHUNK), lambda b, c: (0, (MIX_BATCH * b + 1) * n_chunks + c)),
            pl.BlockSpec((1, W_MIX), const),
            pl.BlockSpec((1, HEAD_DIM), const),
            pl.BlockSpec((LANES, 2 * N_HEADS * LANES), const),
        ],
        out_specs=pl.BlockSpec((MIX_BATCH, CHUNK, 2 * W_MIX), blk),
        out_shape=jax.ShapeDtypeStruct((batch, seq, 2 * W_MIX), BF16),
        scratch_shapes=[
            pltpu.VMEM((MIX_BATCH, N_HEADS, HEAD_DIM, HEAD_DIM), F32),
            pltpu.VMEM((MIX_BATCH, 8, HEAD_DIM), F32),
            pltpu.VMEM((MIX_BATCH, 8, LANES), F32),
            pltpu.VMEM((MIX_BATCH, N_HEADS, HEAD_DIM, HEAD_DIM), F32),
            pltpu.VMEM((MIX_BATCH, CHUNK, W_MIX), F32),
        ],
        compiler_params=_params("parallel", "arbitrary"),
        name="mixer",
    )(zm.reshape(batch, seq, 4 * W_MIX), zh.reshape(batch, seq, 6 * W_MIX), gate.reshape(batch, seq, LANES),
      gate_t, gate_t, g_mlstm, g_hgrn, sel).reshape(n, 2 * W_MIX)


def _pack_bf16_pair(lo, hi):
    lo_bits = pltpu.bitcast(lo.astype(BF16).astype(F32), U32)
    hi_bits = pltpu.bitcast(hi.astype(BF16).astype(F32), U32)
    return (hi_bits & jnp.uint32(0xFFFF0000)) | (lo_bits >> 16)


def _unpack_bf16_pair(w):
    lo = pltpu.bitcast(w << 16, F32).astype(BF16)
    hi = pltpu.bitcast(w & jnp.uint32(0xFFFF0000), F32).astype(BF16)
    return lo, hi


def _post_mix_kernel(x_ref, y_ref, wo_ref, g_ref, wrh_ref, wrl_ref, br_ref,
                     x1_ref, hn_ref, route_t_ref, comb_ref, count_ref, run_ref):
    tm = x_ref.shape[0]
    n_r = wrh_ref.shape[0]

    @pl.when(pl.program_id(0) == 0)
    def _():
        run_ref[...] = jnp.zeros_like(run_ref)

    x1 = x_ref[...] + _dot(y_ref[...], wo_ref[...])
    x1_ref[...] = x1
    hn = _rms(x1, g_ref[...])
    half = D_MODEL // 2
    packed = _pack_bf16_pair(hn[:, :half], hn[:, half:])
    for j in range(PACK_ROWS):
        hn_ref[pl.ds(j, tm, stride=PACK_ROWS), :] = packed[:, j * LANES:(j + 1) * LANES]

    hn_hi = hn.astype(BF16)
    hn_lo = (hn - hn_hi.astype(F32)).astype(BF16)
    logits = (_dot_nt(wrh_ref[...], hn_hi) + (_dot_nt(wrl_ref[...], hn_hi) + _dot_nt(wrh_ref[...], hn_lo))
              + br_ref[...])
    rix = lax.broadcasted_iota(I32, (n_r, tm), 0)
    neg = -jnp.inf
    g_l = jnp.where(rix < N_GROUPS, logits, neg)
    g_max = jnp.max(g_l, axis=0, keepdims=True)
    g_sel = jnp.min(jnp.where(g_l == g_max, rix, n_r), axis=0, keepdims=True)
    g_val = 1.0 / jnp.sum(jnp.exp(g_l - g_max), axis=0, keepdims=True)

    e_row = rix - N_GROUPS
    in_group = (e_row >= g_sel * EXPERTS_PER_GROUP) & (e_row < (g_sel + 1) * EXPERTS_PER_GROUP)
    e_l = jnp.where(in_group, logits, neg)
    v1 = jnp.max(e_l, axis=0, keepdims=True)
    i1 = jnp.min(jnp.where(e_l == v1, rix, n_r), axis=0, keepdims=True)
    e_l2 = jnp.where(rix == i1, neg, e_l)
    v2 = jnp.max(e_l2, axis=0, keepdims=True)
    i2 = jnp.min(jnp.where(e_l2 == v2, rix, n_r), axis=0, keepdims=True)
    t = jnp.exp(v2 - v1)
    c1 = g_val / (1.0 + t)
    c2 = g_val * t / (1.0 + t)

    hot1 = rix == i1
    hot2 = rix == i2
    hot = (hot1 | hot2).astype(F32)
    r_i = lax.broadcasted_iota(I32, (tm, tm), 0)
    c_i = lax.broadcasted_iota(I32, (tm, tm), 1)
    before = _dot(hot.astype(BF16), (r_i < c_i).astype(BF16)) + run_ref[:, 0:1]
    rank1 = jnp.sum(jnp.where(hot1, before, 0.0), axis=0, keepdims=True)
    rank2 = jnp.sum(jnp.where(hot2, before, 0.0), axis=0, keepdims=True)
    run_ref[...] = run_ref[...] + jnp.sum(hot, axis=1, keepdims=True)
    count_ref[...] = run_ref[...]

    r8 = lax.broadcasted_iota(I32, (8, tm), 0)
    out = jnp.where(r8 == 0, (i1 - N_GROUPS).astype(F32), 0.0)
    out = jnp.where(r8 == 1, (i2 - N_GROUPS).astype(F32), out)
    out = jnp.where(r8 == 2, rank1, out)
    out = jnp.where(r8 == 3, rank2, out)
    route_t_ref[...] = out

    r128 = lax.broadcasted_iota(I32, (LANES, tm), 0)
    slab = jnp.where(r128 == 0, c1, jnp.where(r128 == 1, c2, 0.0))
    for c in range(tm // LANES):
        comb_ref[c * LANES:(c + 1) * LANES, :] = slab[:, c * LANES:(c + 1) * LANES].T


def _post_mix(x2, y, w_out, g_ffn, wr_hi, wr_lo, b_r, tm):
    n = x2.shape[0]
    n_r = wr_hi.shape[0]
    const = lambda i: (0, 0)
    rows = lambda i: (i, 0)
    return pl.pallas_call(
        _post_mix_kernel,
        grid=(n // tm,),
        in_specs=[
            pl.BlockSpec((tm, D_MODEL), rows),
            pl.BlockSpec((tm, 2 * W_MIX), rows),
            pl.BlockSpec((2 * W_MIX, D_MODEL), const),
            pl.BlockSpec((1, D_MODEL), const),
            pl.BlockSpec((n_r, D_MODEL), const),
            pl.BlockSpec((n_r, D_MODEL), const),
            pl.BlockSpec((n_r, 1), const),
        ],
        out_specs=[
            pl.BlockSpec((tm, D_MODEL), rows),
            pl.BlockSpec((tm * PACK_ROWS, LANES), rows),
            pl.BlockSpec((8, tm), lambda i: (0, i)),
            pl.BlockSpec((tm, LANES), rows),
            pl.BlockSpec((n_r, LANES), const),
        ],
        out_shape=[
            jax.ShapeDtypeStruct((n, D_MODEL), F32),
            jax.ShapeDtypeStruct((n * PACK_ROWS, LANES), U32),
            jax.ShapeDtypeStruct((8, n), F32),
            jax.ShapeDtypeStruct((n, LANES), F32),
            jax.ShapeDtypeStruct((n_r, LANES), F32),
        ],
        scratch_shapes=[pltpu.VMEM((n_r, LANES), F32)],
        compiler_params=_params("arbitrary"),
        name="post_mix",
    )(x2, y, w_out, g_ffn, wr_hi, wr_lo, b_r)


def _dispatch_kernel(d0_ref, d1_ref, hn_ref, xb_init_ref, wg_ref, wu_ref, wd_ref,
                     xb_ref, wgb_ref, wub_ref, wdb_ref, sem):
    del xb_init_ref
    ts = hn_ref.shape[0] // PACK_ROWS

    def start(t, carry):
        src = pl.multiple_of(t * PACK_ROWS, PACK_ROWS)
        for k, d_ref in enumerate((d0_ref, d1_ref)):
            dst = pl.multiple_of(d_ref[0, 0, t] * PACK_ROWS, PACK_ROWS)
            pltpu.make_async_copy(hn_ref.at[pl.ds(src, PACK_ROWS), :], xb_ref.at[pl.ds(dst, PACK_ROWS), :],
                                  sem.at[k]).start(priority=k)
        return carry

    lax.fori_loop(0, ts, start, 0, unroll=DMA_UNROLL)
    wgb_ref[...] = wg_ref[...].astype(BF16)
    wub_ref[...] = wu_ref[...].astype(BF16)
    wdb_ref[...] = wd_ref[...].astype(BF16)

    for k in range(2):
        pltpu.make_async_copy(hn_ref, xb_ref.at[pl.ds(0, ts * PACK_ROWS), :], sem.at[k]).wait()


def _dispatch(dest0, dest1, hn, w_gate, w_up, w_down, n_rows, ts):
    n = hn.shape[0] // PACK_ROWS
    steps = n // ts
    if steps >= N_EXPERTS:
        parts = steps // N_EXPERTS
        assert steps == parts * N_EXPERTS
        w_spec = lambda rows, cols: pl.BlockSpec((None, rows // parts, cols), lambda i: (i // parts, i % parts, 0))
    else:
        per_step = N_EXPERTS // steps
        assert N_EXPERTS == per_step * steps
        w_spec = lambda rows, cols: pl.BlockSpec((per_step, rows, cols), lambda i: (i, 0, 0))
    w_specs = [w_spec(D_MODEL, D_EXPERT), w_spec(D_MODEL, D_EXPERT), w_spec(D_EXPERT, D_MODEL)]
    return pl.pallas_call(
        _dispatch_kernel,
        grid=(steps,),
        in_specs=[
            pl.BlockSpec((1, 1, ts), lambda i: (i, 0, 0), memory_space=pltpu.SMEM),
            pl.BlockSpec((1, 1, ts), lambda i: (i, 0, 0), memory_space=pltpu.SMEM),
            pl.BlockSpec((ts * PACK_ROWS, LANES), lambda i: (i, 0)),
            pl.BlockSpec(memory_space=pl.ANY),
        ] + w_specs,
        out_specs=[pl.BlockSpec(memory_space=pl.ANY)] + w_specs,
        out_shape=[jax.ShapeDtypeStruct((n_rows * PACK_ROWS, LANES), U32)]
        + [jax.ShapeDtypeStruct(w.shape, BF16) for w in (w_gate, w_up, w_down)],
        scratch_shapes=[pltpu.SemaphoreType.DMA((2,))],
        input_output_aliases={3: 0},
        compiler_params=_params("arbitrary"),
        name="dispatch",
    )(dest0, dest1, hn, jnp.zeros((n_rows * PACK_ROWS, LANES), U32), w_gate, w_up, w_down)


def _experts_kernel(be_ref, bv_ref, nu_ref, xb_ref, wg_ref, wu_ref, wd_ref, yb_ref):
    valid = bv_ref[pl.program_id(0)]
    half = EXPERT_BLOCK // 2

    def mlp(rows):
        pairs = [_unpack_bf16_pair(xb_ref[pl.ds(j, rows, stride=PACK_ROWS), :]) for j in range(PACK_ROWS)]
        x = jnp.concatenate([lo for lo, _ in pairs] + [hi for _, hi in pairs], axis=1)
        a = (_silu(_dot(x, wg_ref[...])) * _dot(x, wu_ref[...])).astype(BF16)
        y = _dot(a, wd_ref[...])
        half_d = D_MODEL // 2
        packed = _pack_bf16_pair(y[:, :half_d], y[:, half_d:])
        for j in range(PACK_ROWS):
            yb_ref[pl.ds(j, rows, stride=PACK_ROWS), :] = packed[:, j * LANES:(j + 1) * LANES]

    @pl.when(valid > half)
    def _():
        mlp(EXPERT_BLOCK)

    @pl.when((valid > 0) & (valid <= half))
    def _():
        mlp(half)
        yb_ref[half * PACK_ROWS:, :] = jnp.zeros((half * PACK_ROWS, LANES), U32)

    @pl.when(valid == 0)
    def _():
        yb_ref[...] = jnp.zeros_like(yb_ref)


def _experts(block_e, block_valid, n_used, xb, w_gate, w_up, w_down):
    n_rows = xb.shape[0] // PACK_ROWS
    n_blocks = n_rows // EXPERT_BLOCK
    xrow = lambda i, be, bv, nu: (jnp.maximum(jnp.minimum(i, nu[0] - 1), 0), 0)
    wsel = lambda i, be, bv, nu: (be[i], 0, 0)
    grid_spec = pltpu.PrefetchScalarGridSpec(
        num_scalar_prefetch=3,
        grid=(n_blocks,),
        in_specs=[
            pl.BlockSpec((EXPERT_BLOCK * PACK_ROWS, LANES), xrow),
            pl.BlockSpec((None, D_MODEL, D_EXPERT), wsel),
            pl.BlockSpec((None, D_MODEL, D_EXPERT), wsel),
            pl.BlockSpec((None, D_EXPERT, D_MODEL), wsel),
        ],
        out_specs=pl.BlockSpec((EXPERT_BLOCK * PACK_ROWS, LANES), lambda i, be, bv, nu: (i, 0)),
    )
    return pl.pallas_call(
        _experts_kernel,
        grid_spec=grid_spec,
        out_shape=jax.ShapeDtypeStruct((n_rows * PACK_ROWS, LANES), U32),
        compiler_params=_params("arbitrary"),
        name="experts",
    )(block_e, block_valid, n_used, xb, w_gate, w_up, w_down)


def _combine_kernel(d0_ref, d1_ref, d0a_ref, d1a_ref, d0b_ref, d1b_ref, x1_ref, p_ref, comb_ref, gpl_ref, wplg_ref,
                    wplp_ref, gfin_ref, yb_ref, out_ref, gbuf, x3_ref, sem):
    tf = x1_ref.shape[0]
    i = pl.program_id(0)
    n_steps = pl.num_programs(0)
    slot = lax.rem(i, GATHER_BUFS)
    slot_a = lax.rem(i + 1, GATHER_BUFS)
    slot_b = lax.rem(i + 2, GATHER_BUFS)

    def start_gather(d_refs, s, t):
        row = pl.multiple_of(t * PACK_ROWS, PACK_ROWS)
        for k, d_ref in enumerate(d_refs):
            src = pl.multiple_of(d_ref[0, 0, t] * PACK_ROWS, PACK_ROWS)
            pltpu.make_async_copy(yb_ref.at[pl.ds(src, PACK_ROWS), :], gbuf.at[s, k, pl.ds(row, PACK_ROWS), :],
                                  sem.at[s, k]).start(priority=k)

    def wait_gathers(s):
        for k in range(2):
            pltpu.make_async_copy(yb_ref.at[pl.ds(0, tf * PACK_ROWS), :], gbuf.at[s, k], sem.at[s, k]).wait()

    @pl.when(i == 0)
    def _():
        lax.fori_loop(0, tf, lambda t, c: (start_gather((d0_ref, d1_ref), 0, t), c)[1], 0, unroll=DMA_UNROLL)
        lax.fori_loop(0, tf, lambda t, c: (start_gather((d0a_ref, d1a_ref), 1, t), c)[1], 0, unroll=DMA_UNROLL)

    wait_gathers(slot)

    comb = comb_ref[...]
    def gathered(k):
        words = [gbuf[slot, k, pl.ds(j, tf, stride=PACK_ROWS), :] for j in range(PACK_ROWS)]
        return jnp.concatenate([pltpu.bitcast(w << 16, F32) for w in words]
                               + [pltpu.bitcast(w & jnp.uint32(0xFFFF0000), F32) for w in words], axis=1)

    g1, g2 = gathered(0), gathered(1)
    y = comb[:, 0:1] * g1 + comb[:, 1:2] * g2
    x2 = x1_ref[...] + y
    hb = _rms(x2, gpl_ref[...]).astype(BF16)
    pb = p_ref[...].astype(BF16)
    n_col = D_MODEL // COMBINE_COLS
    for c in range(n_col):
        for t in range(c * tf // n_col, (c + 1) * tf // n_col):
            start_gather((d0b_ref, d1b_ref), slot_b, t)
        cs = slice(c * COMBINE_COLS, (c + 1) * COMBINE_COLS)
        gate = jax.nn.sigmoid(_dot(hb, wplg_ref[:, cs]))
        x3_ref[:, cs] = x2[:, cs] + gate * _dot(pb, wplp_ref[:, cs])
    x3 = x3_ref[...]
    out_ref[...] = _rms(x3, gfin_ref[...])

    @pl.when(i == n_steps - 1)
    def _():
        wait_gathers(slot_a)
        wait_gathers(slot_b)


def _combine(dest0, dest1, x1, p2, comb, g_pl, w_plg, w_plp, g_final, yb, tf):
    n = x1.shape[0]
    n_steps = n // tf
    assert n_steps >= GATHER_BUFS - 1
    const = lambda i: (0, 0)
    rows = lambda i: (i, 0)
    ahead = lambda a: (lambda i: (jnp.minimum(i + a, n_steps - 1), 0, 0))
    dest_specs = [pl.BlockSpec((1, 1, tf), ahead(a), memory_space=pltpu.SMEM) for a in range(GATHER_BUFS)
                  for _ in range(2)]
    return pl.pallas_call(
        _combine_kernel,
        grid=(n_steps,),
        in_specs=dest_specs + [
            pl.BlockSpec((tf, D_MODEL), rows),
            pl.BlockSpec((tf, PLE_DIM), rows),
            pl.BlockSpec((tf, LANES), rows),
            pl.BlockSpec((1, D_MODEL), const),
            pl.BlockSpec((D_MODEL, D_MODEL), const),
            pl.BlockSpec((PLE_DIM, D_MODEL), const),
            pl.BlockSpec((1, D_MODEL), const),
            pl.BlockSpec(memory_space=pl.ANY),
        ],
        out_specs=pl.BlockSpec((tf, D_MODEL), rows),
        out_shape=jax.ShapeDtypeStruct((n, D_MODEL), F32),
        scratch_shapes=[pltpu.VMEM((GATHER_BUFS, 2, tf * PACK_ROWS, LANES), U32), pltpu.VMEM((tf, D_MODEL), F32),
                        pltpu.SemaphoreType.DMA((GATHER_BUFS, 2))],
        compiler_params=_params("arbitrary"),
        name="combine",
    )(dest0, dest1, dest0, dest1, dest0, dest1, x1, p2, comb, g_pl, w_plg, w_plp, g_final, yb)


def _layer(x2, p2, batch, seq, g_mix, w_in, b_mgate, conv_qk, g_mlstm, lb, g_hgrn, w_out, g_ffn,
           w_rg, b_rg, w_re, b_re, w_e_gate, w_e_up, w_e_down, g_pl, w_pl_gate, w_pl_proj, g_out):
    n = x2.shape[0]
    n_chunks = seq // CHUNK
    m_cols = 4 * W_MIX
    n_gate = 2 * N_HEADS

    w_in_b = w_in.astype(BF16)
    w_m = w_in_b[:, :m_cols]
    w_gcols = w_in_b[:, m_cols:m_cols + n_gate]
    w_h = w_in_b[:, m_cols + n_gate:]
    w_g = jnp.pad(w_gcols, ((0, 0), (0, LANES - n_gate)))
    b_g = jnp.pad(b_mgate.astype(F32), (0, LANES - n_gate))[None, :]
    zm, zh, gate, gate_t = _in_proj(x2, g_mix[None, :], w_m, w_h, w_g, w_gcols.T, b_g,
                                    b_mgate.astype(F32)[:, None], conv_qk, lb[None, :], seq, tm=min(512, seq))

    y = _mixer(zm, zh, gate, gate_t, g_mlstm[None, :], g_hgrn[None, :], batch, n_chunks)

    n_logit = N_GROUPS + N_EXPERTS
    w_rt = jnp.pad(jnp.concatenate([w_rg, w_re], axis=1).T, ((0, ROUTER_ROWS - n_logit), (0, 0)))
    wr_hi = w_rt.astype(BF16)
    wr_lo = (w_rt - wr_hi.astype(F32)).astype(BF16)
    b_r = jnp.pad(jnp.concatenate([b_rg, b_re]), (0, ROUTER_ROWS - n_logit))[:, None]
    x1, hn, route_t, comb, counts = _post_mix(x2, y, w_out.astype(BF16), g_ffn[None, :], wr_hi, wr_lo, b_r,
                                              tm=min(512, n))

    counts = counts[N_GROUPS:n_logit, 0].astype(I32)
    padded = (counts + EXPERT_BLOCK - 1) // EXPERT_BLOCK * EXPERT_BLOCK
    pend = jnp.cumsum(padded)
    pstart = pend - padded
    n_blocks = (2 * n) // EXPERT_BLOCK + N_EXPERTS
    n_rows = n_blocks * EXPERT_BLOCK
    n_used = (pend[-1] // EXPERT_BLOCK).astype(I32)
    block_ids = jnp.arange(n_blocks, dtype=I32)
    block_row = jnp.minimum(block_ids, n_used - 1) * EXPERT_BLOCK
    block_e = jnp.sum((pend[None, :] <= block_row[:, None]).astype(I32), axis=1)
    token_end = jnp.sum(jnp.where(block_e[:, None] == jnp.arange(N_EXPERTS, dtype=I32), pstart + counts, 0), axis=1)
    block_valid = jnp.where(block_ids < n_used, jnp.clip(token_end - block_row, 0, EXPERT_BLOCK), 0).astype(I32)
    expert_id = route_t[0:2].astype(I32)
    hot = expert_id[:, :, None] == jnp.arange(N_EXPERTS, dtype=I32)
    dest = jnp.sum(jnp.where(hot, pstart, 0), axis=-1) + route_t[2:4].astype(I32)

    ts = min(512, n)
    xb, wg_b, wu_b, wd_b = _dispatch(dest[0].reshape(n // ts, 1, ts), dest[1].reshape(n // ts, 1, ts), hn,
                                     w_e_gate, w_e_up, w_e_down, n_rows, ts)
    yb = _experts(block_e, block_valid, n_used[None], xb, wg_b, wu_b, wd_b)
    tf = min(256, n)
    return _combine(dest[0].reshape(n // tf, 1, tf), dest[1].reshape(n // tf, 1, tf), x1, p2, comb, g_pl[None, :],
                    w_pl_gate.astype(BF16), w_pl_proj.astype(BF16), g_out[None, :], yb, tf)


def kernel(x, p, g_mix, w_in, b_mgate, conv_qk, g_mlstm, hg_lb, g_hgrn, w_out, g_ffn, w_rg, b_rg, w_re, b_re,
           w_e_gate, w_e_up, w_e_down, g_pl, w_pl_gate, w_pl_proj, g_final):
    batch, seq, d = x.shape
    depth = p.shape[0]
    assert depth == 1, "the fused final norm assumes a single layer"
    lower_bounds = jnp.cumsum(jax.nn.softmax(hg_lb.astype(F32), axis=0), axis=0)
    i = 0
    out = _layer(x.reshape(batch * seq, d), p[i].reshape(batch * seq, PLE_DIM), batch, seq,
                 g_mix[i], w_in[i], b_mgate[i], conv_qk[i], g_mlstm[i], lower_bounds[i], g_hgrn[i], w_out[i],
                 g_ffn[i], w_rg[i], b_rg[i], w_re[i], b_re[i], w_e_gate[i], w_e_up[i], w_e_down[i],
                 g_pl[i], w_pl_gate[i], w_pl_proj[i], g_final)
    return out.reshape(batch, seq, d)
```

```python
import functools

import jax
import jax.numpy as jnp
from jax import lax
from jax.experimental import pallas as pl
from jax.experimental.pallas import tpu as pltpu

F32 = jnp.float32
BF16 = jnp.bfloat16
I32 = jnp.int32
U32 = jnp.uint32
EPS = 1e-6

LANES = 128
D_MODEL = 1024
W_MIX = 512
N_HEADS = 4
HEAD_DIM = 128
N_GROUPS = 4
EXPERTS_PER_GROUP = 8
N_EXPERTS = N_GROUPS * EXPERTS_PER_GROUP
D_EXPERT = 512
PLE_DIM = 256
CONV_WIDTH = 4
CHUNK = 128
EXPERT_BLOCK = 512
PACK_ROWS = D_MODEL // 2 // LANES
ROUTER_ROWS = 48
DMA_UNROLL = 4
COMBINE_COLS = 256
MIX_BATCH = 2
IN_PROJ_COLS = 256
IN_PROJ_ROWS = 64
IN_PROJ_AHEAD = 1
GATHER_BUFS = 3
VMEM_LIMIT = 56 * 1024 * 1024


def _dot(a, b):
    return jnp.dot(a, b, preferred_element_type=F32)


def _dot_nt(a, b):
    return lax.dot_general(a, b, (((1,), (1,)), ((), ())), preferred_element_type=F32)


def _dot_tn(a, b):
    return lax.dot_general(a, b, (((0,), (0,)), ((), ())), preferred_element_type=F32)


def _rms(u, g):
    return u * lax.rsqrt(jnp.mean(u * u, axis=-1, keepdims=True) + EPS) * g


def _silu(u):
    return u * jax.nn.sigmoid(u)


def _log_sigmoid(u):
    return jnp.minimum(u, 0.0) - jnp.log1p(jnp.exp(-jnp.abs(u)))


def _split_hi_lo(u):
    hi = u.astype(BF16)
    return hi, (u - hi.astype(F32)).astype(BF16)


def _params(*sem):
    return pltpu.CompilerParams(dimension_semantics=sem, vmem_limit_bytes=VMEM_LIMIT)


def _in_proj_kernel(tiles_per_seq, x_ref, g_ref, wm_ref, wh_ref, wg_ref, wgt_ref, bg_ref, bgt_ref, conv_ref, lb_ref,
                    zm_ref, zh_ref, gate_ref, gate_t_ref, cbuf):
    tm = x_ref.shape[0]
    W = W_MIX
    CB = IN_PROJ_COLS

    @pl.when(pl.program_id(0) % tiles_per_seq == 0)
    def _():
        cbuf[0:8, :] = jnp.zeros((8, 2 * W), F32)

    h = _rms(x_ref[...], g_ref[...]).astype(BF16)

    def conv_finish(col0, scale):
        def finish(z):
            cs = slice(col0, col0 + CB)
            cbuf[8:8 + tm, cs] = z
            acc = z * conv_ref[CONV_WIDTH - 1:CONV_WIDTH, cs]
            for j in range(CONV_WIDTH - 1):
                acc = acc + cbuf[5 + j:5 + j + tm, cs] * conv_ref[j:j + 1, cs]
            cbuf[0:8, cs] = cbuf[tm:tm + 8, cs]
            zm_ref[:, cs] = (_silu(acc) * scale).astype(BF16)
        return finish

    def store(ref, col0, fn):
        def finish(z):
            ref[:, col0:col0 + CB] = fn(z).astype(BF16)
        return finish

    def forget_finish(col0):
        def finish(z):
            lb = lb_ref[:, col0:col0 + CB]
            zh_ref[:, W + col0:W + col0 + CB] = ((1.0 - lb) * jax.nn.sigmoid(-z)).astype(BF16)
            lf_hi, lf_lo = _split_hi_lo(jnp.log(lb + (1.0 - lb) * jax.nn.sigmoid(z)))
            zh_ref[:, 4 * W + col0:4 * W + col0 + CB] = lf_hi
            zh_ref[:, 5 * W + col0:5 * W + col0 + CB] = lf_lo
        return finish

    ident = lambda z: z
    plan = []
    for half in range(W // CB):
        c0 = half * CB
        plan += [
            (wm_ref, c0, conv_finish(c0, 1.0)),
            (wh_ref, 2 * W + c0, store(zh_ref, 2 * W + c0, ident)),
            (wm_ref, W + c0, conv_finish(W + c0, HEAD_DIM ** -0.5)),
            (wm_ref, 2 * W + c0, store(zm_ref, 2 * W + c0, ident)),
            (wh_ref, W + c0, forget_finish(c0)),
            (wm_ref, 3 * W + c0, store(zm_ref, 3 * W + c0, jax.nn.sigmoid)),
            (wh_ref, c0, store(zh_ref, c0, _silu)),
            (wh_ref, 3 * W + c0, store(zh_ref, 3 * W + c0, _silu)),
        ]

    project = lambda i: _dot(h, plan[i][0][:, plan[i][1]:plan[i][1] + CB])
    zs = [project(i) for i in range(IN_PROJ_AHEAD)]
    for i in range(len(plan)):
        if i + IN_PROJ_AHEAD < len(plan):
            zs.append(project(i + IN_PROJ_AHEAD))
        plan[i][2](zs[i])

    gate_ref[...] = _dot(h, wg_ref[...]) + bg_ref[...]
    gate_t_ref[...] = _dot_nt(wgt_ref[...], h) + bgt_ref[...]


def _in_proj(x2, g_mix, w_m, w_h, w_g, w_gt, b_g, b_gt, conv_w, lb, seq, tm):
    n = x2.shape[0]
    assert seq % tm == 0
    const = lambda i: (0, 0)
    rows = lambda i: (i, 0)
    return pl.pallas_call(
        functools.partial(_in_proj_kernel, seq // tm),
        grid=(n // tm,),
        in_specs=[
            pl.BlockSpec((tm, D_MODEL), rows),
            pl.BlockSpec((1, D_MODEL), const),
            pl.BlockSpec((D_MODEL, 4 * W_MIX), const),
            pl.BlockSpec((D_MODEL, 4 * W_MIX), const),
            pl.BlockSpec((D_MODEL, LANES), const),
            pl.BlockSpec((8, D_MODEL), const),
            pl.BlockSpec((1, LANES), const),
            pl.BlockSpec((8, 1), const),
            pl.BlockSpec((CONV_WIDTH, 2 * W_MIX), const),
            pl.BlockSpec((1, W_MIX), const),
        ],
        out_specs=[
            pl.BlockSpec((tm, 4 * W_MIX), rows),
            pl.BlockSpec((tm, 6 * W_MIX), rows),
            pl.BlockSpec((tm, LANES), rows),
            pl.BlockSpec((8, tm), lambda i: (0, i)),
        ],
        out_shape=[
            jax.ShapeDtypeStruct((n, 4 * W_MIX), BF16),
            jax.ShapeDtypeStruct((n, 6 * W_MIX), BF16),
            jax.ShapeDtypeStruct((n, LANES), F32),
            jax.ShapeDtypeStruct((8, n), F32),
        ],
        scratch_shapes=[pltpu.VMEM((tm + 8, 2 * W_MIX), F32)],
        compiler_params=_params("arbitrary"),
        name="in_proj",
    )(x2, g_mix, w_m, w_h, w_g, w_gt, b_g, b_gt, conv_w, lb)


def _in_proj_ring_kernel(tiles_per_seq, x_ref, g_ref, wm_ref, wh_ref, wg_ref, wgt_ref, bg_ref, bgt_ref, conv_ref,
                         lb_ref, zm_ref, zh_ref, gate_ref, gate_t_ref, qk0, qk1, zz0, zz1, tail):
    tm = x_ref.shape[0]
    W = W_MIX
    CB = IN_PROJ_COLS
    i = pl.program_id(0)

    @pl.when(i == 0)
    def _():
        for ref in (qk0, qk1, zz0, zz1, tail):
            ref[...] = jnp.zeros_like(ref)

    h = _rms(x_ref[...], g_ref[...]).astype(BF16)
    gate_ref[...] = _dot(h, wg_ref[...]) + bg_ref[...]
    gate_t_ref[...] = _dot_nt(wgt_ref[...], h) + bgt_ref[...]
    seq_start = lax.rem(i - 1, tiles_per_seq) == 0

    def body(wr_qk, wr_zz, rd_qk, rd_zz):
        rd_qk[0:8, :] = jnp.where(seq_start, 0.0, tail[...])

        def conv_finish(col0, scale, rs):
            cs = slice(col0, col0 + CB)
            acc = rd_qk[8 + rs.start:8 + rs.stop, cs] * conv_ref[CONV_WIDTH - 1:CONV_WIDTH, cs]
            for j in range(CONV_WIDTH - 1):
                acc = acc + rd_qk[5 + j + rs.start:5 + j + rs.stop, cs] * conv_ref[j:j + 1, cs]
            zm_ref[rs, cs] = (_silu(acc) * scale).astype(BF16)

        def forget_finish(z, col0, rs):
            lb = lb_ref[:, col0:col0 + CB]
            zh_ref[rs, W + col0:W + col0 + CB] = ((1.0 - lb) * jax.nn.sigmoid(-z)).astype(BF16)
            lf_hi, lf_lo = _split_hi_lo(jnp.log(lb + (1.0 - lb) * jax.nn.sigmoid(z)))
            zh_ref[rs, 4 * W + col0:4 * W + col0 + CB] = lf_hi
            zh_ref[rs, 5 * W + col0:5 * W + col0 + CB] = lf_lo

        row_blocks = [slice(r0, r0 + IN_PROJ_ROWS) for r0 in range(0, tm, IN_PROJ_ROWS)]
        for c0 in range(0, W, CB):
            cs = slice(c0, c0 + CB)
            parked = lambda blk: slice(blk * W + c0, blk * W + c0 + CB)
            projections = [(wm_ref, c0, wr_qk, cs), (wm_ref, W + c0, wr_qk, slice(W + c0, W + c0 + CB))]
            projections += [(w_ref, src * W + c0, wr_zz, parked(blk))
                            for blk, (w_ref, src) in enumerate([(wm_ref, 2), (wm_ref, 3), (wh_ref, 0), (wh_ref, 1),
                                                                (wh_ref, 2), (wh_ref, 3)])]
            read = lambda blk, rs: rd_zz[rs, parked(blk)].astype(F32)
            put = lambda ref, col, rs, val: ref.__setitem__((rs, slice(col, col + CB)), val)
            finishes = [
                lambda rs: conv_finish(c0, 1.0, rs),
                lambda rs: conv_finish(W + c0, HEAD_DIM ** -0.5, rs),
                lambda rs: put(zm_ref, 2 * W + c0, rs, rd_zz[rs, parked(0)]),
                lambda rs: put(zm_ref, 3 * W + c0, rs, jax.nn.sigmoid(read(1, rs)).astype(BF16)),
                lambda rs: put(zh_ref, c0, rs, _silu(read(2, rs)).astype(BF16)),
                lambda rs: forget_finish(read(3, rs), c0, rs),
                lambda rs: put(zh_ref, 2 * W + c0, rs, rd_zz[rs, parked(4)]),
                lambda rs: put(zh_ref, 3 * W + c0, rs, _silu(read(5, rs)).astype(BF16)),
            ]
            for (w_ref, wc, dst, dcols), finish in zip(projections, finishes):
                z = _dot(h, w_ref[:, wc:wc + CB])
                if dst is wr_qk:
                    dst[8:8 + tm, dcols] = z
                else:
                    dst[:, dcols] = z.astype(BF16)
                for rs in row_blocks:
                    finish(rs)

        tail[...] = rd_qk[tm:tm + 8, :]

    @pl.when(lax.rem(i, 2) == 0)
    def _():
        body(qk0, zz0, qk1, zz1)

    @pl.when(lax.rem(i, 2) == 1)
    def _():
        body(qk1, zz1, qk0, zz0)


def _in_proj_ring(x2, g_mix, w_m, w_h, w_g, w_gt, b_g, b_gt, conv_w, lb, seq, tm):
    n = x2.shape[0]
    assert seq % tm == 0
    n_tiles = n // tm
    const = lambda i: (0, 0)
    cur = lambda i: (jnp.minimum(i, n_tiles - 1), 0)
    prev = lambda i: (jnp.maximum(i - 1, 0), 0)
    return pl.pallas_call(
        functools.partial(_in_proj_ring_kernel, seq // tm),
        grid=(n_tiles + 1,),
        in_specs=[
            pl.BlockSpec((tm, D_MODEL), cur),
            pl.BlockSpec((1, D_MODEL), const),
            pl.BlockSpec((D_MODEL, 4 * W_MIX), const),
            pl.BlockSpec((D_MODEL, 4 * W_MIX), const),
            pl.BlockSpec((D_MODEL, LANES), const),
            pl.BlockSpec((8, D_MODEL), const),
            pl.BlockSpec((1, LANES), const),
            pl.BlockSpec((8, 1), const),
            pl.BlockSpec((CONV_WIDTH, 2 * W_MIX), const),
            pl.BlockSpec((1, W_MIX), const),
        ],
        out_specs=[
            pl.BlockSpec((tm, 4 * W_MIX), prev),
            pl.BlockSpec((tm, 6 * W_MIX), prev),
            pl.BlockSpec((tm, LANES), cur),
            pl.BlockSpec((8, tm), lambda i: (0, jnp.minimum(i, n_tiles - 1))),
        ],
        out_shape=[
            jax.ShapeDtypeStruct((n, 4 * W_MIX), BF16),
            jax.ShapeDtypeStruct((n, 6 * W_MIX), BF16),
            jax.ShapeDtypeStruct((n, LANES), F32),
            jax.ShapeDtypeStruct((8, n), F32),
        ],
        scratch_shapes=[pltpu.VMEM((tm + 8, 2 * W_MIX), F32)] * 2 + [pltpu.VMEM((tm, 6 * W_MIX), BF16)] * 2
        + [pltpu.VMEM((8, 2 * W_MIX), F32)],
        compiler_params=_params("arbitrary"),
        name="in_proj",
    )(x2, g_mix, w_m, w_h, w_g, w_gt, b_g, b_gt, conv_w, lb)


def _rows_bcast(ref, rows, span, hs):
    return jnp.concatenate([jnp.broadcast_to(ref[r:r + 1, hs], (span, HEAD_DIM)) for r in rows], axis=0)


def _head_cols(base, h):
    return slice(base * W_MIX + h * HEAD_DIM, base * W_MIX + (h + 1) * HEAD_DIM)


def _lane_block(j):
    return slice(j * LANES, (j + 1) * LANES)


def _mixer_kernel(zm_ref, zh_ref, gate_ref, gt0_ref, gt1_ref, gm_ref, gh_ref, sel_ref, y_ref,
                  c_ref, n_ref, m_ref, st_ref, b_scr):
    L = CHUNK

    @pl.when(pl.program_id(1) == 0)
    def _():
        c_ref[...] = jnp.zeros_like(c_ref)
        n_ref[...] = jnp.zeros_like(n_ref)
        m_ref[...] = jnp.zeros_like(m_ref)
        st_ref[...] = jnp.zeros_like(st_ref)

    units = [(bi, h) for bi in range(MIX_BATCH) for h in range(N_HEADS)]
    gt_refs = (gt0_ref, gt1_ref)
    row = lax.broadcasted_iota(I32, (L, L), 0)
    col = lax.broadcasted_iota(I32, (L, L), 1)
    causal = col <= row
    lower, upper = causal.astype(BF16), (row <= col).astype(BF16)

    gate_t, cols, bcum_r = [], [], []
    for bi in range(MIX_BATCH):
        g = gate_ref[bi]
        g_t = gt_refs[bi][...]
        lf_c, lf_r = _split_hi_lo(_log_sigmoid(g)), _split_hi_lo(_log_sigmoid(g_t))
        gate_t.append(g_t)
        bcum_c = _dot(lower, lf_c[0]) + _dot(lower, lf_c[1])
        bcum_r.append(_dot(lf_r[0], upper) + _dot(lf_r[1], upper))
        mixed = _split_hi_lo(jnp.where(col < N_HEADS, g, bcum_c))
        cols.append(_dot(mixed[0], sel_ref[...]) + _dot(mixed[1], sel_ref[...]))
        b_scr[bi] = (_dot(lower, zh_ref[bi, :, 4 * W_MIX:5 * W_MIX])
                     + _dot(lower, zh_ref[bi, :, 5 * W_MIX:6 * W_MIX]))

    qk, qc = {}, {}
    for u in units:
        bi, h = u
        qb = zm_ref[bi, :, _head_cols(0, h)]
        qk[u] = _dot_nt(qb, zm_ref[bi, :, _head_cols(1, h)])
        qc[u] = _dot(qb, c_ref[bi, h].astype(BF16))

    oi, blocks = {}, {}
    for u in units:
        bi, h = u
        hs = _head_cols(0, h)
        b = b_scr[bi, :, hs]
        q = zh_ref[bi, :, hs].astype(F32)
        k = zh_ref[bi, :, _head_cols(1, h)].astype(F32)
        oi[u] = _dot_nt((q * jnp.exp(b)).astype(BF16), st_ref[bi, h].astype(BF16))
        parts = []
        span = L // 2
        while span >= 16:
            mids = range(span, L, 2 * span)
            r = _rows_bcast(b_scr.at[bi], [m for m in mids for _ in (0, 1)], span, hs)
            low = (row & span) != 0
            z = (jnp.where(low, q, k) * jnp.exp(jnp.where(low, b - r, r - b))).astype(BF16)
            keep = ((row & -(2 * span)) == (col & -(2 * span))) & low & ((col & span) == 0)
            parts.append((keep, _dot_nt(z, z)))
            span //= 2
        r = _rows_bcast(b_scr.at[bi], range(0, L, 16), 16, hs)
        p = _dot_nt((q * jnp.exp(b - r)).astype(BF16), (k * jnp.exp(r - b)).astype(BF16))
        parts.append((((row & -16) == (col & -16)) & causal, p))
        blocks[u] = parts

    log_d, inter, row_max, qn = {}, {}, {}, {}
    for u in units:
        bi, h = u
        bc = cols[bi][:, _lane_block(N_HEADS + h)]
        br = bcum_r[bi][N_HEADS + h:N_HEADS + h + 1, :]
        ir = gate_t[bi][h:h + 1, :]
        log_d[u] = jnp.where(causal, bc - br + ir, -jnp.inf)
        inter[u] = bc + m_ref[bi, h:h + 1, :]
        row_max[u] = jnp.max(log_d[u], axis=-1, keepdims=True)
        q = zm_ref[bi, :, _head_cols(0, h)].astype(F32)
        qn[u] = jnp.sum(q * n_ref[bi, h:h + 1, :], axis=-1, keepdims=True)

    sv, row_sum, w_inter, m_t_all = {}, {}, {}, {}
    for u in units:
        bi, h = u
        m_t = jnp.maximum(inter[u], row_max[u])
        s = qk[u] * jnp.exp(log_d[u] - m_t)
        w_inter[u] = jnp.exp(inter[u] - m_t)
        m_t_all[u] = m_t
        row_sum[u] = jnp.sum(s, axis=-1, keepdims=True)
        sv[u] = _dot(s.astype(BF16), zm_ref[bi, :, _head_cols(2, h)])

    av, sk = {}, {}
    for u in units:
        bi, h = u
        hs = _head_cols(0, h)
        a = None
        for keep, p in blocks[u]:
            term = jnp.where(keep, p, 0.0)
            a = term if a is None else a + term
        v = zh_ref[bi, :, _head_cols(2, h)]
        av[u] = _dot(a.astype(BF16), v)
        k = zh_ref[bi, :, _head_cols(1, h)].astype(F32)
        ke = (k * jnp.exp(b_scr[bi, L - 1:L, hs] - b_scr[bi, :, hs])).astype(BF16)
        sk[u] = _dot_tn(v, ke)

    for u in units:
        bi, h = u
        bc = cols[bi][:, _lane_block(N_HEADS + h)]
        ic = cols[bi][:, _lane_block(h)]
        m_prev = m_ref[bi, h:h + 1, :]
        b_last = bc[L - 1:L, :]
        w_log = b_last - bc + ic
        m_new = jnp.maximum(b_last + m_prev, jnp.max(w_log, axis=0, keepdims=True))
        k = zm_ref[bi, :, _head_cols(1, h)].astype(F32)
        kw = k * jnp.exp(w_log - m_new)
        decay = jnp.exp(b_last + m_prev - m_new)
        c_ref[bi, h] = decay * c_ref[bi, h] + _dot_tn(kw.astype(BF16), zm_ref[bi, :, _head_cols(2, h)])
        n_ref[bi, h:h + 1, :] = decay * n_ref[bi, h:h + 1, :] + jnp.sum(kw, axis=0, keepdims=True)
        m_ref[bi, h:h + 1, :] = m_new

    hh_all, o_all, hh_ms, o_ms = {}, {}, {}, {}
    for u in units:
        bi, h = u
        hs = _head_cols(0, h)
        den = row_sum[u] + w_inter[u] * qn[u]
        num = sv[u] + w_inter[u] * qc[u]
        hh = num * (1.0 / jnp.maximum(jnp.abs(den), jnp.exp(-m_t_all[u])))
        hh = hh * zm_ref[bi, :, _head_cols(3, h)].astype(F32)
        hh_all[u] = hh
        hh_ms[u] = jnp.mean(hh * hh, axis=-1, keepdims=True)

        st_ref[bi, h] = jnp.exp(b_scr[bi, L - 1:L, hs]) * st_ref[bi, h] + sk[u]
        o = oi[u] + av[u]
        o_all[u] = o
        o_ms[u] = jnp.mean(o * o, axis=-1, keepdims=True)

    for u in units:
        bi, h = u
        hs = _head_cols(0, h)
        y_ref[bi, :, hs] = (hh_all[u] * lax.rsqrt(hh_ms[u] + EPS) * gm_ref[:, hs]).astype(BF16)
        o = o_all[u] * lax.rsqrt(o_ms[u] + EPS) * gh_ref[...]
        o = o * zh_ref[bi, :, _head_cols(3, h)].astype(F32)
        y_ref[bi, :, _head_cols(1, h)] = o.astype(BF16)


def _mixer(zm, zh, gate, gate_t, g_mlstm, g_hgrn, batch, n_chunks):
    n = zm.shape[0]
    seq = n // batch
    assert MIX_BATCH == 2 and batch % MIX_BATCH == 0
    blk = lambda b, c: (b, c, 0)
    const = lambda b, c: (0, 0)
    sel = (jnp.arange(LANES)[:, None] == jnp.arange(2 * N_HEADS * LANES)[None, :] // LANES).astype(BF16)
    return pl.pallas_call(
        _mixer_kernel,
        grid=(batch // MIX_BATCH, n_chunks),
        in_specs=[
            pl.BlockSpec((MIX_BATCH, CHUNK, 4 * W_MIX), blk),
            pl.BlockSpec((MIX_BATCH, CHUNK, 6 * W_MIX), blk),
            pl.BlockSpec((MIX_BATCH, CHUNK, LANES), blk),
            pl.BlockSpec((8, CHUNK), lambda b, c: (0, (MIX_BATCH * b) * n_chunks + c)),
            pl.BlockSpec((8, CHUNK), lambda b, c: (0, (MIX_BATCH * b + 1) * n_chunks + c)),
            pl.BlockSpec((1, W_MIX), const),
            pl.BlockSpec((1, HEAD_DIM), const),
            pl.BlockSpec((LANES, 2 * N_HEADS * LANES), const),
        ],
        out_specs=pl.BlockSpec((MIX_BATCH, CHUNK, 2 * W_MIX), blk),
        out_shape=jax.ShapeDtypeStruct((batch, seq, 2 * W_MIX), BF16),
        scratch_shapes=[
            pltpu.VMEM((MIX_BATCH, N_HEADS, HEAD_DIM, HEAD_DIM), F32),
            pltpu.VMEM((MIX_BATCH, 8, HEAD_DIM), F32),
            pltpu.VMEM((MIX_BATCH, 8, LANES), F32),
            pltpu.VMEM((MIX_BATCH, N_HEADS, HEAD_DIM, HEAD_DIM), F32),
            pltpu.VMEM((MIX_BATCH, CHUNK, W_MIX), F32),
        ],
        compiler_params=_params("parallel", "arbitrary"),
        name="mixer",
    )(zm.reshape(batch, seq, 4 * W_MIX), zh.reshape(batch, seq, 6 * W_MIX), gate.reshape(batch, seq, LANES),
      gate_t, gate_t, g_mlstm, g_hgrn, sel).reshape(n, 2 * W_MIX)


def _pack_bf16_pair(lo, hi):
    lo_bits = pltpu.bitcast(lo.astype(BF16).astype(F32), U32)
    hi_bits = pltpu.bitcast(hi.astype(BF16).astype(F32), U32)
    return (hi_bits & jnp.uint32(0xFFFF0000)) | (lo_bits >> 16)


def _unpack_bf16_pair(w):
    lo = pltpu.bitcast(w << 16, F32).astype(BF16)
    hi = pltpu.bitcast(w & jnp.uint32(0xFFFF0000), F32).astype(BF16)
    return lo, hi


def _post_mix_kernel(x_ref, y_ref, wo_ref, g_ref, wrh_ref, wrl_ref, br_ref,
                     x1_ref, hn_ref, route_t_ref, comb_ref, count_ref, run_ref):
    tm = x_ref.shape[0]
    n_r = wrh_ref.shape[0]

    @pl.when(pl.program_id(0) == 0)
    def _():
        run_ref[...] = jnp.zeros_like(run_ref)

    x1 = x_ref[...] + _dot(y_ref[...], wo_ref[...])
    x1_ref[...] = x1
    hn = _rms(x1, g_ref[...])
    half = D_MODEL // 2
    packed = _pack_bf16_pair(hn[:, :half], hn[:, half:])
    for j in range(PACK_ROWS):
        hn_ref[pl.ds(j, tm, stride=PACK_ROWS), :] = packed[:, j * LANES:(j + 1) * LANES]

    hn_hi = hn.astype(BF16)
    hn_lo = (hn - hn_hi.astype(F32)).astype(BF16)
    logits = (_dot_nt(wrh_ref[...], hn_hi) + (_dot_nt(wrl_ref[...], hn_hi) + _dot_nt(wrh_ref[...], hn_lo))
              + br_ref[...])
    rix = lax.broadcasted_iota(I32, (n_r, tm), 0)
    neg = -jnp.inf
    g_l = jnp.where(rix < N_GROUPS, logits, neg)
    g_max = jnp.max(g_l, axis=0, keepdims=True)
    g_sel = jnp.min(jnp.where(g_l == g_max, rix, n_r), axis=0, keepdims=True)
    g_val = 1.0 / jnp.sum(jnp.exp(g_l - g_max), axis=0, keepdims=True)

    e_row = rix - N_GROUPS
    in_group = (e_row >= g_sel * EXPERTS_PER_GROUP) & (e_row < (g_sel + 1) * EXPERTS_PER_GROUP)
    e_l = jnp.where(in_group, logits, neg)
    v1 = jnp.max(e_l, axis=0, keepdims=True)
    i1 = jnp.min(jnp.where(e_l == v1, rix, n_r), axis=0, keepdims=True)
    e_l2 = jnp.where(rix == i1, neg, e_l)
    v2 = jnp.max(e_l2, axis=0, keepdims=True)
    i2 = jnp.min(jnp.where(e_l2 == v2, rix, n_r), axis=0, keepdims=True)
    t = jnp.exp(v2 - v1)
    c1 = g_val / (1.0 + t)
    c2 = g_val * t / (1.0 + t)

    hot1 = rix == i1
    hot2 = rix == i2
    hot = (hot1 | hot2).astype(F32)
    r_i = lax.broadcasted_iota(I32, (tm, tm), 0)
    c_i = lax.broadcasted_iota(I32, (tm, tm), 1)
    before = _dot(hot.astype(BF16), (r_i < c_i).astype(BF16)) + run_ref[:, 0:1]
    rank1 = jnp.sum(jnp.where(hot1, before, 0.0), axis=0, keepdims=True)
    rank2 = jnp.sum(jnp.where(hot2, before, 0.0), axis=0, keepdims=True)
    run_ref[...] = run_ref[...] + jnp.sum(hot, axis=1, keepdims=True)
    count_ref[...] = run_ref[...]

    r8 = lax.broadcasted_iota(I32, (8, tm), 0)
    out = jnp.where(r8 == 0, (i1 - N_GROUPS).astype(F32), 0.0)
    out = jnp.where(r8 == 1, (i2 - N_GROUPS).astype(F32), out)
    out = jnp.where(r8 == 2, rank1, out)
    out = jnp.where(r8 == 3, rank2, out)
    route_t_ref[...] = out

    r128 = lax.broadcasted_iota(I32, (LANES, tm), 0)
    slab = jnp.where(r128 == 0, c1, jnp.where(r128 == 1, c2, 0.0))
    for c in range(tm // LANES):
        comb_ref[c * LANES:(c + 1) * LANES, :] = slab[:, c * LANES:(c + 1) * LANES].T


def _post_mix(x2, y, w_out, g_ffn, wr_hi, wr_lo, b_r, tm):
    n = x2.shape[0]
    n_r = wr_hi.shape[0]
    const = lambda i: (0, 0)
    rows = lambda i: (i, 0)
    return pl.pallas_call(
        _post_mix_kernel,
        grid=(n // tm,),
        in_specs=[
            pl.BlockSpec((tm, D_MODEL), rows),
            pl.BlockSpec((tm, 2 * W_MIX), rows),
            pl.BlockSpec((2 * W_MIX, D_MODEL), const),
            pl.BlockSpec((1, D_MODEL), const),
            pl.BlockSpec((n_r, D_MODEL), const),
            pl.BlockSpec((n_r, D_MODEL), const),
            pl.BlockSpec((n_r, 1), const),
        ],
        out_specs=[
            pl.BlockSpec((tm, D_MODEL), rows),
            pl.BlockSpec((tm * PACK_ROWS, LANES), rows),
            pl.BlockSpec((8, tm), lambda i: (0, i)),
            pl.BlockSpec((tm, LANES), rows),
            pl.BlockSpec((n_r, LANES), const),
        ],
        out_shape=[
            jax.ShapeDtypeStruct((n, D_MODEL), F32),
            jax.ShapeDtypeStruct((n * PACK_ROWS, LANES), U32),
            jax.ShapeDtypeStruct((8, n), F32),
            jax.ShapeDtypeStruct((n, LANES), F32),
            jax.ShapeDtypeStruct((n_r, LANES), F32),
        ],
        scratch_shapes=[pltpu.VMEM((n_r, LANES), F32)],
        compiler_params=_params("arbitrary"),
        name="post_mix",
    )(x2, y, w_out, g_ffn, wr_hi, wr_lo, b_r)


def _dispatch_kernel(zero_ref, d0_ref, d1_ref, hn_ref, wg_ref, wu_ref, wd_ref,
                     xb_ref, wgb_ref, wub_ref, wdb_ref, zbuf, sem):
    ts = hn_ref.shape[0] // PACK_ROWS
    block_rows = EXPERT_BLOCK * PACK_ROWS

    @pl.when(pl.program_id(0) == 0)
    def _():
        zbuf[...] = jnp.zeros_like(zbuf)

        def zero_copy(j):
            row0 = pl.multiple_of(jnp.maximum(zero_ref[j], 0) * block_rows, block_rows)
            return pltpu.make_async_copy(zbuf, xb_ref.at[pl.ds(row0, block_rows), :], sem.at[2])

        def start_zero(j, carry):
            @pl.when(zero_ref[j] >= 0)
            def _():
                zero_copy(j).start()
            return carry

        def wait_zero(j, carry):
            @pl.when(zero_ref[j] >= 0)
            def _():
                zero_copy(j).wait()
            return carry

        lax.fori_loop(0, zero_ref.shape[0], start_zero, 0)
        lax.fori_loop(0, zero_ref.shape[0], wait_zero, 0)

    def start(t, carry):
        src = pl.multiple_of(t * PACK_ROWS, PACK_ROWS)
        for k, d_ref in enumerate((d0_ref, d1_ref)):
            dst = pl.multiple_of(d_ref[0, 0, t] * PACK_ROWS, PACK_ROWS)
            pltpu.make_async_copy(hn_ref.at[pl.ds(src, PACK_ROWS), :], xb_ref.at[pl.ds(dst, PACK_ROWS), :],
                                  sem.at[k]).start(priority=k)
        return carry

    lax.fori_loop(0, ts, start, 0, unroll=DMA_UNROLL)
    wgb_ref[...] = wg_ref[...].astype(BF16)
    wub_ref[...] = wu_ref[...].astype(BF16)
    wdb_ref[...] = wd_ref[...].astype(BF16)

    for k in range(2):
        pltpu.make_async_copy(hn_ref, xb_ref.at[pl.ds(0, ts * PACK_ROWS), :], sem.at[k]).wait()


def _dispatch(zero_blocks, dest0, dest1, hn, w_gate, w_up, w_down, n_rows, ts):
    n = hn.shape[0] // PACK_ROWS
    steps = n // ts
    if steps >= N_EXPERTS:
        parts = steps // N_EXPERTS
        assert steps == parts * N_EXPERTS
        w_spec = lambda rows, cols: pl.BlockSpec((None, rows // parts, cols), lambda i, z: (i // parts, i % parts, 0))
    else:
        per_step = N_EXPERTS // steps
        assert N_EXPERTS == per_step * steps
        w_spec = lambda rows, cols: pl.BlockSpec((per_step, rows, cols), lambda i, z: (i, 0, 0))
    w_specs = [w_spec(D_MODEL, D_EXPERT), w_spec(D_MODEL, D_EXPERT), w_spec(D_EXPERT, D_MODEL)]
    grid_spec = pltpu.PrefetchScalarGridSpec(
        num_scalar_prefetch=1,
        grid=(steps,),
        in_specs=[
            pl.BlockSpec((1, 1, ts), lambda i, z: (i, 0, 0), memory_space=pltpu.SMEM),
            pl.BlockSpec((1, 1, ts), lambda i, z: (i, 0, 0), memory_space=pltpu.SMEM),
            pl.BlockSpec((ts * PACK_ROWS, LANES), lambda i, z: (i, 0)),
        ] + w_specs,
        out_specs=[pl.BlockSpec(memory_space=pl.ANY)] + w_specs,
        scratch_shapes=[pltpu.VMEM((EXPERT_BLOCK * PACK_ROWS, LANES), U32), pltpu.SemaphoreType.DMA((3,))],
    )
    return pl.pallas_call(
        _dispatch_kernel,
        grid_spec=grid_spec,
        out_shape=[jax.ShapeDtypeStruct((n_rows * PACK_ROWS, LANES), U32)]
        + [jax.ShapeDtypeStruct(w.shape, BF16) for w in (w_gate, w_up, w_down)],
        compiler_params=_params("arbitrary"),
        name="dispatch",
    )(zero_blocks, dest0, dest1, hn, w_gate, w_up, w_down)


def _experts_kernel(be_ref, bv_ref, nu_ref, xb_ref, wg_ref, wu_ref, wd_ref, yb_ref):
    valid = bv_ref[pl.program_id(0)]
    half = EXPERT_BLOCK // 2

    def mlp(rows):
        pairs = [_unpack_bf16_pair(xb_ref[pl.ds(j, rows, stride=PACK_ROWS), :]) for j in range(PACK_ROWS)]
        x = jnp.concatenate([lo for lo, _ in pairs] + [hi for _, hi in pairs], axis=1)
        a = (_silu(_dot(x, wg_ref[...])) * _dot(x, wu_ref[...])).astype(BF16)
        y = _dot(a, wd_ref[...])
        half_d = D_MODEL // 2
        packed = _pack_bf16_pair(y[:, :half_d], y[:, half_d:])
        for j in range(PACK_ROWS):
            yb_ref[pl.ds(j, rows, stride=PACK_ROWS), :] = packed[:, j * LANES:(j + 1) * LANES]

    @pl.when(valid > half)
    def _():
        mlp(EXPERT_BLOCK)

    @pl.when((valid > 0) & (valid <= half))
    def _():
        mlp(half)
        yb_ref[half * PACK_ROWS:, :] = jnp.zeros((half * PACK_ROWS, LANES), U32)

    @pl.when(valid == 0)
    def _():
        yb_ref[...] = jnp.zeros_like(yb_ref)


def _experts(block_e, block_valid, n_used, xb, w_gate, w_up, w_down):
    n_rows = xb.shape[0] // PACK_ROWS
    n_blocks = n_rows // EXPERT_BLOCK
    xrow = lambda i, be, bv, nu: (jnp.maximum(jnp.minimum(i, nu[0] - 1), 0), 0)
    wsel = lambda i, be, bv, nu: (be[i], 0, 0)
    grid_spec = pltpu.PrefetchScalarGridSpec(
        num_scalar_prefetch=3,
        grid=(n_blocks,),
        in_specs=[
            pl.BlockSpec((EXPERT_BLOCK * PACK_ROWS, LANES), xrow),
            pl.BlockSpec((None, D_MODEL, D_EXPERT), wsel),
            pl.BlockSpec((None, D_MODEL, D_EXPERT), wsel),
            pl.BlockSpec((None, D_EXPERT, D_MODEL), wsel),
        ],
        out_specs=pl.BlockSpec((EXPERT_BLOCK * PACK_ROWS, LANES), lambda i, be, bv, nu: (i, 0)),
    )
    return pl.pallas_call(
        _experts_kernel,
        grid_spec=grid_spec,
        out_shape=jax.ShapeDtypeStruct((n_rows * PACK_ROWS, LANES), U32),
        compiler_params=_params("arbitrary"),
        name="experts",
    )(block_e, block_valid, n_used, xb, w_gate, w_up, w_down)


def _combine_kernel(d0_ref, d1_ref, d0a_ref, d1a_ref, d0b_ref, d1b_ref, x1_ref, p_ref, comb_ref, gpl_ref, wplg_ref,
                    wplp_ref, gfin_ref, yb_ref, out_ref, gbuf, x3_ref, sem):
    tf = x1_ref.shape[0]
    i = pl.program_id(0)
    n_steps = pl.num_programs(0)
    slot = lax.rem(i, GATHER_BUFS)
    slot_a = lax.rem(i + 1, GATHER_BUFS)
    slot_b = lax.rem(i + 2, GATHER_BUFS)

    def start_gather(d_refs, s, t):
        row = pl.multiple_of(t * PACK_ROWS, PACK_ROWS)
        for k, d_ref in enumerate(d_refs):
            src = pl.multiple_of(d_ref[0, 0, t] * PACK_ROWS, PACK_ROWS)
            pltpu.make_async_copy(yb_ref.at[pl.ds(src, PACK_ROWS), :], gbuf.at[s, k, pl.ds(row, PACK_ROWS), :],
                                  sem.at[s, k]).start(priority=k)

    def wait_gathers(s):
        for k in range(2):
            pltpu.make_async_copy(yb_ref.at[pl.ds(0, tf * PACK_ROWS), :], gbuf.at[s, k], sem.at[s, k]).wait()

    @pl.when(i == 0)
    def _():
        lax.fori_loop(0, tf, lambda t, c: (start_gather((d0_ref, d1_ref), 0, t), c)[1], 0, unroll=DMA_UNROLL)
        lax.fori_loop(0, tf, lambda t, c: (start_gather((d0a_ref, d1a_ref), 1, t), c)[1], 0, unroll=DMA_UNROLL)

    wait_gathers(slot)

    comb = comb_ref[...]
    def gathered(k):
        words = [gbuf[slot, k, pl.ds(j, tf, stride=PACK_ROWS), :] for j in range(PACK_ROWS)]
        return jnp.concatenate([pltpu.bitcast(w << 16, F32) for w in words]
                               + [pltpu.bitcast(w & jnp.uint32(0xFFFF0000), F32) for w in words], axis=1)

    g1, g2 = gathered(0), gathered(1)
    y = comb[:, 0:1] * g1 + comb[:, 1:2] * g2
    x2 = x1_ref[...] + y
    hb = _rms(x2, gpl_ref[...]).astype(BF16)
    pb = p_ref[...].astype(BF16)
    n_col = D_MODEL // COMBINE_COLS
    for c in range(n_col):
        for t in range(c * tf // n_col, (c + 1) * tf // n_col):
            start_gather((d0b_ref, d1b_ref), slot_b, t)
        cs = slice(c * COMBINE_COLS, (c + 1) * COMBINE_COLS)
        gate = jax.nn.sigmoid(_dot(hb, wplg_ref[:, cs]))
        x3_ref[:, cs] = x2[:, cs] + gate * _dot(pb, wplp_ref[:, cs])
    x3 = x3_ref[...]
    out_ref[...] = _rms(x3, gfin_ref[...])

    @pl.when(i == n_steps - 1)
    def _():
        wait_gathers(slot_a)
        wait_gathers(slot_b)


def _combine(dest0, dest1, x1, p2, comb, g_pl, w_plg, w_plp, g_final, yb, tf):
    n = x1.shape[0]
    n_steps = n // tf
    assert n_steps >= GATHER_BUFS - 1
    const = lambda i: (0, 0)
    rows = lambda i: (i, 0)
    ahead = lambda a: (lambda i: (jnp.minimum(i + a, n_steps - 1), 0, 0))
    dest_specs = [pl.BlockSpec((1, 1, tf), ahead(a), memory_space=pltpu.SMEM) for a in range(GATHER_BUFS)
                  for _ in range(2)]
    return pl.pallas_call(
        _combine_kernel,
        grid=(n_steps,),
        in_specs=dest_specs + [
            pl.BlockSpec((tf, D_MODEL), rows),
            pl.BlockSpec((tf, PLE_DIM), rows),
            pl.BlockSpec((tf, LANES), rows),
            pl.BlockSpec((1, D_MODEL), const),
            pl.BlockSpec((D_MODEL, D_MODEL), const),
            pl.BlockSpec((PLE_DIM, D_MODEL), const),
            pl.BlockSpec((1, D_MODEL), const),
            pl.BlockSpec(memory_space=pl.ANY),
        ],
        out_specs=pl.BlockSpec((tf, D_MODEL), rows),
        out_shape=jax.ShapeDtypeStruct((n, D_MODEL), F32),
        scratch_shapes=[pltpu.VMEM((GATHER_BUFS, 2, tf * PACK_ROWS, LANES), U32), pltpu.VMEM((tf, D_MODEL), F32),
                        pltpu.SemaphoreType.DMA((GATHER_BUFS, 2))],
        compiler_params=_params("arbitrary"),
        name="combine",
    )(dest0, dest1, dest0, dest1, dest0, dest1, x1, p2, comb, g_pl, w_plg, w_plp, g_final, yb)


def _layer(x2, p2, batch, seq, g_mix, w_in, b_mgate, conv_qk, g_mlstm, lb, g_hgrn, w_out, g_ffn,
           w_rg, b_rg, w_re, b_re, w_e_gate, w_e_up, w_e_down, g_pl, w_pl_gate, w_pl_proj, g_out):
    n = x2.shape[0]
    n_chunks = seq // CHUNK
    m_cols = 4 * W_MIX
    n_gate = 2 * N_HEADS

    w_in_b = w_in.astype(BF16)
    w_m = w_in_b[:, :m_cols]
    w_gcols = w_in_b[:, m_cols:m_cols + n_gate]
    w_h = w_in_b[:, m_cols + n_gate:]
    w_g = jnp.pad(w_gcols, ((0, 0), (0, LANES - n_gate)))
    b_g = jnp.pad(b_mgate.astype(F32), (0, LANES - n_gate))[None, :]
    zm, zh, gate, gate_t = _in_proj(x2, g_mix[None, :], w_m, w_h, w_g, w_gcols.T, b_g,
                                    b_mgate.astype(F32)[:, None], conv_qk, lb[None, :], seq, tm=min(1024, seq))

    y = _mixer(zm, zh, gate, gate_t, g_mlstm[None, :], g_hgrn[None, :], batch, n_chunks)

    n_logit = N_GROUPS + N_EXPERTS
    w_rt = jnp.pad(jnp.concatenate([w_rg, w_re], axis=1).T, ((0, ROUTER_ROWS - n_logit), (0, 0)))
    wr_hi = w_rt.astype(BF16)
    wr_lo = (w_rt - wr_hi.astype(F32)).astype(BF16)
    b_r = jnp.pad(jnp.concatenate([b_rg, b_re]), (0, ROUTER_ROWS - n_logit))[:, None]
    x1, hn, route_t, comb, counts = _post_mix(x2, y, w_out.astype(BF16), g_ffn[None, :], wr_hi, wr_lo, b_r,
                                              tm=min(512, n))

    counts = counts[N_GROUPS:n_logit, 0].astype(I32)
    padded = (counts + EXPERT_BLOCK - 1) // EXPERT_BLOCK * EXPERT_BLOCK
    pend = jnp.cumsum(padded)
    pstart = pend - padded
    n_blocks = (2 * n) // EXPERT_BLOCK + N_EXPERTS
    n_rows = n_blocks * EXPERT_BLOCK
    n_used = (pend[-1] // EXPERT_BLOCK).astype(I32)
    block_ids = jnp.arange(n_blocks, dtype=I32)
    block_row = jnp.minimum(block_ids, n_used - 1) * EXPERT_BLOCK
    block_e = jnp.sum((pend[None, :] <= block_row[:, None]).astype(I32), axis=1)
    token_end = jnp.sum(jnp.where(block_e[:, None] == jnp.arange(N_EXPERTS, dtype=I32), pstart + counts, 0), axis=1)
    block_valid = jnp.where(block_ids < n_used, jnp.clip(token_end - block_row, 0, EXPERT_BLOCK), 0).astype(I32)
    expert_id = route_t[0:2].astype(I32)
    hot = expert_id[:, :, None] == jnp.arange(N_EXPERTS, dtype=I32)
    dest = jnp.sum(jnp.where(hot, pstart, 0), axis=-1) + route_t[2:4].astype(I32)

    ts = min(512, n)
    last_block = jnp.where(counts % EXPERT_BLOCK != 0, pend // EXPERT_BLOCK - 1, -1)
    tail_block = n_used + jnp.arange(N_EXPERTS, dtype=I32)
    zero_blocks = jnp.concatenate([last_block, jnp.where(tail_block < n_blocks, tail_block, -1)]).astype(I32)
    xb, wg_b, wu_b, wd_b = _dispatch(zero_blocks, dest[0].reshape(n // ts, 1, ts), dest[1].reshape(n // ts, 1, ts),
                                     hn, w_e_gate, w_e_up, w_e_down, n_rows, ts)
    yb = _experts(block_e, block_valid, n_used[None], xb, wg_b, wu_b, wd_b)
    tf = min(256, n)
    return _combine(dest[0].reshape(n // tf, 1, tf), dest[1].reshape(n // tf, 1, tf), x1, p2, comb, g_pl[None, :],
                    w_pl_gate.astype(BF16), w_pl_proj.astype(BF16), g_out[None, :], yb, tf)


def kernel(x, p, g_mix, w_in, b_mgate, conv_qk, g_mlstm, hg_lb, g_hgrn, w_out, g_ffn, w_rg, b_rg, w_re, b_re,
           w_e_gate, w_e_up, w_e_down, g_pl, w_pl_gate, w_pl_proj, g_final):
    batch, seq, d = x.shape
    depth = p.shape[0]
    assert depth == 1, "the fused final norm assumes a single layer"
    lower_bounds = jnp.cumsum(jax.nn.softmax(hg_lb.astype(F32), axis=0), axis=0)
    i = 0
    out = _layer(x.reshape(batch * seq, d), p[i].reshape(batch * seq, PLE_DIM), batch, seq,
                 g_mix[i], w_in[i], b_mgate[i], conv_qk[i], g_mlstm[i], lower_bounds[i], g_hgrn[i], w_out[i],
                 g_ffn[i], w_rg[i], b_rg[i], w_re[i], b_re[i], w_e_gate[i], w_e_up[i], w_e_down[i],
                 g_pl[i], w_pl_gate[i], w_pl_proj[i], g_final)
    return out.reshape(batch, seq, d)
```

```python
import functools

import jax
import jax.numpy as jnp
from jax import lax
from jax.experimental import pallas as pl
from jax.experimental.pallas import tpu as pltpu

F32 = jnp.float32
BF16 = jnp.bfloat16
I32 = jnp.int32
U32 = jnp.uint32
EPS = 1e-6

LANES = 128
D_MODEL = 1024
W_MIX = 512
N_HEADS = 4
HEAD_DIM = 128
N_GROUPS = 4
EXPERTS_PER_GROUP = 8
N_EXPERTS = N_GROUPS * EXPERTS_PER_GROUP
D_EXPERT = 512
PLE_DIM = 256
CONV_WIDTH = 4
CHUNK = 128
EXPERT_BLOCK = 512
PACK_ROWS = D_MODEL // 2 // LANES
ROUTER_ROWS = 48
DMA_UNROLL = 4
COMBINE_COLS = 256
MIX_BATCH = 4
IN_PROJ_COLS = 256
IN_PROJ_ROWS = 64
IN_PROJ_AHEAD = 1
GATHER_BUFS = 3
VMEM_LIMIT = 56 * 1024 * 1024


def _dot(a, b):
    return jnp.dot(a, b, preferred_element_type=F32)


def _dot_nt(a, b):
    return lax.dot_general(a, b, (((1,), (1,)), ((), ())), preferred_element_type=F32)


def _dot_tn(a, b):
    return lax.dot_general(a, b, (((0,), (0,)), ((), ())), preferred_element_type=F32)


def _rms(u, g):
    return u * lax.rsqrt(jnp.mean(u * u, axis=-1, keepdims=True) + EPS) * g


def _silu(u):
    return u * jax.nn.sigmoid(u)


def _log_sigmoid(u):
    return jnp.minimum(u, 0.0) - jnp.log1p(jnp.exp(-jnp.abs(u)))


def _split_hi_lo(u):
    hi = u.astype(BF16)
    return hi, (u - hi.astype(F32)).astype(BF16)


def _params(*sem):
    return pltpu.CompilerParams(dimension_semantics=sem, vmem_limit_bytes=VMEM_LIMIT)


def _in_proj_kernel(tiles_per_seq, x_ref, g_ref, wm_ref, wh_ref, wg_ref, wgt_ref, bg_ref, bgt_ref, conv_ref, lb_ref,
                    zm_ref, zh_ref, gate_ref, gate_t_ref, cbuf):
    tm = x_ref.shape[0]
    W = W_MIX
    CB = IN_PROJ_COLS

    @pl.when(pl.program_id(0) % tiles_per_seq == 0)
    def _():
        cbuf[0:8, :] = jnp.zeros((8, 2 * W), F32)

    h = _rms(x_ref[...], g_ref[...]).astype(BF16)

    def conv_finish(col0, scale):
        def finish(z):
            cs = slice(col0, col0 + CB)
            cbuf[8:8 + tm, cs] = z
            acc = z * conv_ref[CONV_WIDTH - 1:CONV_WIDTH, cs]
            for j in range(CONV_WIDTH - 1):
                acc = acc + cbuf[5 + j:5 + j + tm, cs] * conv_ref[j:j + 1, cs]
            cbuf[0:8, cs] = cbuf[tm:tm + 8, cs]
            zm_ref[:, cs] = (_silu(acc) * scale).astype(BF16)
        return finish

    def store(ref, col0, fn):
        def finish(z):
            ref[:, col0:col0 + CB] = fn(z).astype(BF16)
        return finish

    def forget_finish(col0):
        def finish(z):
            lb = lb_ref[:, col0:col0 + CB]
            zh_ref[:, W + col0:W + col0 + CB] = ((1.0 - lb) * jax.nn.sigmoid(-z)).astype(BF16)
            lf_hi, lf_lo = _split_hi_lo(jnp.log(lb + (1.0 - lb) * jax.nn.sigmoid(z)))
            zh_ref[:, 4 * W + col0:4 * W + col0 + CB] = lf_hi
            zh_ref[:, 5 * W + col0:5 * W + col0 + CB] = lf_lo
        return finish

    ident = lambda z: z
    plan = []
    for half in range(W // CB):
        c0 = half * CB
        plan += [
            (wm_ref, c0, conv_finish(c0, 1.0)),
            (wh_ref, 2 * W + c0, store(zh_ref, 2 * W + c0, ident)),
            (wm_ref, W + c0, conv_finish(W + c0, HEAD_DIM ** -0.5)),
            (wm_ref, 2 * W + c0, store(zm_ref, 2 * W + c0, ident)),
            (wh_ref, W + c0, forget_finish(c0)),
            (wm_ref, 3 * W + c0, store(zm_ref, 3 * W + c0, jax.nn.sigmoid)),
            (wh_ref, c0, store(zh_ref, c0, _silu)),
            (wh_ref, 3 * W + c0, store(zh_ref, 3 * W + c0, _silu)),
        ]

    project = lambda i: _dot(h, plan[i][0][:, plan[i][1]:plan[i][1] + CB])
    zs = [project(i) for i in range(IN_PROJ_AHEAD)]
    for i in range(len(plan)):
        if i + IN_PROJ_AHEAD < len(plan):
            zs.append(project(i + IN_PROJ_AHEAD))
        plan[i][2](zs[i])

    gate_ref[...] = _dot(h, wg_ref[...]) + bg_ref[...]
    gate_t_ref[...] = _dot_nt(wgt_ref[...], h) + bgt_ref[...]


def _in_proj(x2, g_mix, w_m, w_h, w_g, w_gt, b_g, b_gt, conv_w, lb, seq, tm):
    n = x2.shape[0]
    assert seq % tm == 0
    const = lambda i: (0, 0)
    rows = lambda i: (i, 0)
    return pl.pallas_call(
        functools.partial(_in_proj_kernel, seq // tm),
        grid=(n // tm,),
        in_specs=[
            pl.BlockSpec((tm, D_MODEL), rows),
            pl.BlockSpec((1, D_MODEL), const),
            pl.BlockSpec((D_MODEL, 4 * W_MIX), const),
            pl.BlockSpec((D_MODEL, 4 * W_MIX), const),
            pl.BlockSpec((D_MODEL, LANES), const),
            pl.BlockSpec((8, D_MODEL), const),
            pl.BlockSpec((1, LANES), const),
            pl.BlockSpec((8, 1), const),
            pl.BlockSpec((CONV_WIDTH, 2 * W_MIX), const),
            pl.BlockSpec((1, W_MIX), const),
        ],
        out_specs=[
            pl.BlockSpec((tm, 4 * W_MIX), rows),
            pl.BlockSpec((tm, 6 * W_MIX), rows),
            pl.BlockSpec((tm, LANES), rows),
            pl.BlockSpec((8, tm), lambda i: (0, i)),
        ],
        out_shape=[
            jax.ShapeDtypeStruct((n, 4 * W_MIX), BF16),
            jax.ShapeDtypeStruct((n, 6 * W_MIX), BF16),
            jax.ShapeDtypeStruct((n, LANES), F32),
            jax.ShapeDtypeStruct((8, n), F32),
        ],
        scratch_shapes=[pltpu.VMEM((tm + 8, 2 * W_MIX), F32)],
        compiler_params=_params("arbitrary"),
        name="in_proj",
    )(x2, g_mix, w_m, w_h, w_g, w_gt, b_g, b_gt, conv_w, lb)


def _in_proj_ring_kernel(tiles_per_seq, x_ref, g_ref, wm_ref, wh_ref, wg_ref, wgt_ref, bg_ref, bgt_ref, conv_ref,
                         lb_ref, zm_ref, zh_ref, gate_ref, gate_t_ref, qk0, qk1, zz0, zz1, tail):
    tm = x_ref.shape[0]
    W = W_MIX
    CB = IN_PROJ_COLS
    i = pl.program_id(0)

    @pl.when(i == 0)
    def _():
        for ref in (qk0, qk1, zz0, zz1, tail):
            ref[...] = jnp.zeros_like(ref)

    h = _rms(x_ref[...], g_ref[...]).astype(BF16)
    gate_ref[...] = _dot(h, wg_ref[...]) + bg_ref[...]
    gate_t_ref[...] = _dot_nt(wgt_ref[...], h) + bgt_ref[...]
    seq_start = lax.rem(i - 1, tiles_per_seq) == 0

    def body(wr_qk, wr_zz, rd_qk, rd_zz):
        rd_qk[0:8, :] = jnp.where(seq_start, 0.0, tail[...])

        def conv_finish(col0, scale, rs):
            cs = slice(col0, col0 + CB)
            acc = rd_qk[8 + rs.start:8 + rs.stop, cs] * conv_ref[CONV_WIDTH - 1:CONV_WIDTH, cs]
            for j in range(CONV_WIDTH - 1):
                acc = acc + rd_qk[5 + j + rs.start:5 + j + rs.stop, cs] * conv_ref[j:j + 1, cs]
            zm_ref[rs, cs] = (_silu(acc) * scale).astype(BF16)

        def forget_finish(z, col0, rs):
            lb = lb_ref[:, col0:col0 + CB]
            zh_ref[rs, W + col0:W + col0 + CB] = ((1.0 - lb) * jax.nn.sigmoid(-z)).astype(BF16)
            lf_hi, lf_lo = _split_hi_lo(jnp.log(lb + (1.0 - lb) * jax.nn.sigmoid(z)))
            zh_ref[rs, 4 * W + col0:4 * W + col0 + CB] = lf_hi
            zh_ref[rs, 5 * W + col0:5 * W + col0 + CB] = lf_lo

        row_blocks = [slice(r0, r0 + IN_PROJ_ROWS) for r0 in range(0, tm, IN_PROJ_ROWS)]
        for c0 in range(0, W, CB):
            cs = slice(c0, c0 + CB)
            parked = lambda blk: slice(blk * W + c0, blk * W + c0 + CB)
            projections = [(wm_ref, c0, wr_qk, cs), (wm_ref, W + c0, wr_qk, slice(W + c0, W + c0 + CB))]
            projections += [(w_ref, src * W + c0, wr_zz, parked(blk))
                            for blk, (w_ref, src) in enumerate([(wm_ref, 2), (wm_ref, 3), (wh_ref, 0), (wh_ref, 1),
                                                                (wh_ref, 2), (wh_ref, 3)])]
            read = lambda blk, rs: rd_zz[rs, parked(blk)].astype(F32)
            put = lambda ref, col, rs, val: ref.__setitem__((rs, slice(col, col + CB)), val)
            finishes = [
                lambda rs: conv_finish(c0, 1.0, rs),
                lambda rs: conv_finish(W + c0, HEAD_DIM ** -0.5, rs),
                lambda rs: put(zm_ref, 2 * W + c0, rs, rd_zz[rs, parked(0)]),
                lambda rs: put(zm_ref, 3 * W + c0, rs, jax.nn.sigmoid(read(1, rs)).astype(BF16)),
                lambda rs: put(zh_ref, c0, rs, _silu(read(2, rs)).astype(BF16)),
                lambda rs: forget_finish(read(3, rs), c0, rs),
                lambda rs: put(zh_ref, 2 * W + c0, rs, rd_zz[rs, parked(4)]),
                lambda rs: put(zh_ref, 3 * W + c0, rs, _silu(read(5, rs)).astype(BF16)),
            ]
            for (w_ref, wc, dst, dcols), finish in zip(projections, finishes):
                z = _dot(h, w_ref[:, wc:wc + CB])
                if dst is wr_qk:
                    dst[8:8 + tm, dcols] = z
                else:
                    dst[:, dcols] = z.astype(BF16)
                for rs in row_blocks:
                    finish(rs)

        tail[...] = rd_qk[tm:tm + 8, :]

    @pl.when(lax.rem(i, 2) == 0)
    def _():
        body(qk0, zz0, qk1, zz1)

    @pl.when(lax.rem(i, 2) == 1)
    def _():
        body(qk1, zz1, qk0, zz0)


def _in_proj_ring(x2, g_mix, w_m, w_h, w_g, w_gt, b_g, b_gt, conv_w, lb, seq, tm):
    n = x2.shape[0]
    assert seq % tm == 0
    n_tiles = n // tm
    const = lambda i: (0, 0)
    cur = lambda i: (jnp.minimum(i, n_tiles - 1), 0)
    prev = lambda i: (jnp.maximum(i - 1, 0), 0)
    return pl.pallas_call(
        functools.partial(_in_proj_ring_kernel, seq // tm),
        grid=(n_tiles + 1,),
        in_specs=[
            pl.BlockSpec((tm, D_MODEL), cur),
            pl.BlockSpec((1, D_MODEL), const),
            pl.BlockSpec((D_MODEL, 4 * W_MIX), const),
            pl.BlockSpec((D_MODEL, 4 * W_MIX), const),
            pl.BlockSpec((D_MODEL, LANES), const),
            pl.BlockSpec((8, D_MODEL), const),
            pl.BlockSpec((1, LANES), const),
            pl.BlockSpec((8, 1), const),
            pl.BlockSpec((CONV_WIDTH, 2 * W_MIX), const),
            pl.BlockSpec((1, W_MIX), const),
        ],
        out_specs=[
            pl.BlockSpec((tm, 4 * W_MIX), prev),
            pl.BlockSpec((tm, 6 * W_MIX), prev),
            pl.BlockSpec((tm, LANES), cur),
            pl.BlockSpec((8, tm), lambda i: (0, jnp.minimum(i, n_tiles - 1))),
        ],
        out_shape=[
            jax.ShapeDtypeStruct((n, 4 * W_MIX), BF16),
            jax.ShapeDtypeStruct((n, 6 * W_MIX), BF16),
            jax.ShapeDtypeStruct((n, LANES), F32),
            jax.ShapeDtypeStruct((8, n), F32),
        ],
        scratch_shapes=[pltpu.VMEM((tm + 8, 2 * W_MIX), F32)] * 2 + [pltpu.VMEM((tm, 6 * W_MIX), BF16)] * 2
        + [pltpu.VMEM((8, 2 * W_MIX), F32)],
        compiler_params=_params("arbitrary"),
        name="in_proj",
    )(x2, g_mix, w_m, w_h, w_g, w_gt, b_g, b_gt, conv_w, lb)


def _rows_bcast(ref, rows, span, hs):
    return jnp.concatenate([jnp.broadcast_to(ref[r:r + 1, hs], (span, HEAD_DIM)) for r in rows], axis=0)


def _head_cols(base, h):
    return slice(base * W_MIX + h * HEAD_DIM, base * W_MIX + (h + 1) * HEAD_DIM)


def _lane_block(j):
    return slice(j * LANES, (j + 1) * LANES)


def _mixer_kernel(zm_ref, zh_ref, gate_ref, gate_t_ref, gm_ref, gh_ref, sel_ref, y_ref,
                  c_ref, n_ref, m_ref, st_ref, b_scr):
    L = CHUNK

    @pl.when(pl.program_id(1) == 0)
    def _():
        c_ref[...] = jnp.zeros_like(c_ref)
        n_ref[...] = jnp.zeros_like(n_ref)
        m_ref[...] = jnp.zeros_like(m_ref)
        st_ref[...] = jnp.zeros_like(st_ref)

    units = [(bi, h) for bi in range(MIX_BATCH) for h in range(N_HEADS)]
    row = lax.broadcasted_iota(I32, (L, L), 0)
    col = lax.broadcasted_iota(I32, (L, L), 1)
    causal = col <= row
    lower, upper = causal.astype(BF16), (row <= col).astype(BF16)

    gate_t, cols, bcum_r = [], [], []
    for bi in range(MIX_BATCH):
        g = gate_ref[bi]
        g_t = gate_t_ref[bi]
        lf_c, lf_r = _split_hi_lo(_log_sigmoid(g)), _split_hi_lo(_log_sigmoid(g_t))
        gate_t.append(g_t)
        bcum_c = _dot(lower, lf_c[0]) + _dot(lower, lf_c[1])
        bcum_r.append(_dot(lf_r[0], upper) + _dot(lf_r[1], upper))
        mixed = _split_hi_lo(jnp.where(col < N_HEADS, g, bcum_c))
        cols.append(_dot(mixed[0], sel_ref[...]) + _dot(mixed[1], sel_ref[...]))
        b_scr[bi] = (_dot(lower, zh_ref[bi, :, 4 * W_MIX:5 * W_MIX])
                     + _dot(lower, zh_ref[bi, :, 5 * W_MIX:6 * W_MIX]))

    qk, qc = {}, {}
    for u in units:
        bi, h = u
        qb = zm_ref[bi, :, _head_cols(0, h)]
        qk[u] = _dot_nt(qb, zm_ref[bi, :, _head_cols(1, h)])
        qc[u] = _dot(qb, c_ref[bi, h].astype(BF16))

    oi, blocks = {}, {}
    for u in units:
        bi, h = u
        hs = _head_cols(0, h)
        b = b_scr[bi, :, hs]
        q = zh_ref[bi, :, hs].astype(F32)
        k = zh_ref[bi, :, _head_cols(1, h)].astype(F32)
        oi[u] = _dot_nt((q * jnp.exp(b)).astype(BF16), st_ref[bi, h].astype(BF16))
        parts = []
        span = L // 2
        while span >= 16:
            mids = range(span, L, 2 * span)
            r = _rows_bcast(b_scr.at[bi], [m for m in mids for _ in (0, 1)], span, hs)
            low = (row & span) != 0
            z = (jnp.where(low, q, k) * jnp.exp(jnp.where(low, b - r, r - b))).astype(BF16)
            keep = ((row & -(2 * span)) == (col & -(2 * span))) & low & ((col & span) == 0)
            parts.append((keep, _dot_nt(z, z)))
            span //= 2
        r = _rows_bcast(b_scr.at[bi], range(0, L, 16), 16, hs)
        p = _dot_nt((q * jnp.exp(b - r)).astype(BF16), (k * jnp.exp(r - b)).astype(BF16))
        parts.append((((row & -16) == (col & -16)) & causal, p))
        blocks[u] = parts

    log_d, inter, row_max, qn = {}, {}, {}, {}
    for u in units:
        bi, h = u
        bc = cols[bi][:, _lane_block(N_HEADS + h)]
        br = bcum_r[bi][N_HEADS + h:N_HEADS + h + 1, :]
        ir = gate_t[bi][h:h + 1, :]
        log_d[u] = jnp.where(causal, bc - br + ir, -jnp.inf)
        inter[u] = bc + m_ref[bi, h:h + 1, :]
        row_max[u] = jnp.max(log_d[u], axis=-1, keepdims=True)
        q = zm_ref[bi, :, _head_cols(0, h)].astype(F32)
        qn[u] = jnp.sum(q * n_ref[bi, h:h + 1, :], axis=-1, keepdims=True)

    sv, row_sum, w_inter, m_t_all = {}, {}, {}, {}
    for u in units:
        bi, h = u
        m_t = jnp.maximum(inter[u], row_max[u])
        s = qk[u] * jnp.exp(log_d[u] - m_t)
        w_inter[u] = jnp.exp(inter[u] - m_t)
        m_t_all[u] = m_t
        row_sum[u] = jnp.sum(s, axis=-1, keepdims=True)
        sv[u] = _dot(s.astype(BF16), zm_ref[bi, :, _head_cols(2, h)])

    av, sk = {}, {}
    for u in units:
        bi, h = u
        hs = _head_cols(0, h)
        a = None
        for keep, p in blocks[u]:
            term = jnp.where(keep, p, 0.0)
            a = term if a is None else a + term
        v = zh_ref[bi, :, _head_cols(2, h)]
        av[u] = _dot(a.astype(BF16), v)
        k = zh_ref[bi, :, _head_cols(1, h)].astype(F32)
        ke = (k * jnp.exp(b_scr[bi, L - 1:L, hs] - b_scr[bi, :, hs])).astype(BF16)
        sk[u] = _dot_tn(v, ke)

    for u in units:
        bi, h = u
        bc = cols[bi][:, _lane_block(N_HEADS + h)]
        ic = cols[bi][:, _lane_block(h)]
        m_prev = m_ref[bi, h:h + 1, :]
        b_last = bc[L - 1:L, :]
        w_log = b_last - bc + ic
        m_new = jnp.maximum(b_last + m_prev, jnp.max(w_log, axis=0, keepdims=True))
        k = zm_ref[bi, :, _head_cols(1, h)].astype(F32)
        kw = k * jnp.exp(w_log - m_new)
        decay = jnp.exp(b_last + m_prev - m_new)
        c_ref[bi, h] = decay * c_ref[bi, h] + _dot_tn(kw.astype(BF16), zm_ref[bi, :, _head_cols(2, h)])
        n_ref[bi, h:h + 1, :] = decay * n_ref[bi, h:h + 1, :] + jnp.sum(kw, axis=0, keepdims=True)
        m_ref[bi, h:h + 1, :] = m_new

    hh_all, o_all, hh_ms, o_ms = {}, {}, {}, {}
    for u in units:
        bi, h = u
        hs = _head_cols(0, h)
        den = row_sum[u] + w_inter[u] * qn[u]
        num = sv[u] + w_inter[u] * qc[u]
        hh = num * (1.0 / jnp.maximum(jnp.abs(den), jnp.exp(-m_t_all[u])))
        hh = hh * zm_ref[bi, :, _head_cols(3, h)].astype(F32)
        hh_all[u] = hh
        hh_ms[u] = jnp.mean(hh * hh, axis=-1, keepdims=True)

        st_ref[bi, h] = jnp.exp(b_scr[bi, L - 1:L, hs]) * st_ref[bi, h] + sk[u]
        o = oi[u] + av[u]
        o_all[u] = o
        o_ms[u] = jnp.mean(o * o, axis=-1, keepdims=True)

    for u in units:
        bi, h = u
        hs = _head_cols(0, h)
        y_ref[bi, :, hs] = (hh_all[u] * lax.rsqrt(hh_ms[u] + EPS) * gm_ref[:, hs]).astype(BF16)
        o = o_all[u] * lax.rsqrt(o_ms[u] + EPS) * gh_ref[...]
        o = o * zh_ref[bi, :, _head_cols(3, h)].astype(F32)
        y_ref[bi, :, _head_cols(1, h)] = o.astype(BF16)


def _mixer(zm, zh, gate, gate_t, g_mlstm, g_hgrn, batch, n_chunks):
    n = zm.shape[0]
    seq = n // batch
    assert batch % MIX_BATCH == 0
    blk = lambda b, c: (b, c, 0)
    const = lambda b, c: (0, 0)
    sel = (jnp.arange(LANES)[:, None] == jnp.arange(2 * N_HEADS * LANES)[None, :] // LANES).astype(BF16)
    return pl.pallas_call(
        _mixer_kernel,
        grid=(batch // MIX_BATCH, n_chunks),
        in_specs=[
            pl.BlockSpec((MIX_BATCH, CHUNK, 4 * W_MIX), blk),
            pl.BlockSpec((MIX_BATCH, CHUNK, 6 * W_MIX), blk),
            pl.BlockSpec((MIX_BATCH, CHUNK, LANES), blk),
            pl.BlockSpec((MIX_BATCH, 8, CHUNK), lambda b, c: (b, 0, c)),
            pl.BlockSpec((1, W_MIX), const),
            pl.BlockSpec((1, HEAD_DIM), const),
            pl.BlockSpec((LANES, 2 * N_HEADS * LANES), const),
        ],
        out_specs=pl.BlockSpec((MIX_BATCH, CHUNK, 2 * W_MIX), blk),
        out_shape=jax.ShapeDtypeStruct((batch, seq, 2 * W_MIX), BF16),
        scratch_shapes=[
            pltpu.VMEM((MIX_BATCH, N_HEADS, HEAD_DIM, HEAD_DIM), F32),
            pltpu.VMEM((MIX_BATCH, 8, HEAD_DIM), F32),
            pltpu.VMEM((MIX_BATCH, 8, LANES), F32),
            pltpu.VMEM((MIX_BATCH, N_HEADS, HEAD_DIM, HEAD_DIM), F32),
            pltpu.VMEM((MIX_BATCH, CHUNK, W_MIX), F32),
        ],
        compiler_params=_params("parallel", "arbitrary"),
        name="mixer",
    )(zm.reshape(batch, seq, 4 * W_MIX), zh.reshape(batch, seq, 6 * W_MIX), gate.reshape(batch, seq, LANES),
      gate_t.reshape(8, batch, seq).transpose(1, 0, 2), g_mlstm, g_hgrn, sel).reshape(n, 2 * W_MIX)


def _pack_bf16_pair(lo, hi):
    lo_bits = pltpu.bitcast(lo.astype(BF16).astype(F32), U32)
    hi_bits = pltpu.bitcast(hi.astype(BF16).astype(F32), U32)
    return (hi_bits & jnp.uint32(0xFFFF0000)) | (lo_bits >> 16)


def _unpack_bf16_pair(w):
    lo = pltpu.bitcast(w << 16, F32).astype(BF16)
    hi = pltpu.bitcast(w & jnp.uint32(0xFFFF0000), F32).astype(BF16)
    return lo, hi


def _post_mix_kernel(x_ref, y_ref, wo_ref, g_ref, wrh_ref, wrl_ref, br_ref,
                     x1_ref, hn_ref, route_t_ref, comb_ref, count_ref, run_ref):
    tm = x_ref.shape[0]
    n_r = wrh_ref.shape[0]

    @pl.when(pl.program_id(0) == 0)
    def _():
        run_ref[...] = jnp.zeros_like(run_ref)

    x1 = x_ref[...] + _dot(y_ref[...], wo_ref[...])
    x1_ref[...] = x1
    hn = _rms(x1, g_ref[...])
    half = D_MODEL // 2
    packed = _pack_bf16_pair(hn[:, :half], hn[:, half:])
    for j in range(PACK_ROWS):
        hn_ref[pl.ds(j, tm, stride=PACK_ROWS), :] = packed[:, j * LANES:(j + 1) * LANES]

    hn_hi = hn.astype(BF16)
    hn_lo = (hn - hn_hi.astype(F32)).astype(BF16)
    logits = (_dot_nt(wrh_ref[...], hn_hi) + (_dot_nt(wrl_ref[...], hn_hi) + _dot_nt(wrh_ref[...], hn_lo))
              + br_ref[...])
    rix = lax.broadcasted_iota(I32, (n_r, tm), 0)
    neg = -jnp.inf
    g_l = jnp.where(rix < N_GROUPS, logits, neg)
    g_max = jnp.max(g_l, axis=0, keepdims=True)
    g_sel = jnp.min(jnp.where(g_l == g_max, rix, n_r), axis=0, keepdims=True)
    g_val = 1.0 / jnp.sum(jnp.exp(g_l - g_max), axis=0, keepdims=True)

    e_row = rix - N_GROUPS
    in_group = (e_row >= g_sel * EXPERTS_PER_GROUP) & (e_row < (g_sel + 1) * EXPERTS_PER_GROUP)
    e_l = jnp.where(in_group, logits, neg)
    v1 = jnp.max(e_l, axis=0, keepdims=True)
    i1 = jnp.min(jnp.where(e_l == v1, rix, n_r), axis=0, keepdims=True)
    e_l2 = jnp.where(rix == i1, neg, e_l)
    v2 = jnp.max(e_l2, axis=0, keepdims=True)
    i2 = jnp.min(jnp.where(e_l2 == v2, rix, n_r), axis=0, keepdims=True)
    t = jnp.exp(v2 - v1)
    c1 = g_val / (1.0 + t)
    c2 = g_val * t / (1.0 + t)

    hot1 = rix == i1
    hot2 = rix == i2
    hot = (hot1 | hot2).astype(F32)
    r_i = lax.broadcasted_iota(I32, (tm, tm), 0)
    c_i = lax.broadcasted_iota(I32, (tm, tm), 1)
    before = _dot(hot.astype(BF16), (r_i < c_i).astype(BF16)) + run_ref[:, 0:1]
    rank1 = jnp.sum(jnp.where(hot1, before, 0.0), axis=0, keepdims=True)
    rank2 = jnp.sum(jnp.where(hot2, before, 0.0), axis=0, keepdims=True)
    run_ref[...] = run_ref[...] + jnp.sum(hot, axis=1, keepdims=True)
    count_ref[...] = run_ref[...]

    r8 = lax.broadcasted_iota(I32, (8, tm), 0)
    out = jnp.where(r8 == 0, (i1 - N_GROUPS).astype(F32), 0.0)
    out = jnp.where(r8 == 1, (i2 - N_GROUPS).astype(F32), out)
    out = jnp.where(r8 == 2, rank1, out)
    out = jnp.where(r8 == 3, rank2, out)
    route_t_ref[...] = out

    r128 = lax.broadcasted_iota(I32, (LANES, tm), 0)
    slab = jnp.where(r128 == 0, c1, jnp.where(r128 == 1, c2, 0.0))
    for c in range(tm // LANES):
        comb_ref[c * LANES:(c + 1) * LANES, :] = slab[:, c * LANES:(c + 1) * LANES].T


def _post_mix(x2, y, w_out, g_ffn, wr_hi, wr_lo, b_r, tm):
    n = x2.shape[0]
    n_r = wr_hi.shape[0]
    const = lambda i: (0, 0)
    rows = lambda i: (i, 0)
    return pl.pallas_call(
        _post_mix_kernel,
        grid=(n // tm,),
        in_specs=[
            pl.BlockSpec((tm, D_MODEL), rows),
            pl.BlockSpec((tm, 2 * W_MIX), rows),
            pl.BlockSpec((2 * W_MIX, D_MODEL), const),
            pl.BlockSpec((1, D_MODEL), const),
            pl.BlockSpec((n_r, D_MODEL), const),
            pl.BlockSpec((n_r, D_MODEL), const),
            pl.BlockSpec((n_r, 1), const),
        ],
        out_specs=[
            pl.BlockSpec((tm, D_MODEL), rows),
            pl.BlockSpec((tm * PACK_ROWS, LANES), rows),
            pl.BlockSpec((8, tm), lambda i: (0, i)),
            pl.BlockSpec((tm, LANES), rows),
            pl.BlockSpec((n_r, LANES), const),
        ],
        out_shape=[
            jax.ShapeDtypeStruct((n, D_MODEL), F32),
            jax.ShapeDtypeStruct((n * PACK_ROWS, LANES), U32),
            jax.ShapeDtypeStruct((8, n), F32),
            jax.ShapeDtypeStruct((n, LANES), F32),
            jax.ShapeDtypeStruct((n_r, LANES), F32),
        ],
        scratch_shapes=[pltpu.VMEM((n_r, LANES), F32)],
        compiler_params=_params("arbitrary"),
        name="post_mix",
    )(x2, y, w_out, g_ffn, wr_hi, wr_lo, b_r)


def _dispatch_kernel(zero_ref, d0_ref, d1_ref, hn_ref, wg_ref, wu_ref, wd_ref,
                     xb_ref, wgb_ref, wub_ref, wdb_ref, zbuf, sem):
    ts = hn_ref.shape[0] // PACK_ROWS
    block_rows = EXPERT_BLOCK * PACK_ROWS

    @pl.when(pl.program_id(0) == 0)
    def _():
        zbuf[...] = jnp.zeros_like(zbuf)

        def zero_copy(j):
            row0 = pl.multiple_of(jnp.maximum(zero_ref[j], 0) * block_rows, block_rows)
            return pltpu.make_async_copy(zbuf, xb_ref.at[pl.ds(row0, block_rows), :], sem.at[2])

        def start_zero(j, carry):
            @pl.when(zero_ref[j] >= 0)
            def _():
                zero_copy(j).start()
            return carry

        def wait_zero(j, carry):
            @pl.when(zero_ref[j] >= 0)
            def _():
                zero_copy(j).wait()
            return carry

        lax.fori_loop(0, zero_ref.shape[0], start_zero, 0)
        lax.fori_loop(0, zero_ref.shape[0], wait_zero, 0)

    def start(t, carry):
        src = pl.multiple_of(t * PACK_ROWS, PACK_ROWS)
        for k, d_ref in enumerate((d0_ref, d1_ref)):
            dst = pl.multiple_of(d_ref[0, 0, t] * PACK_ROWS, PACK_ROWS)
            pltpu.make_async_copy(hn_ref.at[pl.ds(src, PACK_ROWS), :], xb_ref.at[pl.ds(dst, PACK_ROWS), :],
                                  sem.at[k]).start(priority=k)
        return carry

    lax.fori_loop(0, ts, start, 0, unroll=DMA_UNROLL)
    wgb_ref[...] = wg_ref[...].astype(BF16)
    wub_ref[...] = wu_ref[...].astype(BF16)
    wdb_ref[...] = wd_ref[...].astype(BF16)

    for k in range(2):
        pltpu.make_async_copy(hn_ref, xb_ref.at[pl.ds(0, ts * PACK_ROWS), :], sem.at[k]).wait()


def _dispatch(zero_blocks, dest0, dest1, hn, w_gate, w_up, w_down, n_rows, ts):
    n = hn.shape[0] // PACK_ROWS
    steps = n // ts
    if steps >= N_EXPERTS:
        parts = steps // N_EXPERTS
        assert steps == parts * N_EXPERTS
        w_spec = lambda rows, cols: pl.BlockSpec((None, rows // parts, cols), lambda i, z: (i // parts, i % parts, 0))
    else:
        per_step = N_EXPERTS // steps
        assert N_EXPERTS == per_step * steps
        w_spec = lambda rows, cols: pl.BlockSpec((per_step, rows, cols), lambda i, z: (i, 0, 0))
    w_specs = [w_spec(D_MODEL, D_EXPERT), w_spec(D_MODEL, D_EXPERT), w_spec(D_EXPERT, D_MODEL)]
    grid_spec = pltpu.PrefetchScalarGridSpec(
        num_scalar_prefetch=1,
        grid=(steps,),
        in_specs=[
            pl.BlockSpec((1, 1, ts), lambda i, z: (i, 0, 0), memory_space=pltpu.SMEM),
            pl.BlockSpec((1, 1, ts), lambda i, z: (i, 0, 0), memory_space=pltpu.SMEM),
            pl.BlockSpec((ts * PACK_ROWS, LANES), lambda i, z: (i, 0)),
        ] + w_specs,
        out_specs=[pl.BlockSpec(memory_space=pl.ANY)] + w_specs,
        scratch_shapes=[pltpu.VMEM((EXPERT_BLOCK * PACK_ROWS, LANES), U32), pltpu.SemaphoreType.DMA((3,))],
    )
    return pl.pallas_call(
        _dispatch_kernel,
        grid_spec=grid_spec,
        out_shape=[jax.ShapeDtypeStruct((n_rows * PACK_ROWS, LANES), U32)]
        + [jax.ShapeDtypeStruct(w.shape, BF16) for w in (w_gate, w_up, w_down)],
        compiler_params=_params("arbitrary"),
        name="dispatch",
    )(zero_blocks, dest0, dest1, hn, w_gate, w_up, w_down)


def _experts_kernel(be_ref, bv_ref, nu_ref, xb_ref, wg_ref, wu_ref, wd_ref, yb_ref):
    valid = bv_ref[pl.program_id(0)]
    half = EXPERT_BLOCK // 2

    def mlp(rows):
        pairs = [_unpack_bf16_pair(xb_ref[pl.ds(j, rows, stride=PACK_ROWS), :]) for j in range(PACK_ROWS)]
        x = jnp.concatenate([lo for lo, _ in pairs] + [hi for _, hi in pairs], axis=1)
        a = (_silu(_dot(x, wg_ref[...])) * _dot(x, wu_ref[...])).astype(BF16)
        y = _dot(a, wd_ref[...])
        half_d = D_MODEL // 2
        packed = _pack_bf16_pair(y[:, :half_d], y[:, half_d:])
        for j in range(PACK_ROWS):
            yb_ref[pl.ds(j, rows, stride=PACK_ROWS), :] = packed[:, j * LANES:(j + 1) * LANES]

    @pl.when(valid > half)
    def _():
        mlp(EXPERT_BLOCK)

    @pl.when((valid > 0) & (valid <= half))
    def _():
        mlp(half)
        yb_ref[half * PACK_ROWS:, :] = jnp.zeros((half * PACK_ROWS, LANES), U32)

    @pl.when(valid == 0)
    def _():
        yb_ref[...] = jnp.zeros_like(yb_ref)


def _experts(block_e, block_valid, n_used, xb, w_gate, w_up, w_down):
    n_rows = xb.shape[0] // PACK_ROWS
    n_blocks = n_rows // EXPERT_BLOCK
    xrow = lambda i, be, bv, nu: (jnp.maximum(jnp.minimum(i, nu[0] - 1), 0), 0)
    wsel = lambda i, be, bv, nu: (be[i], 0, 0)
    grid_spec = pltpu.PrefetchScalarGridSpec(
        num_scalar_prefetch=3,
        grid=(n_blocks,),
        in_specs=[
            pl.BlockSpec((EXPERT_BLOCK * PACK_ROWS, LANES), xrow),
            pl.BlockSpec((None, D_MODEL, D_EXPERT), wsel),
            pl.BlockSpec((None, D_MODEL, D_EXPERT), wsel),
            pl.BlockSpec((None, D_EXPERT, D_MODEL), wsel),
        ],
        out_specs=pl.BlockSpec((EXPERT_BLOCK * PACK_ROWS, LANES), lambda i, be, bv, nu: (i, 0)),
    )
    return pl.pallas_call(
        _experts_kernel,
        grid_spec=grid_spec,
        out_shape=jax.ShapeDtypeStruct((n_rows * PACK_ROWS, LANES), U32),
        compiler_params=_params("arbitrary"),
        name="experts",
    )(block_e, block_valid, n_used, xb, w_gate, w_up, w_down)


def _combine_kernel(d0_ref, d1_ref, d0a_ref, d1a_ref, d0b_ref, d1b_ref, x1_ref, p_ref, comb_ref, gpl_ref, wplg_ref,
                    wplp_ref, gfin_ref, yb_ref, out_ref, gbuf, x3_ref, sem):
    tf = x1_ref.shape[0]
    i = pl.program_id(0)
    n_steps = pl.num_programs(0)
    slot = lax.rem(i, GATHER_BUFS)
    slot_a = lax.rem(i + 1, GATHER_BUFS)
    slot_b = lax.rem(i + 2, GATHER_BUFS)

    def start_gather(d_refs, s, t):
        row = pl.multiple_of(t * PACK_ROWS, PACK_ROWS)
        for k, d_ref in enumerate(d_refs):
            src = pl.multiple_of(d_ref[0, 0, t] * PACK_ROWS, PACK_ROWS)
            pltpu.make_async_copy(yb_ref.at[pl.ds(src, PACK_ROWS), :], gbuf.at[s, k, pl.ds(row, PACK_ROWS), :],
                                  sem.at[s, k]).start(priority=k)

    def wait_gathers(s):
        for k in range(2):
            pltpu.make_async_copy(yb_ref.at[pl.ds(0, tf * PACK_ROWS), :], gbuf.at[s, k], sem.at[s, k]).wait()

    @pl.when(i == 0)
    def _():
        lax.fori_loop(0, tf, lambda t, c: (start_gather((d0_ref, d1_ref), 0, t), c)[1], 0, unroll=DMA_UNROLL)
        lax.fori_loop(0, tf, lambda t, c: (start_gather((d0a_ref, d1a_ref), 1, t), c)[1], 0, unroll=DMA_UNROLL)

    wait_gathers(slot)

    def gathered(k):
        words = [gbuf[slot, k, pl.ds(j, tf, stride=PACK_ROWS), :] for j in range(PACK_ROWS)]
        return jnp.concatenate([pltpu.bitcast(w << 16, F32) for w in words]
                               + [pltpu.bitcast(w & jnp.uint32(0xFFFF0000), F32) for w in words], axis=1)

    comb = comb_ref[...]
    y = comb[:, 0:1] * gathered(0) + comb[:, 1:2] * gathered(1)
    x2 = x1_ref[...] + y
    hb = _rms(x2, gpl_ref[...]).astype(BF16)
    pb = p_ref[...].astype(BF16)
    n_col = D_MODEL // COMBINE_COLS
    for c in range(n_col):
        for t in range(c * tf // n_col, (c + 1) * tf // n_col):
            start_gather((d0b_ref, d1b_ref), slot_b, t)
        cs = slice(c * COMBINE_COLS, (c + 1) * COMBINE_COLS)
        gate = jax.nn.sigmoid(_dot(hb, wplg_ref[:, cs]))
        x3_ref[:, cs] = x2[:, cs] + gate * _dot(pb, wplp_ref[:, cs])
    x3 = x3_ref[...]
    out_ref[...] = _rms(x3, gfin_ref[...])

    @pl.when(i == n_steps - 1)
    def _():
        wait_gathers(slot_a)
        wait_gathers(slot_b)


def _combine(dest0, dest1, x1, p2, comb, g_pl, w_plg, w_plp, g_final, yb, tf):
    n = x1.shape[0]
    n_steps = n // tf
    assert n_steps >= GATHER_BUFS - 1
    const = lambda i: (0, 0)
    rows = lambda i: (i, 0)
    ahead = lambda a: (lambda i: (jnp.minimum(i + a, n_steps - 1), 0, 0))
    dest_specs = [pl.BlockSpec((1, 1, tf), ahead(a), memory_space=pltpu.SMEM) for a in range(GATHER_BUFS)
                  for _ in range(2)]
    return pl.pallas_call(
        _combine_kernel,
        grid=(n_steps,),
        in_specs=dest_specs + [
            pl.BlockSpec((tf, D_MODEL), rows),
            pl.BlockSpec((tf, PLE_DIM), rows),
            pl.BlockSpec((tf, LANES), rows),
            pl.BlockSpec((1, D_MODEL), const),
            pl.BlockSpec((D_MODEL, D_MODEL), const),
            pl.BlockSpec((PLE_DIM, D_MODEL), const),
            pl.BlockSpec((1, D_MODEL), const),
            pl.BlockSpec(memory_space=pl.ANY),
        ],
        out_specs=pl.BlockSpec((tf, D_MODEL), rows),
        out_shape=jax.ShapeDtypeStruct((n, D_MODEL), F32),
        scratch_shapes=[pltpu.VMEM((GATHER_BUFS, 2, tf * PACK_ROWS, LANES), U32), pltpu.VMEM((tf, D_MODEL), F32),
                        pltpu.SemaphoreType.DMA((GATHER_BUFS, 2))],
        compiler_params=_params("arbitrary"),
        name="combine",
    )(dest0, dest1, dest0, dest1, dest0, dest1, x1, p2, comb, g_pl, w_plg, w_plp, g_final, yb)


def _layer(x2, p2, batch, seq, g_mix, w_in, b_mgate, conv_qk, g_mlstm, lb, g_hgrn, w_out, g_ffn,
           w_rg, b_rg, w_re, b_re, w_e_gate, w_e_up, w_e_down, g_pl, w_pl_gate, w_pl_proj, g_out):
    n = x2.shape[0]
    n_chunks = seq // CHUNK
    m_cols = 4 * W_MIX
    n_gate = 2 * N_HEADS

    w_in_b = w_in.astype(BF16)
    w_m = w_in_b[:, :m_cols]
    w_gcols = w_in_b[:, m_cols:m_cols + n_gate]
    w_h = w_in_b[:, m_cols + n_gate:]
    w_g = jnp.pad(w_gcols, ((0, 0), (0, LANES - n_gate)))
    b_g = jnp.pad(b_mgate.astype(F32), (0, LANES - n_gate))[None, :]
    zm, zh, gate, gate_t = _in_proj(x2, g_mix[None, :], w_m, w_h, w_g, w_gcols.T, b_g,
                                    b_mgate.astype(F32)[:, None], conv_qk, lb[None, :], seq, tm=min(1024, seq))

    y = _mixer(zm, zh, gate, gate_t, g_mlstm[None, :], g_hgrn[None, :], batch, n_chunks)

    n_logit = N_GROUPS + N_EXPERTS
    w_rt = jnp.pad(jnp.concatenate([w_rg, w_re], axis=1).T, ((0, ROUTER_ROWS - n_logit), (0, 0)))
    wr_hi = w_rt.astype(BF16)
    wr_lo = (w_rt - wr_hi.astype(F32)).astype(BF16)
    b_r = jnp.pad(jnp.concatenate([b_rg, b_re]), (0, ROUTER_ROWS - n_logit))[:, None]
    x1, hn, route_t, comb, counts = _post_mix(x2, y, w_out.astype(BF16), g_ffn[None, :], wr_hi, wr_lo, b_r,
                                              tm=min(512, n))

    counts = counts[N_GROUPS:n_logit, 0].astype(I32)
    padded = (counts + EXPERT_BLOCK - 1) // EXPERT_BLOCK * EXPERT_BLOCK
    pend = jnp.cumsum(padded)
    pstart = pend - padded
    n_blocks = (2 * n) // EXPERT_BLOCK + N_EXPERTS
    n_rows = n_blocks * EXPERT_BLOCK
    n_used = (pend[-1] // EXPERT_BLOCK).astype(I32)
    block_ids = jnp.arange(n_blocks, dtype=I32)
    block_row = jnp.minimum(block_ids, n_used - 1) * EXPERT_BLOCK
    block_e = jnp.sum((pend[None, :] <= block_row[:, None]).astype(I32), axis=1)
    token_end = jnp.sum(jnp.where(block_e[:, None] == jnp.arange(N_EXPERTS, dtype=I32), pstart + counts, 0), axis=1)
    block_valid = jnp.where(block_ids < n_used, jnp.clip(token_end - block_row, 0, EXPERT_BLOCK), 0).astype(I32)
    expert_id = route_t[0:2].astype(I32)
    hot = expert_id[:, :, None] == jnp.arange(N_EXPERTS, dtype=I32)
    dest = jnp.sum(jnp.where(hot, pstart, 0), axis=-1) + route_t[2:4].astype(I32)

    ts = min(1024, n)
    last_block = jnp.where(counts % EXPERT_BLOCK != 0, pend // EXPERT_BLOCK - 1, -1)
    tail_block = n_used + jnp.arange(N_EXPERTS, dtype=I32)
    zero_blocks = jnp.concatenate([last_block, jnp.where(tail_block < n_blocks, tail_block, -1)]).astype(I32)
    xb, wg_b, wu_b, wd_b = _dispatch(zero_blocks, dest[0].reshape(n // ts, 1, ts), dest[1].reshape(n // ts, 1, ts),
                                     hn, w_e_gate, w_e_up, w_e_down, n_rows, ts)
    yb = _experts(block_e, block_valid, n_used[None], xb, wg_b, wu_b, wd_b)
    tf = min(512, n)
    return _combine(dest[0].reshape(n // tf, 1, tf), dest[1].reshape(n // tf, 1, tf), x1, p2, comb, g_pl[None, :],
                    w_pl_gate.astype(BF16), w_pl_proj.astype(BF16), g_out[None, :], yb, tf)


def kernel(x, p, g_mix, w_in, b_mgate, conv_qk, g_mlstm, hg_lb, g_hgrn, w_out, g_ffn, w_rg, b_rg, w_re, b_re,
           w_e_gate, w_e_up, w_e_down, g_pl, w_pl_gate, w_pl_proj, g_final):
    batch, seq, d = x.shape
    depth = p.shape[0]
    assert depth == 1, "the fused final norm assumes a single layer"
    lower_bounds = jnp.cumsum(jax.nn.softmax(hg_lb.astype(F32), axis=0), axis=0)
    i = 0
    out = _layer(x.reshape(batch * seq, d), p[i].reshape(batch * seq, PLE_DIM), batch, seq,
                 g_mix[i], w_in[i], b_mgate[i], conv_qk[i], g_mlstm[i], lower_bounds[i], g_hgrn[i], w_out[i],
                 g_ffn[i], w_rg[i], b_rg[i], w_re[i], b_re[i], w_e_gate[i], w_e_up[i], w_e_down[i],
                 g_pl[i], w_pl_gate[i], w_pl_proj[i], g_final)
    return out.reshape(batch, seq, d)
```

```python
import functools

import jax
import jax.numpy as jnp
from jax import lax
from jax.experimental import pallas as pl
from jax.experimental.pallas import tpu as pltpu

F32 = jnp.float32
BF16 = jnp.bfloat16
I32 = jnp.int32
U32 = jnp.uint32
EPS = 1e-6

LANES = 128
D_MODEL = 1024
W_MIX = 512
N_HEADS = 4
HEAD_DIM = 128
N_GROUPS = 4
EXPERTS_PER_GROUP = 8
N_EXPERTS = N_GROUPS * EXPERTS_PER_GROUP
D_EXPERT = 512
PLE_DIM = 256
CONV_WIDTH = 4
CHUNK = 128
EXPERT_BLOCK = 512
PACK_ROWS = D_MODEL // 2 // LANES
ROUTER_ROWS = 48
DMA_UNROLL = 4
COMBINE_COLS = 256
MIX_BATCH = 4
IN_PROJ_COLS = 256
IN_PROJ_AHEAD = 1
GATHER_BUFS = 3
VMEM_LIMIT = 56 * 1024 * 1024


def _dot(a, b):
    return jnp.dot(a, b, preferred_element_type=F32)


def _dot_nt(a, b):
    return lax.dot_general(a, b, (((1,), (1,)), ((), ())), preferred_element_type=F32)


def _dot_tn(a, b):
    return lax.dot_general(a, b, (((0,), (0,)), ((), ())), preferred_element_type=F32)


def _rms(u, g):
    return u * lax.rsqrt(jnp.mean(u * u, axis=-1, keepdims=True) + EPS) * g


def _silu(u):
    return u * jax.nn.sigmoid(u)


def _log_sigmoid(u):
    return jnp.minimum(u, 0.0) - jnp.log1p(jnp.exp(-jnp.abs(u)))


def _split_hi_lo(u):
    hi = u.astype(BF16)
    return hi, (u - hi.astype(F32)).astype(BF16)


def _params(*sem):
    return pltpu.CompilerParams(dimension_semantics=sem, vmem_limit_bytes=VMEM_LIMIT)


def _in_proj_kernel(tiles_per_seq, x_ref, g_ref, wm_ref, wh_ref, wg_ref, wgt_ref, bg_ref, bgt_ref, conv_ref, lb_ref,
                    zm_ref, zh_ref, gate_ref, gate_t_ref, cbuf):
    tm = x_ref.shape[0]
    W = W_MIX
    CB = IN_PROJ_COLS

    @pl.when(pl.program_id(0) % tiles_per_seq == 0)
    def _():
        cbuf[0:8, :] = jnp.zeros((8, 2 * W), F32)

    h = _rms(x_ref[...], g_ref[...]).astype(BF16)

    def conv_finish(col0, scale):
        def finish(z):
            cs = slice(col0, col0 + CB)
            cbuf[8:8 + tm, cs] = z
            acc = z * conv_ref[CONV_WIDTH - 1:CONV_WIDTH, cs]
            for j in range(CONV_WIDTH - 1):
                acc = acc + cbuf[5 + j:5 + j + tm, cs] * conv_ref[j:j + 1, cs]
            cbuf[0:8, cs] = cbuf[tm:tm + 8, cs]
            zm_ref[:, cs] = (_silu(acc) * scale).astype(BF16)
        return finish

    def store(ref, col0, fn):
        def finish(z):
            ref[:, col0:col0 + CB] = fn(z).astype(BF16)
        return finish

    def forget_finish(col0):
        def finish(z):
            lb = lb_ref[:, col0:col0 + CB]
            zh_ref[:, W + col0:W + col0 + CB] = ((1.0 - lb) * jax.nn.sigmoid(-z)).astype(BF16)
            lf_hi, lf_lo = _split_hi_lo(jnp.log(lb + (1.0 - lb) * jax.nn.sigmoid(z)))
            zh_ref[:, 4 * W + col0:4 * W + col0 + CB] = lf_hi
            zh_ref[:, 5 * W + col0:5 * W + col0 + CB] = lf_lo
        return finish

    ident = lambda z: z
    plan = []
    for half in range(W // CB):
        c0 = half * CB
        plan += [
            (wm_ref, c0, conv_finish(c0, 1.0)),
            (wh_ref, 2 * W + c0, store(zh_ref, 2 * W + c0, ident)),
            (wm_ref, W + c0, conv_finish(W + c0, HEAD_DIM ** -0.5)),
            (wm_ref, 2 * W + c0, store(zm_ref, 2 * W + c0, ident)),
            (wh_ref, W + c0, forget_finish(c0)),
            (wm_ref, 3 * W + c0, store(zm_ref, 3 * W + c0, jax.nn.sigmoid)),
            (wh_ref, c0, store(zh_ref, c0, _silu)),
            (wh_ref, 3 * W + c0, store(zh_ref, 3 * W + c0, _silu)),
        ]

    project = lambda i: _dot(h, plan[i][0][:, plan[i][1]:plan[i][1] + CB])
    zs = [project(i) for i in range(IN_PROJ_AHEAD)]
    for i in range(len(plan)):
        if i + IN_PROJ_AHEAD < len(plan):
            zs.append(project(i + IN_PROJ_AHEAD))
        plan[i][2](zs[i])

    gate_ref[...] = _dot(h, wg_ref[...]) + bg_ref[...]
    gate_t_ref[...] = _dot_nt(wgt_ref[...], h) + bgt_ref[...]


def _in_proj(x2, g_mix, w_m, w_h, w_g, w_gt, b_g, b_gt, conv_w, lb, seq, tm):
    n = x2.shape[0]
    assert seq % tm == 0
    const = lambda i: (0, 0)
    rows = lambda i: (i, 0)
    return pl.pallas_call(
        functools.partial(_in_proj_kernel, seq // tm),
        grid=(n // tm,),
        in_specs=[
            pl.BlockSpec((tm, D_MODEL), rows),
            pl.BlockSpec((1, D_MODEL), const),
            pl.BlockSpec((D_MODEL, 4 * W_MIX), const),
            pl.BlockSpec((D_MODEL, 4 * W_MIX), const),
            pl.BlockSpec((D_MODEL, LANES), const),
            pl.BlockSpec((8, D_MODEL), const),
            pl.BlockSpec((1, LANES), const),
            pl.BlockSpec((8, 1), const),
            pl.BlockSpec((CONV_WIDTH, 2 * W_MIX), const),
            pl.BlockSpec((1, W_MIX), const),
        ],
        out_specs=[
            pl.BlockSpec((tm, 4 * W_MIX), rows),
            pl.BlockSpec((tm, 6 * W_MIX), rows),
            pl.BlockSpec((tm, LANES), rows),
            pl.BlockSpec((8, tm), lambda i: (0, i)),
        ],
        out_shape=[
            jax.ShapeDtypeStruct((n, 4 * W_MIX), BF16),
            jax.ShapeDtypeStruct((n, 6 * W_MIX), BF16),
            jax.ShapeDtypeStruct((n, LANES), F32),
            jax.ShapeDtypeStruct((8, n), F32),
        ],
        scratch_shapes=[pltpu.VMEM((tm + 8, 2 * W_MIX), F32)],
        compiler_params=_params("arbitrary"),
        name="in_proj",
    )(x2, g_mix, w_m, w_h, w_g, w_gt, b_g, b_gt, conv_w, lb)


def _rows_bcast(ref, rows, span, hs):
    return jnp.concatenate([jnp.broadcast_to(ref[r:r + 1, hs], (span, HEAD_DIM)) for r in rows], axis=0)


def _head_cols(base, h):
    return slice(base * W_MIX + h * HEAD_DIM, base * W_MIX + (h + 1) * HEAD_DIM)


def _lane_block(j):
    return slice(j * LANES, (j + 1) * LANES)


def _mixer_kernel(zm_ref, zh_ref, gate_ref, gate_t_ref, gm_ref, gh_ref, sel_ref, y_ref,
                  c_ref, n_ref, m_ref, st_ref, b_scr):
    L = CHUNK

    @pl.when(pl.program_id(1) == 0)
    def _():
        c_ref[...] = jnp.zeros_like(c_ref)
        n_ref[...] = jnp.zeros_like(n_ref)
        m_ref[...] = jnp.zeros_like(m_ref)
        st_ref[...] = jnp.zeros_like(st_ref)

    units = [(bi, h) for bi in range(MIX_BATCH) for h in range(N_HEADS)]
    row = lax.broadcasted_iota(I32, (L, L), 0)
    col = lax.broadcasted_iota(I32, (L, L), 1)
    causal = col <= row
    lower, upper = causal.astype(BF16), (row <= col).astype(BF16)

    gate_t, cols, bcum_r = [], [], []
    for bi in range(MIX_BATCH):
        g = gate_ref[bi]
        g_t = gate_t_ref[bi]
        lf_c, lf_r = _split_hi_lo(_log_sigmoid(g)), _split_hi_lo(_log_sigmoid(g_t))
        gate_t.append(g_t)
        bcum_c = _dot(lower, lf_c[0]) + _dot(lower, lf_c[1])
        bcum_r.append(_dot(lf_r[0], upper) + _dot(lf_r[1], upper))
        mixed = _split_hi_lo(jnp.where(col < N_HEADS, g, bcum_c))
        cols.append(_dot(mixed[0], sel_ref[...]) + _dot(mixed[1], sel_ref[...]))
        b_scr[bi] = (_dot(lower, zh_ref[bi, :, 4 * W_MIX:5 * W_MIX])
                     + _dot(lower, zh_ref[bi, :, 5 * W_MIX:6 * W_MIX]))

    qk, qc = {}, {}
    for u in units:
        bi, h = u
        qb = zm_ref[bi, :, _head_cols(0, h)]
        qk[u] = _dot_nt(qb, zm_ref[bi, :, _head_cols(1, h)])
        qc[u] = _dot(qb, c_ref[bi, h].astype(BF16))

    oi, blocks = {}, {}
    for u in units:
        bi, h = u
        hs = _head_cols(0, h)
        b = b_scr[bi, :, hs]
        q = zh_ref[bi, :, hs].astype(F32)
        k = zh_ref[bi, :, _head_cols(1, h)].astype(F32)
        oi[u] = _dot_nt((q * jnp.exp(b)).astype(BF16), st_ref[bi, h].astype(BF16))
        parts = []
        span = L // 2
        while span >= 16:
            mids = range(span, L, 2 * span)
            r = _rows_bcast(b_scr.at[bi], [m for m in mids for _ in (0, 1)], span, hs)
            low = (row & span) != 0
            z = (jnp.where(low, q, k) * jnp.exp(jnp.where(low, b - r, r - b))).astype(BF16)
            keep = ((row & -(2 * span)) == (col & -(2 * span))) & low & ((col & span) == 0)
            parts.append((keep, _dot_nt(z, z)))
            span //= 2
        r = _rows_bcast(b_scr.at[bi], range(0, L, 16), 16, hs)
        p = _dot_nt((q * jnp.exp(b - r)).astype(BF16), (k * jnp.exp(r - b)).astype(BF16))
        parts.append((((row & -16) == (col & -16)) & causal, p))
        blocks[u] = parts

    log_d, inter, row_max, qn = {}, {}, {}, {}
    for u in units:
        bi, h = u
        bc = cols[bi][:, _lane_block(N_HEADS + h)]
        br = bcum_r[bi][N_HEADS + h:N_HEADS + h + 1, :]
        ir = gate_t[bi][h:h + 1, :]
        log_d[u] = jnp.where(causal, bc - br + ir, -jnp.inf)
        inter[u] = bc + m_ref[bi, h:h + 1, :]
        row_max[u] = jnp.max(log_d[u], axis=-1, keepdims=True)
        q = zm_ref[bi, :, _head_cols(0, h)].astype(F32)
        qn[u] = jnp.sum(q * n_ref[bi, h:h + 1, :], axis=-1, keepdims=True)

    sv, row_sum, w_inter, m_t_all = {}, {}, {}, {}
    for u in units:
        bi, h = u
        m_t = jnp.maximum(inter[u], row_max[u])
        s = qk[u] * jnp.exp(log_d[u] - m_t)
        w_inter[u] = jnp.exp(inter[u] - m_t)
        m_t_all[u] = m_t
        row_sum[u] = jnp.sum(s, axis=-1, keepdims=True)
        sv[u] = _dot(s.astype(BF16), zm_ref[bi, :, _head_cols(2, h)])

    av, sk = {}, {}
    for u in units:
        bi, h = u
        hs = _head_cols(0, h)
        a = None
        for keep, p in blocks[u]:
            term = jnp.where(keep, p, 0.0)
            a = term if a is None else a + term
        v = zh_ref[bi, :, _head_cols(2, h)]
        av[u] = _dot(a.astype(BF16), v)
        k = zh_ref[bi, :, _head_cols(1, h)].astype(F32)
        ke = (k * jnp.exp(b_scr[bi, L - 1:L, hs] - b_scr[bi, :, hs])).astype(BF16)
        sk[u] = _dot_tn(v, ke)

    for u in units:
        bi, h = u
        bc = cols[bi][:, _lane_block(N_HEADS + h)]
        ic = cols[bi][:, _lane_block(h)]
        m_prev = m_ref[bi, h:h + 1, :]
        b_last = bc[L - 1:L, :]
        w_log = b_last - bc + ic
        m_new = jnp.maximum(b_last + m_prev, jnp.max(w_log, axis=0, keepdims=True))
        k = zm_ref[bi, :, _head_cols(1, h)].astype(F32)
        kw = k * jnp.exp(w_log - m_new)
        decay = jnp.exp(b_last + m_prev - m_new)
        c_ref[bi, h] = decay * c_ref[bi, h] + _dot_tn(kw.astype(BF16), zm_ref[bi, :, _head_cols(2, h)])
        n_ref[bi, h:h + 1, :] = decay * n_ref[bi, h:h + 1, :] + jnp.sum(kw, axis=0, keepdims=True)
        m_ref[bi, h:h + 1, :] = m_new

    hh_all, o_all, hh_ms, o_ms = {}, {}, {}, {}
    for u in units:
        bi, h = u
        hs = _head_cols(0, h)
        den = row_sum[u] + w_inter[u] * qn[u]
        num = sv[u] + w_inter[u] * qc[u]
        hh = num * (1.0 / jnp.maximum(jnp.abs(den), jnp.exp(-m_t_all[u])))
        hh = hh * zm_ref[bi, :, _head_cols(3, h)].astype(F32)
        hh_all[u] = hh
        hh_ms[u] = jnp.mean(hh * hh, axis=-1, keepdims=True)

        st_ref[bi, h] = jnp.exp(b_scr[bi, L - 1:L, hs]) * st_ref[bi, h] + sk[u]
        o = oi[u] + av[u]
        o_all[u] = o
        o_ms[u] = jnp.mean(o * o, axis=-1, keepdims=True)

    for u in units:
        bi, h = u
        hs = _head_cols(0, h)
        y_ref[bi, :, hs] = (hh_all[u] * lax.rsqrt(hh_ms[u] + EPS) * gm_ref[:, hs]).astype(BF16)
        o = o_all[u] * lax.rsqrt(o_ms[u] + EPS) * gh_ref[...]
        o = o * zh_ref[bi, :, _head_cols(3, h)].astype(F32)
        y_ref[bi, :, _head_cols(1, h)] = o.astype(BF16)


def _mixer(zm, zh, gate, gate_t, g_mlstm, g_hgrn, batch, n_chunks):
    n = zm.shape[0]
    seq = n // batch
    assert batch % MIX_BATCH == 0
    blk = lambda b, c: (b, c, 0)
    const = lambda b, c: (0, 0)
    sel = (jnp.arange(LANES)[:, None] == jnp.arange(2 * N_HEADS * LANES)[None, :] // LANES).astype(BF16)
    return pl.pallas_call(
        _mixer_kernel,
        grid=(batch // MIX_BATCH, n_chunks),
        in_specs=[
            pl.BlockSpec((MIX_BATCH, CHUNK, 4 * W_MIX), blk),
            pl.BlockSpec((MIX_BATCH, CHUNK, 6 * W_MIX), blk),
            pl.BlockSpec((MIX_BATCH, CHUNK, LANES), blk),
            pl.BlockSpec((MIX_BATCH, 8, CHUNK), lambda b, c: (b, 0, c)),
            pl.BlockSpec((1, W_MIX), const),
            pl.BlockSpec((1, HEAD_DIM), const),
            pl.BlockSpec((LANES, 2 * N_HEADS * LANES), const),
        ],
        out_specs=pl.BlockSpec((MIX_BATCH, CHUNK, 2 * W_MIX), blk),
        out_shape=jax.ShapeDtypeStruct((batch, seq, 2 * W_MIX), BF16),
        scratch_shapes=[
            pltpu.VMEM((MIX_BATCH, N_HEADS, HEAD_DIM, HEAD_DIM), F32),
            pltpu.VMEM((MIX_BATCH, 8, HEAD_DIM), F32),
            pltpu.VMEM((MIX_BATCH, 8, LANES), F32),
            pltpu.VMEM((MIX_BATCH, N_HEADS, HEAD_DIM, HEAD_DIM), F32),
            pltpu.VMEM((MIX_BATCH, CHUNK, W_MIX), F32),
        ],
        compiler_params=_params("parallel", "arbitrary"),
        name="mixer",
    )(zm.reshape(batch, seq, 4 * W_MIX), zh.reshape(batch, seq, 6 * W_MIX), gate.reshape(batch, seq, LANES),
      gate_t.reshape(8, batch, seq).transpose(1, 0, 2), g_mlstm, g_hgrn, sel).reshape(n, 2 * W_MIX)


def _pack_bf16_pair(lo, hi):
    lo_bits = pltpu.bitcast(lo.astype(BF16).astype(F32), U32)
    hi_bits = pltpu.bitcast(hi.astype(BF16).astype(F32), U32)
    return (hi_bits & jnp.uint32(0xFFFF0000)) | (lo_bits >> 16)


def _unpack_bf16_pair(w):
    lo = pltpu.bitcast(w << 16, F32).astype(BF16)
    hi = pltpu.bitcast(w & jnp.uint32(0xFFFF0000), F32).astype(BF16)
    return lo, hi


def _post_mix_kernel(x_ref, y_ref, wo_ref, g_ref, wr_ref, br_ref,
                     x1_ref, hn_ref, route_t_ref, comb_ref, count_ref, run_ref):
    tm = x_ref.shape[0]
    n_r = ROUTER_ROWS

    @pl.when(pl.program_id(0) == 0)
    def _():
        run_ref[...] = jnp.zeros_like(run_ref)

    x1 = x_ref[...] + _dot(y_ref[...], wo_ref[...])
    x1_ref[...] = x1
    hn = _rms(x1, g_ref[...])
    half = D_MODEL // 2
    packed = _pack_bf16_pair(hn[:, :half], hn[:, half:])
    for j in range(PACK_ROWS):
        hn_ref[pl.ds(j, tm, stride=PACK_ROWS), :] = packed[:, j * LANES:(j + 1) * LANES]

    both = _dot(hn.astype(BF16), wr_ref[...])
    logits_c = both[:, :LANES] + both[:, LANES:] + br_ref[...]
    logits = jnp.concatenate([logits_c[c * LANES:(c + 1) * LANES, :].T for c in range(tm // LANES)],
                             axis=1)[0:n_r, :]
    rix = lax.broadcasted_iota(I32, (n_r, tm), 0)
    neg = -jnp.inf
    g_l = jnp.where(rix < N_GROUPS, logits, neg)
    g_max = jnp.max(g_l, axis=0, keepdims=True)
    g_sel = jnp.min(jnp.where(g_l == g_max, rix, n_r), axis=0, keepdims=True)
    g_val = 1.0 / jnp.sum(jnp.exp(g_l - g_max), axis=0, keepdims=True)

    e_row = rix - N_GROUPS
    in_group = (e_row >= g_sel * EXPERTS_PER_GROUP) & (e_row < (g_sel + 1) * EXPERTS_PER_GROUP)
    e_l = jnp.where(in_group, logits, neg)
    v1 = jnp.max(e_l, axis=0, keepdims=True)
    i1 = jnp.min(jnp.where(e_l == v1, rix, n_r), axis=0, keepdims=True)
    e_l2 = jnp.where(rix == i1, neg, e_l)
    v2 = jnp.max(e_l2, axis=0, keepdims=True)
    i2 = jnp.min(jnp.where(e_l2 == v2, rix, n_r), axis=0, keepdims=True)
    t = jnp.exp(v2 - v1)
    c1 = g_val / (1.0 + t)
    c2 = g_val * t / (1.0 + t)

    hot1 = rix == i1
    hot2 = rix == i2
    hot = (hot1 | hot2).astype(F32)
    r_i = lax.broadcasted_iota(I32, (tm, tm), 0)
    c_i = lax.broadcasted_iota(I32, (tm, tm), 1)
    before = _dot(hot.astype(BF16), (r_i < c_i).astype(BF16)) + run_ref[:, 0:1]
    rank1 = jnp.sum(jnp.where(hot1, before, 0.0), axis=0, keepdims=True)
    rank2 = jnp.sum(jnp.where(hot2, before, 0.0), axis=0, keepdims=True)
    run_ref[...] = run_ref[...] + jnp.sum(hot, axis=1, keepdims=True)
    count_ref[...] = run_ref[...]

    r8 = lax.broadcasted_iota(I32, (8, tm), 0)
    out = jnp.where(r8 == 0, (i1 - N_GROUPS).astype(F32), 0.0)
    out = jnp.where(r8 == 1, (i2 - N_GROUPS).astype(F32), out)
    out = jnp.where(r8 == 2, rank1, out)
    out = jnp.where(r8 == 3, rank2, out)
    route_t_ref[...] = out

    r128 = lax.broadcasted_iota(I32, (LANES, tm), 0)
    slab = jnp.where(r128 == 0, c1, jnp.where(r128 == 1, c2, 0.0))
    for c in range(tm // LANES):
        comb_ref[c * LANES:(c + 1) * LANES, :] = slab[:, c * LANES:(c + 1) * LANES].T


def _post_mix(x2, y, w_out, g_ffn, w_r2, b_r, tm):
    n = x2.shape[0]
    n_r = ROUTER_ROWS
    const = lambda i: (0, 0)
    rows = lambda i: (i, 0)
    return pl.pallas_call(
        _post_mix_kernel,
        grid=(n // tm,),
        in_specs=[
            pl.BlockSpec((tm, D_MODEL), rows),
            pl.BlockSpec((tm, 2 * W_MIX), rows),
            pl.BlockSpec((2 * W_MIX, D_MODEL), const),
            pl.BlockSpec((1, D_MODEL), const),
            pl.BlockSpec((D_MODEL, 2 * LANES), const),
            pl.BlockSpec((1, LANES), const),
        ],
        out_specs=[
            pl.BlockSpec((tm, D_MODEL), rows),
            pl.BlockSpec((tm * PACK_ROWS, LANES), rows),
            pl.BlockSpec((8, tm), lambda i: (0, i)),
            pl.BlockSpec((tm, LANES), rows),
            pl.BlockSpec((n_r, LANES), const),
        ],
        out_shape=[
            jax.ShapeDtypeStruct((n, D_MODEL), F32),
            jax.ShapeDtypeStruct((n * PACK_ROWS, LANES), U32),
            jax.ShapeDtypeStruct((8, n), F32),
            jax.ShapeDtypeStruct((n, LANES), F32),
            jax.ShapeDtypeStruct((n_r, LANES), F32),
        ],
        scratch_shapes=[pltpu.VMEM((n_r, LANES), F32)],
        compiler_params=_params("arbitrary"),
        name="post_mix",
    )(x2, y, w_out, g_ffn, w_r2, b_r)


def _dispatch_kernel(zero_ref, d0_ref, d1_ref, hn_ref, wg_ref, wu_ref, wd_ref,
                     xb_ref, wgb_ref, wub_ref, wdb_ref, zbuf, sem):
    ts = hn_ref.shape[0] // PACK_ROWS
    block_rows = EXPERT_BLOCK * PACK_ROWS

    @pl.when(pl.program_id(0) == 0)
    def _():
        zbuf[...] = jnp.zeros_like(zbuf)

        def zero_copy(j):
            row0 = pl.multiple_of(jnp.maximum(zero_ref[j], 0) * block_rows, block_rows)
            return pltpu.make_async_copy(zbuf, xb_ref.at[pl.ds(row0, block_rows), :], sem.at[2])

        def start_zero(j, carry):
            @pl.when(zero_ref[j] >= 0)
            def _():
                zero_copy(j).start()
            return carry

        def wait_zero(j, carry):
            @pl.when(zero_ref[j] >= 0)
            def _():
                zero_copy(j).wait()
            return carry

        lax.fori_loop(0, zero_ref.shape[0], start_zero, 0)
        lax.fori_loop(0, zero_ref.shape[0], wait_zero, 0)

    def start(t, carry):
        src = pl.multiple_of(t * PACK_ROWS, PACK_ROWS)
        for k, d_ref in enumerate((d0_ref, d1_ref)):
            dst = pl.multiple_of(d_ref[0, 0, t] * PACK_ROWS, PACK_ROWS)
            pltpu.make_async_copy(hn_ref.at[pl.ds(src, PACK_ROWS), :], xb_ref.at[pl.ds(dst, PACK_ROWS), :],
                                  sem.at[k]).start(priority=k)
        return carry

    lax.fori_loop(0, ts, start, 0, unroll=DMA_UNROLL)
    wgb_ref[...] = wg_ref[...].astype(BF16)
    wub_ref[...] = wu_ref[...].astype(BF16)
    wdb_ref[...] = wd_ref[...].astype(BF16)

    for k in range(2):
        pltpu.make_async_copy(hn_ref, xb_ref.at[pl.ds(0, ts * PACK_ROWS), :], sem.at[k]).wait()


def _dispatch(zero_blocks, dest0, dest1, hn, w_gate, w_up, w_down, n_rows, ts):
    n = hn.shape[0] // PACK_ROWS
    steps = n // ts
    if steps >= N_EXPERTS:
        parts = steps // N_EXPERTS
        assert steps == parts * N_EXPERTS
        w_spec = lambda rows, cols: pl.BlockSpec((None, rows // parts, cols), lambda i, z: (i // parts, i % parts, 0))
    else:
        per_step = N_EXPERTS // steps
        assert N_EXPERTS == per_step * steps
        w_spec = lambda rows, cols: pl.BlockSpec((per_step, rows, cols), lambda i, z: (i, 0, 0))
    w_specs = [w_spec(D_MODEL, D_EXPERT), w_spec(D_MODEL, D_EXPERT), w_spec(D_EXPERT, D_MODEL)]
    grid_spec = pltpu.PrefetchScalarGridSpec(
        num_scalar_prefetch=1,
        grid=(steps,),
        in_specs=[
            pl.BlockSpec((1, 1, ts), lambda i, z: (i, 0, 0), memory_space=pltpu.SMEM),
            pl.BlockSpec((1, 1, ts), lambda i, z: (i, 0, 0), memory_space=pltpu.SMEM),
            pl.BlockSpec((ts * PACK_ROWS, LANES), lambda i, z: (i, 0)),
        ] + w_specs,
        out_specs=[pl.BlockSpec(memory_space=pl.ANY)] + w_specs,
        scratch_shapes=[pltpu.VMEM((EXPERT_BLOCK * PACK_ROWS, LANES), U32), pltpu.SemaphoreType.DMA((3,))],
    )
    return pl.pallas_call(
        _dispatch_kernel,
        grid_spec=grid_spec,
        out_shape=[jax.ShapeDtypeStruct((n_rows * PACK_ROWS, LANES), U32)]
        + [jax.ShapeDtypeStruct(w.shape, BF16) for w in (w_gate, w_up, w_down)],
        compiler_params=_params("arbitrary"),
        name="dispatch",
    )(zero_blocks, dest0, dest1, hn, w_gate, w_up, w_down)


def _experts_kernel(be_ref, bv_ref, nu_ref, xb_ref, wg_ref, wu_ref, wd_ref, yb_ref):
    valid = bv_ref[pl.program_id(0)]
    half = EXPERT_BLOCK // 2

    def mlp(rows):
        pairs = [_unpack_bf16_pair(xb_ref[pl.ds(j, rows, stride=PACK_ROWS), :]) for j in range(PACK_ROWS)]
        x = jnp.concatenate([lo for lo, _ in pairs] + [hi for _, hi in pairs], axis=1)
        a = (_silu(_dot(x, wg_ref[...])) * _dot(x, wu_ref[...])).astype(BF16)
        y = _dot(a, wd_ref[...])
        half_d = D_MODEL // 2
        packed = _pack_bf16_pair(y[:, :half_d], y[:, half_d:])
        for j in range(PACK_ROWS):
            yb_ref[pl.ds(j, rows, stride=PACK_ROWS), :] = packed[:, j * LANES:(j + 1) * LANES]

    @pl.when(valid > half)
    def _():
        mlp(EXPERT_BLOCK)

    @pl.when((valid > 0) & (valid <= half))
    def _():
        mlp(half)
        yb_ref[half * PACK_ROWS:, :] = jnp.zeros((half * PACK_ROWS, LANES), U32)

    @pl.when(valid == 0)
    def _():
        yb_ref[...] = jnp.zeros_like(yb_ref)


def _experts(block_e, block_valid, n_used, xb, w_gate, w_up, w_down):
    n_rows = xb.shape[0] // PACK_ROWS
    n_blocks = n_rows // EXPERT_BLOCK
    xrow = lambda i, be, bv, nu: (jnp.maximum(jnp.minimum(i, nu[0] - 1), 0), 0)
    wsel = lambda i, be, bv, nu: (be[i], 0, 0)
    grid_spec = pltpu.PrefetchScalarGridSpec(
        num_scalar_prefetch=3,
        grid=(n_blocks,),
        in_specs=[
            pl.BlockSpec((EXPERT_BLOCK * PACK_ROWS, LANES), xrow),
            pl.BlockSpec((None, D_MODEL, D_EXPERT), wsel),
            pl.BlockSpec((None, D_MODEL, D_EXPERT), wsel),
            pl.BlockSpec((None, D_EXPERT, D_MODEL), wsel),
        ],
        out_specs=pl.BlockSpec((EXPERT_BLOCK * PACK_ROWS, LANES), lambda i, be, bv, nu: (i, 0)),
    )
    return pl.pallas_call(
        _experts_kernel,
        grid_spec=grid_spec,
        out_shape=jax.ShapeDtypeStruct((n_rows * PACK_ROWS, LANES), U32),
        compiler_params=_params("arbitrary"),
        name="experts",
    )(block_e, block_valid, n_used, xb, w_gate, w_up, w_down)


def _combine_kernel(d0_ref, d1_ref, d0a_ref, d1a_ref, d0b_ref, d1b_ref, x1_ref, p_ref, comb_ref, gpl_ref, wplg_ref,
                    wplp_ref, gfin_ref, yb_ref, out_ref, gbuf, x3_ref, sem):
    tf = x1_ref.shape[0]
    i = pl.program_id(0)
    n_steps = pl.num_programs(0)
    slot = lax.rem(i, GATHER_BUFS)
    slot_a = lax.rem(i + 1, GATHER_BUFS)
    slot_b = lax.rem(i + 2, GATHER_BUFS)

    def start_gather(d_refs, s, t):
        row = pl.multiple_of(t * PACK_ROWS, PACK_ROWS)
        for k, d_ref in enumerate(d_refs):
            src = pl.multiple_of(d_ref[0, 0, t] * PACK_ROWS, PACK_ROWS)
            pltpu.make_async_copy(yb_ref.at[pl.ds(src, PACK_ROWS), :], gbuf.at[s, k, pl.ds(row, PACK_ROWS), :],
                                  sem.at[s, k]).start(priority=k)

    def wait_gathers(s):
        for k in range(2):
            pltpu.make_async_copy(yb_ref.at[pl.ds(0, tf * PACK_ROWS), :], gbuf.at[s, k], sem.at[s, k]).wait()

    @pl.when(i == 0)
    def _():
        lax.fori_loop(0, tf, lambda t, c: (start_gather((d0_ref, d1_ref), 0, t), c)[1], 0, unroll=DMA_UNROLL)
        lax.fori_loop(0, tf, lambda t, c: (start_gather((d0a_ref, d1a_ref), 1, t), c)[1], 0, unroll=DMA_UNROLL)

    wait_gathers(slot)

    def gathered(k):
        words = [gbuf[slot, k, pl.ds(j, tf, stride=PACK_ROWS), :] for j in range(PACK_ROWS)]
        return jnp.concatenate([pltpu.bitcast(w << 16, F32) for w in words]
                               + [pltpu.bitcast(w & jnp.uint32(0xFFFF0000), F32) for w in words], axis=1)

    comb = comb_ref[...]
    y = comb[:, 0:1] * gathered(0) + comb[:, 1:2] * gathered(1)
    x2 = x1_ref[...] + y
    hb = _rms(x2, gpl_ref[...]).astype(BF16)
    pb = p_ref[...].astype(BF16)
    n_col = D_MODEL // COMBINE_COLS
    for c in range(n_col):
        for t in range(c * tf // n_col, (c + 1) * tf // n_col):
            start_gather((d0b_ref, d1b_ref), slot_b, t)
        cs = slice(c * COMBINE_COLS, (c + 1) * COMBINE_COLS)
        gate = jax.nn.sigmoid(_dot(hb, wplg_ref[:, cs]))
        x3_ref[:, cs] = x2[:, cs] + gate * _dot(pb, wplp_ref[:, cs])
    x3 = x3_ref[...]
    out_ref[...] = _rms(x3, gfin_ref[...])

    @pl.when(i == n_steps - 1)
    def _():
        wait_gathers(slot_a)
        wait_gathers(slot_b)


def _combine(dest0, dest1, x1, p2, comb, g_pl, w_plg, w_plp, g_final, yb, tf):
    n = x1.shape[0]
    n_steps = n // tf
    assert n_steps >= GATHER_BUFS - 1
    const = lambda i: (0, 0)
    rows = lambda i: (i, 0)
    ahead = lambda a: (lambda i: (jnp.minimum(i + a, n_steps - 1), 0, 0))
    dest_specs = [pl.BlockSpec((1, 1, tf), ahead(a), memory_space=pltpu.SMEM) for a in range(GATHER_BUFS)
                  for _ in range(2)]
    return pl.pallas_call(
        _combine_kernel,
        grid=(n_steps,),
        in_specs=dest_specs + [
            pl.BlockSpec((tf, D_MODEL), rows),
            pl.BlockSpec((tf, PLE_DIM), rows),
            pl.BlockSpec((tf, LANES), rows),
            pl.BlockSpec((1, D_MODEL), const),
            pl.BlockSpec((D_MODEL, D_MODEL), const),
            pl.BlockSpec((PLE_DIM, D_MODEL), const),
            pl.BlockSpec((1, D_MODEL), const),
            pl.BlockSpec(memory_space=pl.ANY),
        ],
        out_specs=pl.BlockSpec((tf, D_MODEL), rows),
        out_shape=jax.ShapeDtypeStruct((n, D_MODEL), F32),
        scratch_shapes=[pltpu.VMEM((GATHER_BUFS, 2, tf * PACK_ROWS, LANES), U32), pltpu.VMEM((tf, D_MODEL), F32),
                        pltpu.SemaphoreType.DMA((GATHER_BUFS, 2))],
        compiler_params=_params("arbitrary"),
        name="combine",
    )(dest0, dest1, dest0, dest1, dest0, dest1, x1, p2, comb, g_pl, w_plg, w_plp, g_final, yb)


def _layer(x2, p2, batch, seq, g_mix, w_in, b_mgate, conv_qk, g_mlstm, lb, g_hgrn, w_out, g_ffn,
           w_rg, b_rg, w_re, b_re, w_e_gate, w_e_up, w_e_down, g_pl, w_pl_gate, w_pl_proj, g_out):
    n = x2.shape[0]
    n_chunks = seq // CHUNK
    m_cols = 4 * W_MIX
    n_gate = 2 * N_HEADS

    w_in_b = w_in.astype(BF16)
    w_m = w_in_b[:, :m_cols]
    w_gcols = w_in_b[:, m_cols:m_cols + n_gate]
    w_h = w_in_b[:, m_cols + n_gate:]
    w_g = jnp.pad(w_gcols, ((0, 0), (0, LANES - n_gate)))
    b_g = jnp.pad(b_mgate.astype(F32), (0, LANES - n_gate))[None, :]
    zm, zh, gate, gate_t = _in_proj(x2, g_mix[None, :], w_m, w_h, w_g, w_gcols.T, b_g,
                                    b_mgate.astype(F32)[:, None], conv_qk, lb[None, :], seq, tm=min(1024, seq))

    y = _mixer(zm, zh, gate, gate_t, g_mlstm[None, :], g_hgrn[None, :], batch, n_chunks)

    n_logit = N_GROUPS + N_EXPERTS
    w_r = jnp.pad(jnp.concatenate([w_rg, w_re], axis=1), ((0, 0), (0, LANES - n_logit)))
    wr_hi = w_r.astype(BF16)
    w_r2 = jnp.concatenate([wr_hi, (w_r - wr_hi.astype(F32)).astype(BF16)], axis=1)
    b_r = jnp.pad(jnp.concatenate([b_rg, b_re]), (0, LANES - n_logit))[None, :]
    x1, hn, route_t, comb, counts = _post_mix(x2, y, w_out.astype(BF16), g_ffn[None, :], w_r2, b_r,
                                              tm=min(512, n))

    counts = counts[N_GROUPS:n_logit, 0].astype(I32)
    padded = (counts + EXPERT_BLOCK - 1) // EXPERT_BLOCK * EXPERT_BLOCK
    pend = jnp.cumsum(padded)
    pstart = pend - padded
    n_blocks = (2 * n) // EXPERT_BLOCK + N_EXPERTS
    n_rows = n_blocks * EXPERT_BLOCK
    n_used = (pend[-1] // EXPERT_BLOCK).astype(I32)
    block_ids = jnp.arange(n_blocks, dtype=I32)
    block_row = jnp.minimum(block_ids, n_used - 1) * EXPERT_BLOCK
    block_e = jnp.sum((pend[None, :] <= block_row[:, None]).astype(I32), axis=1)
    token_end = jnp.sum(jnp.where(block_e[:, None] == jnp.arange(N_EXPERTS, dtype=I32), pstart + counts, 0), axis=1)
    block_valid = jnp.where(block_ids < n_used, jnp.clip(token_end - block_row, 0, EXPERT_BLOCK), 0).astype(I32)
    expert_id = route_t[0:2].astype(I32)
    hot = expert_id[:, :, None] == jnp.arange(N_EXPERTS, dtype=I32)
    dest = jnp.sum(jnp.where(hot, pstart, 0), axis=-1) + route_t[2:4].astype(I32)

    ts = min(1024, n)
    last_block = jnp.where(counts % EXPERT_BLOCK != 0, pend // EXPERT_BLOCK - 1, -1)
    tail_block = n_used + jnp.arange(N_EXPERTS, dtype=I32)
    zero_blocks = jnp.concatenate([last_block, jnp.where(tail_block < n_blocks, tail_block, -1)]).astype(I32)
    xb, wg_b, wu_b, wd_b = _dispatch(zero_blocks, dest[0].reshape(n // ts, 1, ts), dest[1].reshape(n // ts, 1, ts),
                                     hn, w_e_gate, w_e_up, w_e_down, n_rows, ts)
    yb = _experts(block_e, block_valid, n_used[None], xb, wg_b, wu_b, wd_b)
    tf = min(512, n)
    return _combine(dest[0].reshape(n // tf, 1, tf), dest[1].reshape(n // tf, 1, tf), x1, p2, comb, g_pl[None, :],
                    w_pl_gate.astype(BF16), w_pl_proj.astype(BF16), g_out[None, :], yb, tf)


def kernel(x, p, g_mix, w_in, b_mgate, conv_qk, g_mlstm, hg_lb, g_hgrn, w_out, g_ffn, w_rg, b_rg, w_re, b_re,
           w_e_gate, w_e_up, w_e_down, g_pl, w_pl_gate, w_pl_proj, g_final):
    batch, seq, d = x.shape
    depth = p.shape[0]
    assert depth == 1, "the fused final norm assumes a single layer"
    lower_bounds = jnp.cumsum(jax.nn.softmax(hg_lb.astype(F32), axis=0), axis=0)
    i = 0
    out = _layer(x.reshape(batch * seq, d), p[i].reshape(batch * seq, PLE_DIM), batch, seq,
                 g_mix[i], w_in[i], b_mgate[i], conv_qk[i], g_mlstm[i], lower_bounds[i], g_hgrn[i], w_out[i],
                 g_ffn[i], w_rg[i], b_rg[i], w_re[i], b_re[i], w_e_gate[i], w_e_up[i], w_e_down[i],
                 g_pl[i], w_pl_gate[i], w_pl_proj[i], g_final)
    return out.reshape(batch, seq, d)
```

```python
import functools

import jax
import jax.numpy as jnp
from jax import lax
from jax.experimental import pallas as pl
from jax.experimental.pallas import tpu as pltpu

F32 = jnp.float32
BF16 = jnp.bfloat16
I32 = jnp.int32
U32 = jnp.uint32
EPS = 1e-6
LOG2E = 1.4426950408889634

LANES = 128
D_MODEL = 1024
W_MIX = 512
N_HEADS = 4
HEAD_DIM = 128
N_GROUPS = 4
EXPERTS_PER_GROUP = 8
N_EXPERTS = N_GROUPS * EXPERTS_PER_GROUP
D_EXPERT = 512
PLE_DIM = 256
CONV_WIDTH = 4
CHUNK = 128
EXPERT_BLOCK = 512
PACK_ROWS = D_MODEL // 2 // LANES
ROUTER_ROWS = 48
DMA_UNROLL = 4
COMBINE_COLS = 256
MIX_BATCH = 4
IN_PROJ_COLS = 256
IN_PROJ_AHEAD = 1
GATHER_BUFS = 3
VMEM_LIMIT = 56 * 1024 * 1024


def _dot(a, b):
    return jnp.dot(a, b, preferred_element_type=F32)


def _dot_nt(a, b):
    return lax.dot_general(a, b, (((1,), (1,)), ((), ())), preferred_element_type=F32)


def _dot_tn(a, b):
    return lax.dot_general(a, b, (((0,), (0,)), ((), ())), preferred_element_type=F32)


def _rms(u, g):
    return u * lax.rsqrt(jnp.mean(u * u, axis=-1, keepdims=True) + EPS) * g


def _silu(u):
    return u * jax.nn.sigmoid(u)


def _log_sigmoid(u):
    return jnp.minimum(u, 0.0) - jnp.log1p(jnp.exp(-jnp.abs(u)))


def _split_hi_lo(u):
    hi = u.astype(BF16)
    return hi, (u - hi.astype(F32)).astype(BF16)


def _params(*sem):
    return pltpu.CompilerParams(dimension_semantics=sem, vmem_limit_bytes=VMEM_LIMIT)


def _in_proj_kernel(tiles_per_seq, x_ref, g_ref, wm_ref, wh_ref, wg_ref, bg_ref, conv_ref, lb_ref,
                    zm_ref, zh_ref, gate_ref, gate_t_ref, cbuf):
    tm = x_ref.shape[0]
    W = W_MIX
    CB = IN_PROJ_COLS

    @pl.when(pl.program_id(0) % tiles_per_seq == 0)
    def _():
        cbuf[0:8, :] = jnp.zeros((8, 2 * W), F32)

    h = _rms(x_ref[...], g_ref[...]).astype(BF16)

    def conv_finish(col0, scale):
        def finish(z):
            cs = slice(col0, col0 + CB)
            cbuf[8:8 + tm, cs] = z
            acc = z * conv_ref[CONV_WIDTH - 1:CONV_WIDTH, cs]
            for j in range(CONV_WIDTH - 1):
                acc = acc + cbuf[5 + j:5 + j + tm, cs] * conv_ref[j:j + 1, cs]
            cbuf[0:8, cs] = cbuf[tm:tm + 8, cs]
            zm_ref[:, cs] = (_silu(acc) * scale).astype(BF16)
        return finish

    def store(ref, col0, fn):
        def finish(z):
            ref[:, col0:col0 + CB] = fn(z).astype(BF16)
        return finish

    def forget_finish(col0):
        def finish(z):
            lb = lb_ref[:, col0:col0 + CB]
            zh_ref[:, W + col0:W + col0 + CB] = ((1.0 - lb) * jax.nn.sigmoid(-z)).astype(BF16)
            lf_hi, lf_lo = _split_hi_lo(jnp.log2(lb + (1.0 - lb) * jax.nn.sigmoid(z)))
            zh_ref[:, 4 * W + col0:4 * W + col0 + CB] = lf_hi
            zh_ref[:, 5 * W + col0:5 * W + col0 + CB] = lf_lo
        return finish

    ident = lambda z: z
    plan = []
    for half in range(W // CB):
        c0 = half * CB
        plan += [
            (wm_ref, c0, conv_finish(c0, 1.0)),
            (wh_ref, 2 * W + c0, store(zh_ref, 2 * W + c0, ident)),
            (wm_ref, W + c0, conv_finish(W + c0, HEAD_DIM ** -0.5)),
            (wm_ref, 2 * W + c0, store(zm_ref, 2 * W + c0, ident)),
            (wh_ref, W + c0, forget_finish(c0)),
            (wm_ref, 3 * W + c0, store(zm_ref, 3 * W + c0, jax.nn.sigmoid)),
            (wh_ref, c0, store(zh_ref, c0, _silu)),
            (wh_ref, 3 * W + c0, store(zh_ref, 3 * W + c0, _silu)),
        ]

    project = lambda i: _dot(h, plan[i][0][:, plan[i][1]:plan[i][1] + CB])
    zs = [project(i) for i in range(IN_PROJ_AHEAD)]
    for i in range(len(plan)):
        if i + IN_PROJ_AHEAD < len(plan):
            zs.append(project(i + IN_PROJ_AHEAD))
        plan[i][2](zs[i])

    gate = _dot(h, wg_ref[...]) + bg_ref[...]
    gate_ref[...] = gate
    gate_t_ref[...] = jnp.concatenate([gate[c * LANES:(c + 1) * LANES, :].T for c in range(tm // LANES)],
                                      axis=1)[0:2 * N_HEADS, :]


def _in_proj(x2, g_mix, w_m, w_h, w_g, b_g, conv_w, lb, seq, tm):
    n = x2.shape[0]
    assert seq % tm == 0
    const = lambda i: (0, 0)
    rows = lambda i: (i, 0)
    return pl.pallas_call(
        functools.partial(_in_proj_kernel, seq // tm),
        grid=(n // tm,),
        in_specs=[
            pl.BlockSpec((tm, D_MODEL), rows),
            pl.BlockSpec((1, D_MODEL), const),
            pl.BlockSpec((D_MODEL, 4 * W_MIX), const),
            pl.BlockSpec((D_MODEL, 4 * W_MIX), const),
            pl.BlockSpec((D_MODEL, LANES), const),
            pl.BlockSpec((1, LANES), const),
            pl.BlockSpec((CONV_WIDTH, 2 * W_MIX), const),
            pl.BlockSpec((1, W_MIX), const),
        ],
        out_specs=[
            pl.BlockSpec((tm, 4 * W_MIX), rows),
            pl.BlockSpec((tm, 6 * W_MIX), rows),
            pl.BlockSpec((tm, LANES), rows),
            pl.BlockSpec((8, tm), lambda i: (0, i)),
        ],
        out_shape=[
            jax.ShapeDtypeStruct((n, 4 * W_MIX), BF16),
            jax.ShapeDtypeStruct((n, 6 * W_MIX), BF16),
            jax.ShapeDtypeStruct((n, LANES), F32),
            jax.ShapeDtypeStruct((8, n), F32),
        ],
        scratch_shapes=[pltpu.VMEM((tm + 8, 2 * W_MIX), F32)],
        compiler_params=_params("arbitrary"),
        name="in_proj",
    )(x2, g_mix, w_m, w_h, w_g, b_g, conv_w, lb)


def _rows_bcast(ref, rows, span, hs):
    return jnp.concatenate([jnp.broadcast_to(ref[r:r + 1, hs], (span, HEAD_DIM)) for r in rows], axis=0)


def _head_cols(base, h):
    return slice(base * W_MIX + h * HEAD_DIM, base * W_MIX + (h + 1) * HEAD_DIM)


def _lane_block(j):
    return slice(j * LANES, (j + 1) * LANES)


def _mixer_kernel(zm_ref, zh_ref, gate_ref, gate_t_ref, gm_ref, gh_ref, sel_ref, y_ref,
                  c_ref, n_ref, m_ref, st_ref, b_scr):
    L = CHUNK

    @pl.when(pl.program_id(1) == 0)
    def _():
        c_ref[...] = jnp.zeros_like(c_ref)
        n_ref[...] = jnp.zeros_like(n_ref)
        m_ref[...] = jnp.zeros_like(m_ref)
        st_ref[...] = jnp.zeros_like(st_ref)

    units = [(bi, h) for bi in range(MIX_BATCH) for h in range(N_HEADS)]
    row = lax.broadcasted_iota(I32, (L, L), 0)
    col = lax.broadcasted_iota(I32, (L, L), 1)
    causal = col <= row
    lower, upper = causal.astype(BF16), (row <= col).astype(BF16)

    gate_t, cols, bcum_r = [], [], []
    for bi in range(MIX_BATCH):
        g = gate_ref[bi]
        g_t = gate_t_ref[bi]
        lf_c, lf_r = _split_hi_lo(_log_sigmoid(g) * LOG2E), _split_hi_lo(_log_sigmoid(g_t) * LOG2E)
        g = g * LOG2E
        gate_t.append(g_t * LOG2E)
        bcum_c = _dot(lower, lf_c[0]) + _dot(lower, lf_c[1])
        bcum_r.append(_dot(lf_r[0], upper) + _dot(lf_r[1], upper))
        mixed = _split_hi_lo(jnp.where(col < N_HEADS, g, bcum_c))
        cols.append(_dot(mixed[0], sel_ref[...]) + _dot(mixed[1], sel_ref[...]))
        b_scr[bi] = (_dot(lower, zh_ref[bi, :, 4 * W_MIX:5 * W_MIX])
                     + _dot(lower, zh_ref[bi, :, 5 * W_MIX:6 * W_MIX]))

    qk, qc = {}, {}
    for u in units:
        bi, h = u
        qb = zm_ref[bi, :, _head_cols(0, h)]
        qk[u] = _dot_nt(qb, zm_ref[bi, :, _head_cols(1, h)])
        qc[u] = _dot(qb, c_ref[bi, h].astype(BF16))

    oi, blocks = {}, {}
    for u in units:
        bi, h = u
        hs = _head_cols(0, h)
        b = b_scr[bi, :, hs]
        q = zh_ref[bi, :, hs].astype(F32)
        k = zh_ref[bi, :, _head_cols(1, h)].astype(F32)
        oi[u] = _dot_nt((q * jnp.exp2(b)).astype(BF16), st_ref[bi, h].astype(BF16))
        parts = []
        span = L // 2
        while span >= 16:
            mids = range(span, L, 2 * span)
            r = _rows_bcast(b_scr.at[bi], [m for m in mids for _ in (0, 1)], span, hs)
            low = (row & span) != 0
            z = (jnp.where(low, q, k) * jnp.exp2(jnp.where(low, b - r, r - b))).astype(BF16)
            keep = ((row & -(2 * span)) == (col & -(2 * span))) & low & ((col & span) == 0)
            parts.append((keep, _dot_nt(z, z)))
            span //= 2
        r = _rows_bcast(b_scr.at[bi], range(0, L, 16), 16, hs)
        p = _dot_nt((q * jnp.exp2(b - r)).astype(BF16), (k * jnp.exp2(r - b)).astype(BF16))
        parts.append((((row & -16) == (col & -16)) & causal, p))
        blocks[u] = parts

    log_d, inter, row_max, qn = {}, {}, {}, {}
    for u in units:
        bi, h = u
        bc = cols[bi][:, _lane_block(N_HEADS + h)]
        br = bcum_r[bi][N_HEADS + h:N_HEADS + h + 1, :]
        ir = gate_t[bi][h:h + 1, :]
        log_d[u] = jnp.where(causal, bc - br + ir, -jnp.inf)
        inter[u] = bc + m_ref[bi, h:h + 1, :]
        row_max[u] = jnp.max(log_d[u], axis=-1, keepdims=True)
        q = zm_ref[bi, :, _head_cols(0, h)].astype(F32)
        qn[u] = jnp.sum(q * n_ref[bi, h:h + 1, :], axis=-1, keepdims=True)

    sv, row_sum, w_inter, m_t_all = {}, {}, {}, {}
    for u in units:
        bi, h = u
        m_t = jnp.maximum(inter[u], row_max[u])
        s = qk[u] * jnp.exp2(log_d[u] - m_t)
        w_inter[u] = jnp.exp2(inter[u] - m_t)
        m_t_all[u] = m_t
        row_sum[u] = jnp.sum(s, axis=-1, keepdims=True)
        sv[u] = _dot(s.astype(BF16), zm_ref[bi, :, _head_cols(2, h)])

    av, sk = {}, {}
    for u in units:
        bi, h = u
        hs = _head_cols(0, h)
        a = None
        for keep, p in blocks[u]:
            term = jnp.where(keep, p, 0.0)
            a = term if a is None else a + term
        v = zh_ref[bi, :, _head_cols(2, h)]
        av[u] = _dot(a.astype(BF16), v)
        k = zh_ref[bi, :, _head_cols(1, h)].astype(F32)
        ke = (k * jnp.exp2(b_scr[bi, L - 1:L, hs] - b_scr[bi, :, hs])).astype(BF16)
        sk[u] = _dot_tn(v, ke)

    for u in units:
        bi, h = u
        bc = cols[bi][:, _lane_block(N_HEADS + h)]
        ic = cols[bi][:, _lane_block(h)]
        m_prev = m_ref[bi, h:h + 1, :]
        b_last = bc[L - 1:L, :]
        w_log = b_last - bc + ic
        m_new = jnp.maximum(b_last + m_prev, jnp.max(w_log, axis=0, keepdims=True))
        k = zm_ref[bi, :, _head_cols(1, h)].astype(F32)
        kw = k * jnp.exp2(w_log - m_new)
        decay = jnp.exp2(b_last + m_prev - m_new)
        c_ref[bi, h] = decay * c_ref[bi, h] + _dot_tn(kw.astype(BF16), zm_ref[bi, :, _head_cols(2, h)])
        n_ref[bi, h:h + 1, :] = decay * n_ref[bi, h:h + 1, :] + jnp.sum(kw, axis=0, keepdims=True)
        m_ref[bi, h:h + 1, :] = m_new

    hh_all, o_all, hh_ms, o_ms = {}, {}, {}, {}
    for u in units:
        bi, h = u
        hs = _head_cols(0, h)
        den = row_sum[u] + w_inter[u] * qn[u]
        num = sv[u] + w_inter[u] * qc[u]
        hh = num * (1.0 / jnp.maximum(jnp.abs(den), jnp.exp2(-m_t_all[u])))
        hh = hh * zm_ref[bi, :, _head_cols(3, h)].astype(F32)
        hh_all[u] = hh
        hh_ms[u] = jnp.mean(hh * hh, axis=-1, keepdims=True)

        st_ref[bi, h] = jnp.exp2(b_scr[bi, L - 1:L, hs]) * st_ref[bi, h] + sk[u]
        o = oi[u] + av[u]
        o_all[u] = o
        o_ms[u] = jnp.mean(o * o, axis=-1, keepdims=True)

    for u in units:
        bi, h = u
        hs = _head_cols(0, h)
        y_ref[bi, :, hs] = (hh_all[u] * lax.rsqrt(hh_ms[u] + EPS) * gm_ref[:, hs]).astype(BF16)
        o = o_all[u] * lax.rsqrt(o_ms[u] + EPS) * gh_ref[...]
        o = o * zh_ref[bi, :, _head_cols(3, h)].astype(F32)
        y_ref[bi, :, _head_cols(1, h)] = o.astype(BF16)


def _mixer(zm, zh, gate, gate_t, g_mlstm, g_hgrn, batch, n_chunks):
    n = zm.shape[0]
    seq = n // batch
    assert batch % MIX_BATCH == 0
    blk = lambda b, c: (b, c, 0)
    const = lambda b, c: (0, 0)
    sel = (jnp.arange(LANES)[:, None] == jnp.arange(2 * N_HEADS * LANES)[None, :] // LANES).astype(BF16)
    return pl.pallas_call(
        _mixer_kernel,
        grid=(batch // MIX_BATCH, n_chunks),
        in_specs=[
            pl.BlockSpec((MIX_BATCH, CHUNK, 4 * W_MIX), blk),
            pl.BlockSpec((MIX_BATCH, CHUNK, 6 * W_MIX), blk),
            pl.BlockSpec((MIX_BATCH, CHUNK, LANES), blk),
            pl.BlockSpec((MIX_BATCH, 8, CHUNK), lambda b, c: (b, 0, c)),
            pl.BlockSpec((1, W_MIX), const),
            pl.BlockSpec((1, HEAD_DIM), const),
            pl.BlockSpec((LANES, 2 * N_HEADS * LANES), const),
        ],
        out_specs=pl.BlockSpec((MIX_BATCH, CHUNK, 2 * W_MIX), blk),
        out_shape=jax.ShapeDtypeStruct((batch, seq, 2 * W_MIX), BF16),
        scratch_shapes=[
            pltpu.VMEM((MIX_BATCH, N_HEADS, HEAD_DIM, HEAD_DIM), F32),
            pltpu.VMEM((MIX_BATCH, 8, HEAD_DIM), F32),
            pltpu.VMEM((MIX_BATCH, 8, LANES), F32),
            pltpu.VMEM((MIX_BATCH, N_HEADS, HEAD_DIM, HEAD_DIM), F32),
            pltpu.VMEM((MIX_BATCH, CHUNK, W_MIX), F32),
        ],
        compiler_params=_params("parallel", "arbitrary"),
        name="mixer",
    )(zm.reshape(batch, seq, 4 * W_MIX), zh.reshape(batch, seq, 6 * W_MIX), gate.reshape(batch, seq, LANES),
      gate_t.reshape(8, batch, seq).transpose(1, 0, 2), g_mlstm, g_hgrn, sel).reshape(n, 2 * W_MIX)


def _pack_bf16_pair(lo, hi):
    lo_bits = pltpu.bitcast(lo.astype(BF16).astype(F32), U32)
    hi_bits = pltpu.bitcast(hi.astype(BF16).astype(F32), U32)
    return (hi_bits & jnp.uint32(0xFFFF0000)) | (lo_bits >> 16)


def _unpack_bf16_pair(w):
    lo = pltpu.bitcast(w << 16, F32).astype(BF16)
    hi = pltpu.bitcast(w & jnp.uint32(0xFFFF0000), F32).astype(BF16)
    return lo, hi


def _post_mix_kernel(x_ref, y_ref, wo_ref, g_ref, wr_ref, br_ref,
                     x1_ref, hn_ref, route_t_ref, comb_ref, count_ref, run_ref):
    tm = x_ref.shape[0]
    n_r = ROUTER_ROWS

    @pl.when(pl.program_id(0) == 0)
    def _():
        run_ref[...] = jnp.zeros_like(run_ref)

    x1 = x_ref[...] + _dot(y_ref[...], wo_ref[...])
    x1_ref[...] = x1
    hn = _rms(x1, g_ref[...])
    half = D_MODEL // 2
    packed = _pack_bf16_pair(hn[:, :half], hn[:, half:])
    for j in range(PACK_ROWS):
        hn_ref[pl.ds(j, tm, stride=PACK_ROWS), :] = packed[:, j * LANES:(j + 1) * LANES]

    both = _dot(hn.astype(BF16), wr_ref[...])
    logits_c = both[:, :LANES] + both[:, LANES:] + br_ref[...]
    logits = jnp.concatenate([logits_c[c * LANES:(c + 1) * LANES, :].T for c in range(tm // LANES)],
                             axis=1)[0:n_r, :]
    rix = lax.broadcasted_iota(I32, (n_r, tm), 0)
    neg = -jnp.inf
    g_l = jnp.where(rix < N_GROUPS, logits, neg)
    g_max = jnp.max(g_l, axis=0, keepdims=True)
    g_sel = jnp.min(jnp.where(g_l == g_max, rix, n_r), axis=0, keepdims=True)
    g_val = 1.0 / jnp.sum(jnp.exp(g_l - g_max), axis=0, keepdims=True)

    e_row = rix - N_GROUPS
    in_group = (e_row >= g_sel * EXPERTS_PER_GROUP) & (e_row < (g_sel + 1) * EXPERTS_PER_GROUP)
    e_l = jnp.where(in_group, logits, neg)
    v1 = jnp.max(e_l, axis=0, keepdims=True)
    i1 = jnp.min(jnp.where(e_l == v1, rix, n_r), axis=0, keepdims=True)
    e_l2 = jnp.where(rix == i1, neg, e_l)
    v2 = jnp.max(e_l2, axis=0, keepdims=True)
    i2 = jnp.min(jnp.where(e_l2 == v2, rix, n_r), axis=0, keepdims=True)
    t = jnp.exp(v2 - v1)
    c1 = g_val / (1.0 + t)
    c2 = g_val * t / (1.0 + t)

    hot1 = rix == i1
    hot2 = rix == i2
    hot = (hot1 | hot2).astype(F32)
    r_i = lax.broadcasted_iota(I32, (tm, tm), 0)
    c_i = lax.broadcasted_iota(I32, (tm, tm), 1)
    before = _dot(hot.astype(BF16), (r_i < c_i).astype(BF16)) + run_ref[:, 0:1]
    rank1 = jnp.sum(jnp.where(hot1, before, 0.0), axis=0, keepdims=True)
    rank2 = jnp.sum(jnp.where(hot2, before, 0.0), axis=0, keepdims=True)
    run_ref[...] = run_ref[...] + jnp.sum(hot, axis=1, keepdims=True)
    count_ref[...] = run_ref[...]

    r8 = lax.broadcasted_iota(I32, (8, tm), 0)
    out = jnp.where(r8 == 0, (i1 - N_GROUPS).astype(F32), 0.0)
    out = jnp.where(r8 == 1, (i2 - N_GROUPS).astype(F32), out)
    out = jnp.where(r8 == 2, rank1, out)
    out = jnp.where(r8 == 3, rank2, out)
    route_t_ref[...] = out

    r128 = lax.broadcasted_iota(I32, (LANES, tm), 0)
    slab = jnp.where(r128 == 0, c1, jnp.where(r128 == 1, c2, 0.0))
    for c in range(tm // LANES):
        comb_ref[c * LANES:(c + 1) * LANES, :] = slab[:, c * LANES:(c + 1) * LANES].T


def _post_mix(x2, y, w_out, g_ffn, w_r2, b_r, tm):
    n = x2.shape[0]
    n_r = ROUTER_ROWS
    const = lambda i: (0, 0)
    rows = lambda i: (i, 0)
    return pl.pallas_call(
        _post_mix_kernel,
        grid=(n // tm,),
        in_specs=[
            pl.BlockSpec((tm, D_MODEL), rows),
            pl.BlockSpec((tm, 2 * W_MIX), rows),
            pl.BlockSpec((2 * W_MIX, D_MODEL), const),
            pl.BlockSpec((1, D_MODEL), const),
            pl.BlockSpec((D_MODEL, 2 * LANES), const),
            pl.BlockSpec((1, LANES), const),
        ],
        out_specs=[
            pl.BlockSpec((tm, D_MODEL), rows),
            pl.BlockSpec((tm * PACK_ROWS, LANES), rows),
            pl.BlockSpec((8, tm), lambda i: (0, i)),
            pl.BlockSpec((tm, LANES), rows),
            pl.BlockSpec((n_r, LANES), const),
        ],
        out_shape=[
            jax.ShapeDtypeStruct((n, D_MODEL), F32),
            jax.ShapeDtypeStruct((n * PACK_ROWS, LANES), U32),
            jax.ShapeDtypeStruct((8, n), F32),
            jax.ShapeDtypeStruct((n, LANES), F32),
            jax.ShapeDtypeStruct((n_r, LANES), F32),
        ],
        scratch_shapes=[pltpu.VMEM((n_r, LANES), F32)],
        compiler_params=_params("arbitrary"),
        name="post_mix",
    )(x2, y, w_out, g_ffn, w_r2, b_r)


def _dispatch_kernel(zero_ref, d0_ref, d1_ref, hn_ref, wg_ref, wu_ref, wd_ref,
                     xb_ref, wgb_ref, wub_ref, wdb_ref, zbuf, sem):
    ts = hn_ref.shape[0] // PACK_ROWS
    block_rows = EXPERT_BLOCK * PACK_ROWS

    @pl.when(pl.program_id(0) == 0)
    def _():
        zbuf[...] = jnp.zeros_like(zbuf)

        def zero_copy(j):
            row0 = pl.multiple_of(jnp.maximum(zero_ref[j], 0) * block_rows, block_rows)
            return pltpu.make_async_copy(zbuf, xb_ref.at[pl.ds(row0, block_rows), :], sem.at[2])

        def start_zero(j, carry):
            @pl.when(zero_ref[j] >= 0)
            def _():
                zero_copy(j).start()
            return carry

        def wait_zero(j, carry):
            @pl.when(zero_ref[j] >= 0)
            def _():
                zero_copy(j).wait()
            return carry

        lax.fori_loop(0, zero_ref.shape[0], start_zero, 0)
        lax.fori_loop(0, zero_ref.shape[0], wait_zero, 0)

    def start(t, carry):
        src = pl.multiple_of(t * PACK_ROWS, PACK_ROWS)
        for k, d_ref in enumerate((d0_ref, d1_ref)):
            dst = pl.multiple_of(d_ref[0, 0, t] * PACK_ROWS, PACK_ROWS)
            pltpu.make_async_copy(hn_ref.at[pl.ds(src, PACK_ROWS), :], xb_ref.at[pl.ds(dst, PACK_ROWS), :],
                                  sem.at[k]).start(priority=k)
        return carry

    lax.fori_loop(0, ts, start, 0, unroll=DMA_UNROLL)
    wgb_ref[...] = wg_ref[...].astype(BF16)
    wub_ref[...] = wu_ref[...].astype(BF16)
    wdb_ref[...] = wd_ref[...].astype(BF16)

    for k in range(2):
        pltpu.make_async_copy(hn_ref, xb_ref.at[pl.ds(0, ts * PACK_ROWS), :], sem.at[k]).wait()


def _dispatch(zero_blocks, dest0, dest1, hn, w_gate, w_up, w_down, n_rows, ts):
    n = hn.shape[0] // PACK_ROWS
    steps = n // ts
    if steps >= N_EXPERTS:
        parts = steps // N_EXPERTS
        assert steps == parts * N_EXPERTS
        w_spec = lambda rows, cols: pl.BlockSpec((None, rows // parts, cols), lambda i, z: (i // parts, i % parts, 0))
    else:
        per_step = N_EXPERTS // steps
        assert N_EXPERTS == per_step * steps
        w_spec = lambda rows, cols: pl.BlockSpec((per_step, rows, cols), lambda i, z: (i, 0, 0))
    w_specs = [w_spec(D_MODEL, D_EXPERT), w_spec(D_MODEL, D_EXPERT), w_spec(D_EXPERT, D_MODEL)]
    grid_spec = pltpu.PrefetchScalarGridSpec(
        num_scalar_prefetch=1,
        grid=(steps,),
        in_specs=[
            pl.BlockSpec((1, 1, ts), lambda i, z: (i, 0, 0), memory_space=pltpu.SMEM),
            pl.BlockSpec((1, 1, ts), lambda i, z: (i, 0, 0), memory_space=pltpu.SMEM),
            pl.BlockSpec((ts * PACK_ROWS, LANES), lambda i, z: (i, 0)),
        ] + w_specs,
        out_specs=[pl.BlockSpec(memory_space=pl.ANY)] + w_specs,
        scratch_shapes=[pltpu.VMEM((EXPERT_BLOCK * PACK_ROWS, LANES), U32), pltpu.SemaphoreType.DMA((3,))],
    )
    return pl.pallas_call(
        _dispatch_kernel,
        grid_spec=grid_spec,
        out_shape=[jax.ShapeDtypeStruct((n_rows * PACK_ROWS, LANES), U32)]
        + [jax.ShapeDtypeStruct(w.shape, BF16) for w in (w_gate, w_up, w_down)],
        compiler_params=_params("arbitrary"),
        name="dispatch",
    )(zero_blocks, dest0, dest1, hn, w_gate, w_up, w_down)


def _experts_kernel(be_ref, bv_ref, nu_ref, xb_ref, wg_ref, wu_ref, wd_ref, yb_ref):
    valid = bv_ref[pl.program_id(0)]
    half = EXPERT_BLOCK // 2

    def mlp(rows):
        pairs = [_unpack_bf16_pair(xb_ref[pl.ds(j, rows, stride=PACK_ROWS), :]) for j in range(PACK_ROWS)]
        x = jnp.concatenate([lo for lo, _ in pairs] + [hi for _, hi in pairs], axis=1)
        a = (_silu(_dot(x, wg_ref[...])) * _dot(x, wu_ref[...])).astype(BF16)
        y = _dot(a, wd_ref[...])
        half_d = D_MODEL // 2
        packed = _pack_bf16_pair(y[:, :half_d], y[:, half_d:])
        for j in range(PACK_ROWS):
            yb_ref[pl.ds(j, rows, stride=PACK_ROWS), :] = packed[:, j * LANES:(j + 1) * LANES]

    @pl.when(valid > half)
    def _():
        mlp(EXPERT_BLOCK)

    @pl.when((valid > 0) & (valid <= half))
    def _():
        mlp(half)
        yb_ref[half * PACK_ROWS:, :] = jnp.zeros((half * PACK_ROWS, LANES), U32)

    @pl.when(valid == 0)
    def _():
        yb_ref[...] = jnp.zeros_like(yb_ref)


def _experts(block_e, block_valid, n_used, xb, w_gate, w_up, w_down):
    n_rows = xb.shape[0] // PACK_ROWS
    n_blocks = n_rows // EXPERT_BLOCK
    xrow = lambda i, be, bv, nu: (jnp.maximum(jnp.minimum(i, nu[0] - 1), 0), 0)
    wsel = lambda i, be, bv, nu: (be[i], 0, 0)
    grid_spec = pltpu.PrefetchScalarGridSpec(
        num_scalar_prefetch=3,
        grid=(n_blocks,),
        in_specs=[
            pl.BlockSpec((EXPERT_BLOCK * PACK_ROWS, LANES), xrow),
            pl.BlockSpec((None, D_MODEL, D_EXPERT), wsel),
            pl.BlockSpec((None, D_MODEL, D_EXPERT), wsel),
            pl.BlockSpec((None, D_EXPERT, D_MODEL), wsel),
        ],
        out_specs=pl.BlockSpec((EXPERT_BLOCK * PACK_ROWS, LANES), lambda i, be, bv, nu: (i, 0)),
    )
    return pl.pallas_call(
        _experts_kernel,
        grid_spec=grid_spec,
        out_shape=jax.ShapeDtypeStruct((n_rows * PACK_ROWS, LANES), U32),
        compiler_params=_params("arbitrary"),
        name="experts",
    )(block_e, block_valid, n_used, xb, w_gate, w_up, w_down)


def _combine_kernel(d0_ref, d1_ref, d0a_ref, d1a_ref, d0b_ref, d1b_ref, x1_ref, p_ref, comb_ref, gpl_ref, wplg_ref,
                    wplp_ref, gfin_ref, yb_ref, out_ref, gbuf, x3_ref, sem):
    tf = x1_ref.shape[0]
    i = pl.program_id(0)
    n_steps = pl.num_programs(0)
    slot = lax.rem(i, GATHER_BUFS)
    slot_a = lax.rem(i + 1, GATHER_BUFS)
    slot_b = lax.rem(i + 2, GATHER_BUFS)

    def start_gather(d_refs, s, t):
        row = pl.multiple_of(t * PACK_ROWS, PACK_ROWS)
        for k, d_ref in enumerate(d_refs):
            src = pl.multiple_of(d_ref[0, 0, t] * PACK_ROWS, PACK_ROWS)
            pltpu.make_async_copy(yb_ref.at[pl.ds(src, PACK_ROWS), :], gbuf.at[s, k, pl.ds(row, PACK_ROWS), :],
                                  sem.at[s, k]).start(priority=k)

    def wait_gathers(s):
        for k in range(2):
            pltpu.make_async_copy(yb_ref.at[pl.ds(0, tf * PACK_ROWS), :], gbuf.at[s, k], sem.at[s, k]).wait()

    @pl.when(i == 0)
    def _():
        lax.fori_loop(0, tf, lambda t, c: (start_gather((d0_ref, d1_ref), 0, t), c)[1], 0, unroll=DMA_UNROLL)
        lax.fori_loop(0, tf, lambda t, c: (start_gather((d0a_ref, d1a_ref), 1, t), c)[1], 0, unroll=DMA_UNROLL)

    wait_gathers(slot)

    def gathered(k):
        words = [gbuf[slot, k, pl.ds(j, tf, stride=PACK_ROWS), :] for j in range(PACK_ROWS)]
        return jnp.concatenate([pltpu.bitcast(w << 16, F32) for w in words]
                               + [pltpu.bitcast(w & jnp.uint32(0xFFFF0000), F32) for w in words], axis=1)

    comb = comb_ref[...]
    y = comb[:, 0:1] * gathered(0) + comb[:, 1:2] * gathered(1)
    x2 = x1_ref[...] + y
    hb = _rms(x2, gpl_ref[...]).astype(BF16)
    pb = p_ref[...].astype(BF16)
    n_col = D_MODEL // COMBINE_COLS
    for c in range(n_col):
        for t in range(c * tf // n_col, (c + 1) * tf // n_col):
            start_gather((d0b_ref, d1b_ref), slot_b, t)
        cs = slice(c * COMBINE_COLS, (c + 1) * COMBINE_COLS)
        gate = jax.nn.sigmoid(_dot(hb, wplg_ref[:, cs]))
        x3_ref[:, cs] = x2[:, cs] + gate * _dot(pb, wplp_ref[:, cs])
    x3 = x3_ref[...]
    out_ref[...] = _rms(x3, gfin_ref[...])

    @pl.when(i == n_steps - 1)
    def _():
        wait_gathers(slot_a)
        wait_gathers(slot_b)


def _combine(dest0, dest1, x1, p2, comb, g_pl, w_plg, w_plp, g_final, yb, tf):
    n = x1.shape[0]
    n_steps = n // tf
    assert n_steps >= GATHER_BUFS - 1
    const = lambda i: (0, 0)
    rows = lambda i: (i, 0)
    ahead = lambda a: (lambda i: (jnp.minimum(i + a, n_steps - 1), 0, 0))
    dest_specs = [pl.BlockSpec((1, 1, tf), ahead(a), memory_space=pltpu.SMEM) for a in range(GATHER_BUFS)
                  for _ in range(2)]
    return pl.pallas_call(
        _combine_kernel,
        grid=(n_steps,),
        in_specs=dest_specs + [
            pl.BlockSpec((tf, D_MODEL), rows),
            pl.BlockSpec((tf, PLE_DIM), rows),
            pl.BlockSpec((tf, LANES), rows),
            pl.BlockSpec((1, D_MODEL), const),
            pl.BlockSpec((D_MODEL, D_MODEL), const),
            pl.BlockSpec((PLE_DIM, D_MODEL), const),
            pl.BlockSpec((1, D_MODEL), const),
            pl.BlockSpec(memory_space=pl.ANY),
        ],
        out_specs=pl.BlockSpec((tf, D_MODEL), rows),
        out_shape=jax.ShapeDtypeStruct((n, D_MODEL), F32),
        scratch_shapes=[pltpu.VMEM((GATHER_BUFS, 2, tf * PACK_ROWS, LANES), U32), pltpu.VMEM((tf, D_MODEL), F32),
                        pltpu.SemaphoreType.DMA((GATHER_BUFS, 2))],
        compiler_params=_params("arbitrary"),
        name="combine",
    )(dest0, dest1, dest0, dest1, dest0, dest1, x1, p2, comb, g_pl, w_plg, w_plp, g_final, yb)


def _layer(x2, p2, batch, seq, g_mix, w_in, b_mgate, conv_qk, g_mlstm, lb, g_hgrn, w_out, g_ffn,
           w_rg, b_rg, w_re, b_re, w_e_gate, w_e_up, w_e_down, g_pl, w_pl_gate, w_pl_proj, g_out):
    n = x2.shape[0]
    n_chunks = seq // CHUNK
    m_cols = 4 * W_MIX
    n_gate = 2 * N_HEADS

    w_in_b = w_in.astype(BF16)
    w_m = w_in_b[:, :m_cols]
    w_gcols = w_in_b[:, m_cols:m_cols + n_gate]
    w_h = w_in_b[:, m_cols + n_gate:]
    w_g = jnp.pad(w_gcols, ((0, 0), (0, LANES - n_gate)))
    b_g = jnp.pad(b_mgate.astype(F32), (0, LANES - n_gate))[None, :]
    zm, zh, gate, gate_t = _in_proj(x2, g_mix[None, :], w_m, w_h, w_g, b_g, conv_qk, lb[None, :], seq,
                                    tm=min(1024, seq))

    y = _mixer(zm, zh, gate, gate_t, g_mlstm[None, :], g_hgrn[None, :], batch, n_chunks)

    n_logit = N_GROUPS + N_EXPERTS
    w_r = jnp.pad(jnp.concatenate([w_rg, w_re], axis=1), ((0, 0), (0, LANES - n_logit)))
    wr_hi = w_r.astype(BF16)
    w_r2 = jnp.concatenate([wr_hi, (w_r - wr_hi.astype(F32)).astype(BF16)], axis=1)
    b_r = jnp.pad(jnp.concatenate([b_rg, b_re]), (0, LANES - n_logit))[None, :]
    x1, hn, route_t, comb, counts = _post_mix(x2, y, w_out.astype(BF16), g_ffn[None, :], w_r2, b_r,
                                              tm=min(512, n))

    counts = counts[N_GROUPS:n_logit, 0].astype(I32)
    padded = (counts + EXPERT_BLOCK - 1) // EXPERT_BLOCK * EXPERT_BLOCK
    pend = jnp.cumsum(padded)
    pstart = pend - padded
    n_blocks = (2 * n) // EXPERT_BLOCK + N_EXPERTS
    n_rows = n_blocks * EXPERT_BLOCK
    n_used = (pend[-1] // EXPERT_BLOCK).astype(I32)
    block_ids = jnp.arange(n_blocks, dtype=I32)
    block_row = jnp.minimum(block_ids, n_used - 1) * EXPERT_BLOCK
    block_e = jnp.sum((pend[None, :] <= block_row[:, None]).astype(I32), axis=1)
    token_end = jnp.sum(jnp.where(block_e[:, None] == jnp.arange(N_EXPERTS, dtype=I32), pstart + counts, 0), axis=1)
    block_valid = jnp.where(block_ids < n_used, jnp.clip(token_end - block_row, 0, EXPERT_BLOCK), 0).astype(I32)
    expert_id = route_t[0:2].astype(I32)
    hot = expert_id[:, :, None] == jnp.arange(N_EXPERTS, dtype=I32)
    dest = jnp.sum(jnp.where(hot, pstart, 0), axis=-1) + route_t[2:4].astype(I32)

    ts = min(1024, n)
    last_block = jnp.where(counts % EXPERT_BLOCK != 0, pend // EXPERT_BLOCK - 1, -1)
    tail_block = n_used + jnp.arange(N_EXPERTS, dtype=I32)
    zero_blocks = jnp.concatenate([last_block, jnp.where(tail_block < n_blocks, tail_block, -1)]).astype(I32)
    xb, wg_b, wu_b, wd_b = _dispatch(zero_blocks, dest[0].reshape(n // ts, 1, ts), dest[1].reshape(n // ts, 1, ts),
                                     hn, w_e_gate, w_e_up, w_e_down, n_rows, ts)
    yb = _experts(block_e, block_valid, n_used[None], xb, wg_b, wu_b, wd_b)
    tf = min(512, n)
    return _combine(dest[0].reshape(n // tf, 1, tf), dest[1].reshape(n // tf, 1, tf), x1, p2, comb, g_pl[None, :],
                    w_pl_gate.astype(BF16), w_pl_proj.astype(BF16), g_out[None, :], yb, tf)


def kernel(x, p, g_mix, w_in, b_mgate, conv_qk, g_mlstm, hg_lb, g_hgrn, w_out, g_ffn, w_rg, b_rg, w_re, b_re,
           w_e_gate, w_e_up, w_e_down, g_pl, w_pl_gate, w_pl_proj, g_final):
    batch, seq, d = x.shape
    depth = p.shape[0]
    assert depth == 1, "the fused final norm assumes a single layer"
    lower_bounds = jnp.cumsum(jax.nn.softmax(hg_lb.astype(F32), axis=0), axis=0)
    i = 0
    out = _layer(x.reshape(batch * seq, d), p[i].reshape(batch * seq, PLE_DIM), batch, seq,
                 g_mix[i], w_in[i], b_mgate[i], conv_qk[i], g_mlstm[i], lower_bounds[i], g_hgrn[i], w_out[i],
                 g_ffn[i], w_rg[i], b_rg[i], w_re[i], b_re[i], w_e_gate[i], w_e_up[i], w_e_down[i],
                 g_pl[i], w_pl_gate[i], w_pl_proj[i], g_final)
    return out.reshape(batch, seq, d)
```

```python
import functools

import jax
import jax.numpy as jnp
from jax import lax
from jax.experimental import pallas as pl
from jax.experimental.pallas import tpu as pltpu

F32 = jnp.float32
BF16 = jnp.bfloat16
I32 = jnp.int32
U32 = jnp.uint32
EPS = 1e-6
LOG2E = 1.4426950408889634

LANES = 128
D_MODEL = 1024
W_MIX = 512
N_HEADS = 4
HEAD_DIM = 128
N_GROUPS = 4
EXPERTS_PER_GROUP = 8
N_EXPERTS = N_GROUPS * EXPERTS_PER_GROUP
D_EXPERT = 512
PLE_DIM = 256
CONV_WIDTH = 4
CHUNK = 128
EXPERT_BLOCK = 512
PACK_ROWS = D_MODEL // 2 // LANES
ROUTER_ROWS = 48
DMA_UNROLL = 32
COMBINE_COLS = 256
MIX_BATCH = 4
IN_PROJ_COLS = 256
IN_PROJ_AHEAD = 1
GATHER_BUFS = 3
VMEM_LIMIT = 56 * 1024 * 1024


def _dot(a, b):
    return jnp.dot(a, b, preferred_element_type=F32)


def _dot_nt(a, b):
    return lax.dot_general(a, b, (((1,), (1,)), ((), ())), preferred_element_type=F32)


def _dot_tn(a, b):
    return lax.dot_general(a, b, (((0,), (0,)), ((), ())), preferred_element_type=F32)


def _rms(u, g):
    return u * lax.rsqrt(jnp.mean(u * u, axis=-1, keepdims=True) + EPS) * g


def _silu(u):
    return u * jax.nn.sigmoid(u)


def _log_sigmoid(u):
    return jnp.minimum(u, 0.0) - jnp.log1p(jnp.exp(-jnp.abs(u)))


def _split_hi_lo(u):
    hi = u.astype(BF16)
    return hi, (u - hi.astype(F32)).astype(BF16)


def _params(*sem):
    return pltpu.CompilerParams(dimension_semantics=sem, vmem_limit_bytes=VMEM_LIMIT)


def _in_proj_kernel(tiles_per_seq, x_ref, g_ref, wm_ref, wh_ref, wg_ref, bg_ref, conv_ref, lb_ref,
                    zm_ref, zh_ref, gate_ref, gate_t_ref, cbuf):
    tm = x_ref.shape[0]
    W = W_MIX
    CB = IN_PROJ_COLS

    @pl.when(pl.program_id(0) % tiles_per_seq == 0)
    def _():
        cbuf[0:8, :] = jnp.zeros((8, 2 * W), F32)

    h = _rms(x_ref[...], g_ref[...]).astype(BF16)

    def conv_finish(col0, scale):
        def finish(z):
            cs = slice(col0, col0 + CB)
            cbuf[8:8 + tm, cs] = z
            acc = z * conv_ref[CONV_WIDTH - 1:CONV_WIDTH, cs]
            for j in range(CONV_WIDTH - 1):
                acc = acc + cbuf[5 + j:5 + j + tm, cs] * conv_ref[j:j + 1, cs]
            cbuf[0:8, cs] = cbuf[tm:tm + 8, cs]
            zm_ref[:, cs] = (_silu(acc) * scale).astype(BF16)
        return finish

    def store(ref, col0, fn):
        def finish(z):
            ref[:, col0:col0 + CB] = fn(z).astype(BF16)
        return finish

    def forget_finish(col0):
        def finish(z):
            lb = lb_ref[:, col0:col0 + CB]
            zh_ref[:, W + col0:W + col0 + CB] = ((1.0 - lb) * jax.nn.sigmoid(-z)).astype(BF16)
            lf_hi, lf_lo = _split_hi_lo(jnp.log2(lb + (1.0 - lb) * jax.nn.sigmoid(z)))
            zh_ref[:, 4 * W + col0:4 * W + col0 + CB] = lf_hi
            zh_ref[:, 5 * W + col0:5 * W + col0 + CB] = lf_lo
        return finish

    ident = lambda z: z
    plan = []
    for half in range(W // CB):
        c0 = half * CB
        plan += [
            (wm_ref, c0, conv_finish(c0, 1.0)),
            (wh_ref, 2 * W + c0, store(zh_ref, 2 * W + c0, ident)),
            (wm_ref, W + c0, conv_finish(W + c0, HEAD_DIM ** -0.5)),
            (wm_ref, 2 * W + c0, store(zm_ref, 2 * W + c0, ident)),
            (wh_ref, W + c0, forget_finish(c0)),
            (wm_ref, 3 * W + c0, store(zm_ref, 3 * W + c0, jax.nn.sigmoid)),
            (wh_ref, c0, store(zh_ref, c0, _silu)),
            (wh_ref, 3 * W + c0, store(zh_ref, 3 * W + c0, _silu)),
        ]

    project = lambda i: _dot(h, plan[i][0][:, plan[i][1]:plan[i][1] + CB])
    zs = [project(i) for i in range(IN_PROJ_AHEAD)]
    for i in range(len(plan)):
        if i + IN_PROJ_AHEAD < len(plan):
            zs.append(project(i + IN_PROJ_AHEAD))
        plan[i][2](zs[i])

    gate = _dot(h, wg_ref[...]) + bg_ref[...]
    gate_ref[...] = gate
    gate_t_ref[...] = jnp.concatenate([gate[c * LANES:(c + 1) * LANES, :].T for c in range(tm // LANES)],
                                      axis=1)[0:2 * N_HEADS, :]


def _in_proj(x2, g_mix, w_m, w_h, w_g, b_g, conv_w, lb, seq, tm):
    n = x2.shape[0]
    assert seq % tm == 0
    const = lambda i: (0, 0)
    rows = lambda i: (i, 0)
    return pl.pallas_call(
        functools.partial(_in_proj_kernel, seq // tm),
        grid=(n // tm,),
        in_specs=[
            pl.BlockSpec((tm, D_MODEL), rows),
            pl.BlockSpec((1, D_MODEL), const),
            pl.BlockSpec((D_MODEL, 4 * W_MIX), const),
            pl.BlockSpec((D_MODEL, 4 * W_MIX), const),
            pl.BlockSpec((D_MODEL, LANES), const),
            pl.BlockSpec((1, LANES), const),
            pl.BlockSpec((CONV_WIDTH, 2 * W_MIX), const),
            pl.BlockSpec((1, W_MIX), const),
        ],
        out_specs=[
            pl.BlockSpec((tm, 4 * W_MIX), rows),
            pl.BlockSpec((tm, 6 * W_MIX), rows),
            pl.BlockSpec((tm, LANES), rows),
            pl.BlockSpec((8, tm), lambda i: (0, i)),
        ],
        out_shape=[
            jax.ShapeDtypeStruct((n, 4 * W_MIX), BF16),
            jax.ShapeDtypeStruct((n, 6 * W_MIX), BF16),
            jax.ShapeDtypeStruct((n, LANES), F32),
            jax.ShapeDtypeStruct((8, n), F32),
        ],
        scratch_shapes=[pltpu.VMEM((tm + 8, 2 * W_MIX), F32)],
        compiler_params=_params("arbitrary"),
        name="in_proj",
    )(x2, g_mix, w_m, w_h, w_g, b_g, conv_w, lb)


def _rows_bcast(ref, rows, span, hs):
    return jnp.concatenate([jnp.broadcast_to(ref[r:r + 1, hs], (span, HEAD_DIM)) for r in rows], axis=0)


def _head_cols(base, h):
    return slice(base * W_MIX + h * HEAD_DIM, base * W_MIX + (h + 1) * HEAD_DIM)


def _lane_block(j):
    return slice(j * LANES, (j + 1) * LANES)


def _mixer_kernel(zm_ref, zh_ref, gate_ref, gate_t_ref, gm_ref, gh_ref, sel_ref, y_ref,
                  c_ref, n_ref, m_ref, st_ref, b_scr):
    L = CHUNK

    @pl.when(pl.program_id(1) == 0)
    def _():
        c_ref[...] = jnp.zeros_like(c_ref)
        n_ref[...] = jnp.zeros_like(n_ref)
        m_ref[...] = jnp.zeros_like(m_ref)
        st_ref[...] = jnp.zeros_like(st_ref)

    units = [(bi, h) for bi in range(MIX_BATCH) for h in range(N_HEADS)]
    row = lax.broadcasted_iota(I32, (L, L), 0)
    col = lax.broadcasted_iota(I32, (L, L), 1)
    causal = col <= row
    lower, upper = causal.astype(BF16), (row <= col).astype(BF16)

    gate_t, cols, bcum_r = [], [], []
    for bi in range(MIX_BATCH):
        g = gate_ref[bi]
        g_t = gate_t_ref[bi]
        lf_c, lf_r = _split_hi_lo(_log_sigmoid(g) * LOG2E), _split_hi_lo(_log_sigmoid(g_t) * LOG2E)
        g = g * LOG2E
        gate_t.append(g_t * LOG2E)
        bcum_c = _dot(lower, lf_c[0]) + _dot(lower, lf_c[1])
        bcum_r.append(_dot(lf_r[0], upper) + _dot(lf_r[1], upper))
        mixed = _split_hi_lo(jnp.where(col < N_HEADS, g, bcum_c))
        cols.append(_dot(mixed[0], sel_ref[...]) + _dot(mixed[1], sel_ref[...]))
        b_scr[bi] = (_dot(lower, zh_ref[bi, :, 4 * W_MIX:5 * W_MIX])
                     + _dot(lower, zh_ref[bi, :, 5 * W_MIX:6 * W_MIX]))

    qk, qc = {}, {}
    for u in units:
        bi, h = u
        qb = zm_ref[bi, :, _head_cols(0, h)]
        qk[u] = _dot_nt(qb, zm_ref[bi, :, _head_cols(1, h)])
        qc[u] = _dot(qb, c_ref[bi, h].astype(BF16))

    oi, blocks = {}, {}
    for u in units:
        bi, h = u
        hs = _head_cols(0, h)
        b = b_scr[bi, :, hs]
        q = zh_ref[bi, :, hs].astype(F32)
        k = zh_ref[bi, :, _head_cols(1, h)].astype(F32)
        oi[u] = _dot_nt((q * jnp.exp2(b)).astype(BF16), st_ref[bi, h].astype(BF16))
        parts = []
        span = L // 2
        while span >= 16:
            mids = range(span, L, 2 * span)
            r = _rows_bcast(b_scr.at[bi], [m for m in mids for _ in (0, 1)], span, hs)
            low = (row & span) != 0
            z = (jnp.where(low, q, k) * jnp.exp2(jnp.where(low, b - r, r - b))).astype(BF16)
            keep = ((row & -(2 * span)) == (col & -(2 * span))) & low & ((col & span) == 0)
            parts.append((keep, _dot_nt(z, z)))
            span //= 2
        r = _rows_bcast(b_scr.at[bi], range(0, L, 16), 16, hs)
        p = _dot_nt((q * jnp.exp2(b - r)).astype(BF16), (k * jnp.exp2(r - b)).astype(BF16))
        parts.append((((row & -16) == (col & -16)) & causal, p))
        blocks[u] = parts

    log_d, inter, row_max, qn = {}, {}, {}, {}
    for u in units:
        bi, h = u
        bc = cols[bi][:, _lane_block(N_HEADS + h)]
        br = bcum_r[bi][N_HEADS + h:N_HEADS + h + 1, :]
        ir = gate_t[bi][h:h + 1, :]
        log_d[u] = jnp.where(causal, bc - br + ir, -jnp.inf)
        inter[u] = bc + m_ref[bi, h:h + 1, :]
        row_max[u] = jnp.max(log_d[u], axis=-1, keepdims=True)
        q = zm_ref[bi, :, _head_cols(0, h)].astype(F32)
        qn[u] = jnp.sum(q * n_ref[bi, h:h + 1, :], axis=-1, keepdims=True)

    sv, row_sum, w_inter, m_t_all = {}, {}, {}, {}
    for u in units:
        bi, h = u
        m_t = jnp.maximum(inter[u], row_max[u])
        s = qk[u] * jnp.exp2(log_d[u] - m_t)
        w_inter[u] = jnp.exp2(inter[u] - m_t)
        m_t_all[u] = m_t
        row_sum[u] = jnp.sum(s, axis=-1, keepdims=True)
        sv[u] = _dot(s.astype(BF16), zm_ref[bi, :, _head_cols(2, h)])

    av, sk = {}, {}
    for u in units:
        bi, h = u
        hs = _head_cols(0, h)
        a = None
        for keep, p in blocks[u]:
            term = jnp.where(keep, p, 0.0)
            a = term if a is None else a + term
        v = zh_ref[bi, :, _head_cols(2, h)]
        av[u] = _dot(a.astype(BF16), v)
        k = zh_ref[bi, :, _head_cols(1, h)].astype(F32)
        ke = (k * jnp.exp2(b_scr[bi, L - 1:L, hs] - b_scr[bi, :, hs])).astype(BF16)
        sk[u] = _dot_tn(v, ke)

    for u in units:
        bi, h = u
        bc = cols[bi][:, _lane_block(N_HEADS + h)]
        ic = cols[bi][:, _lane_block(h)]
        m_prev = m_ref[bi, h:h + 1, :]
        b_last = bc[L - 1:L, :]
        w_log = b_last - bc + ic
        m_new = jnp.maximum(b_last + m_prev, jnp.max(w_log, axis=0, keepdims=True))
        k = zm_ref[bi, :, _head_cols(1, h)].astype(F32)
        kw = k * jnp.exp2(w_log - m_new)
        decay = jnp.exp2(b_last + m_prev - m_new)
        c_ref[bi, h] = decay * c_ref[bi, h] + _dot_tn(kw.astype(BF16), zm_ref[bi, :, _head_cols(2, h)])
        n_ref[bi, h:h + 1, :] = decay * n_ref[bi, h:h + 1, :] + jnp.sum(kw, axis=0, keepdims=True)
        m_ref[bi, h:h + 1, :] = m_new

    hh_all, o_all, hh_ms, o_ms = {}, {}, {}, {}
    for u in units:
        bi, h = u
        hs = _head_cols(0, h)
        den = row_sum[u] + w_inter[u] * qn[u]
        num = sv[u] + w_inter[u] * qc[u]
        hh = num * (1.0 / jnp.maximum(jnp.abs(den), jnp.exp2(-m_t_all[u])))
        hh = hh * zm_ref[bi, :, _head_cols(3, h)].astype(F32)
        hh_all[u] = hh
        hh_ms[u] = jnp.mean(hh * hh, axis=-1, keepdims=True)

        st_ref[bi, h] = jnp.exp2(b_scr[bi, L - 1:L, hs]) * st_ref[bi, h] + sk[u]
        o = oi[u] + av[u]
        o_all[u] = o
        o_ms[u] = jnp.mean(o * o, axis=-1, keepdims=True)

    for u in units:
        bi, h = u
        hs = _head_cols(0, h)
        y_ref[bi, :, hs] = (hh_all[u] * lax.rsqrt(hh_ms[u] + EPS) * gm_ref[:, hs]).astype(BF16)
        o = o_all[u] * lax.rsqrt(o_ms[u] + EPS) * gh_ref[...]
        o = o * zh_ref[bi, :, _head_cols(3, h)].astype(F32)
        y_ref[bi, :, _head_cols(1, h)] = o.astype(BF16)


def _mixer(zm, zh, gate, gate_t, g_mlstm, g_hgrn, batch, n_chunks):
    n = zm.shape[0]
    seq = n // batch
    assert batch % MIX_BATCH == 0
    blk = lambda b, c: (b, c, 0)
    const = lambda b, c: (0, 0)
    sel = (jnp.arange(LANES)[:, None] == jnp.arange(2 * N_HEADS * LANES)[None, :] // LANES).astype(BF16)
    return pl.pallas_call(
        _mixer_kernel,
        grid=(batch // MIX_BATCH, n_chunks),
        in_specs=[
            pl.BlockSpec((MIX_BATCH, CHUNK, 4 * W_MIX), blk),
            pl.BlockSpec((MIX_BATCH, CHUNK, 6 * W_MIX), blk),
            pl.BlockSpec((MIX_BATCH, CHUNK, LANES), blk),
            pl.BlockSpec((MIX_BATCH, 8, CHUNK), lambda b, c: (b, 0, c)),
            pl.BlockSpec((1, W_MIX), const),
            pl.BlockSpec((1, HEAD_DIM), const),
            pl.BlockSpec((LANES, 2 * N_HEADS * LANES), const),
        ],
        out_specs=pl.BlockSpec((MIX_BATCH, CHUNK, 2 * W_MIX), blk),
        out_shape=jax.ShapeDtypeStruct((batch, seq, 2 * W_MIX), BF16),
        scratch_shapes=[
            pltpu.VMEM((MIX_BATCH, N_HEADS, HEAD_DIM, HEAD_DIM), F32),
            pltpu.VMEM((MIX_BATCH, 8, HEAD_DIM), F32),
            pltpu.VMEM((MIX_BATCH, 8, LANES), F32),
            pltpu.VMEM((MIX_BATCH, N_HEADS, HEAD_DIM, HEAD_DIM), F32),
            pltpu.VMEM((MIX_BATCH, CHUNK, W_MIX), F32),
        ],
        compiler_params=_params("parallel", "arbitrary"),
        name="mixer",
    )(zm.reshape(batch, seq, 4 * W_MIX), zh.reshape(batch, seq, 6 * W_MIX), gate.reshape(batch, seq, LANES),
      gate_t.reshape(8, batch, seq).transpose(1, 0, 2), g_mlstm, g_hgrn, sel).reshape(n, 2 * W_MIX)


def _pack_bf16_pair(lo, hi):
    lo_bits = pltpu.bitcast(lo.astype(BF16).astype(F32), U32)
    hi_bits = pltpu.bitcast(hi.astype(BF16).astype(F32), U32)
    return (hi_bits & jnp.uint32(0xFFFF0000)) | (lo_bits >> 16)


def _unpack_bf16_pair(w):
    lo = pltpu.bitcast(w << 16, F32).astype(BF16)
    hi = pltpu.bitcast(w & jnp.uint32(0xFFFF0000), F32).astype(BF16)
    return lo, hi


def _post_mix_kernel(x_ref, y_ref, wo_ref, g_ref, wr_ref, br_ref,
                     x1_ref, hn_ref, route_t_ref, comb_ref, count_ref, run_ref):
    tm = x_ref.shape[0]
    n_r = ROUTER_ROWS

    @pl.when(pl.program_id(0) == 0)
    def _():
        run_ref[...] = jnp.zeros_like(run_ref)

    x1 = x_ref[...] + _dot(y_ref[...], wo_ref[...])
    x1_ref[...] = x1
    hn = _rms(x1, g_ref[...])
    half = D_MODEL // 2
    packed = _pack_bf16_pair(hn[:, :half], hn[:, half:])
    for j in range(PACK_ROWS):
        hn_ref[pl.ds(j, tm, stride=PACK_ROWS), :] = packed[:, j * LANES:(j + 1) * LANES]

    both = _dot(hn.astype(BF16), wr_ref[...])
    logits_c = both[:, :LANES] + both[:, LANES:] + br_ref[...]
    logits = jnp.concatenate([logits_c[c * LANES:(c + 1) * LANES, :].T for c in range(tm // LANES)],
                             axis=1)[0:n_r, :]
    rix = lax.broadcasted_iota(I32, (n_r, tm), 0)
    neg = -jnp.inf
    g_l = jnp.where(rix < N_GROUPS, logits, neg)
    g_max = jnp.max(g_l, axis=0, keepdims=True)
    g_sel = jnp.min(jnp.where(g_l == g_max, rix, n_r), axis=0, keepdims=True)
    g_val = 1.0 / jnp.sum(jnp.exp(g_l - g_max), axis=0, keepdims=True)

    e_row = rix - N_GROUPS
    in_group = (e_row >= g_sel * EXPERTS_PER_GROUP) & (e_row < (g_sel + 1) * EXPERTS_PER_GROUP)
    e_l = jnp.where(in_group, logits, neg)
    v1 = jnp.max(e_l, axis=0, keepdims=True)
    i1 = jnp.min(jnp.where(e_l == v1, rix, n_r), axis=0, keepdims=True)
    e_l2 = jnp.where(rix == i1, neg, e_l)
    v2 = jnp.max(e_l2, axis=0, keepdims=True)
    i2 = jnp.min(jnp.where(e_l2 == v2, rix, n_r), axis=0, keepdims=True)
    t = jnp.exp(v2 - v1)
    c1 = g_val / (1.0 + t)
    c2 = g_val * t / (1.0 + t)

    hot1 = rix == i1
    hot2 = rix == i2
    hot = (hot1 | hot2).astype(F32)
    r_i = lax.broadcasted_iota(I32, (tm, tm), 0)
    c_i = lax.broadcasted_iota(I32, (tm, tm), 1)
    before = _dot(hot.astype(BF16), (r_i < c_i).astype(BF16)) + run_ref[:, 0:1]
    rank1 = jnp.sum(jnp.where(hot1, before, 0.0), axis=0, keepdims=True)
    rank2 = jnp.sum(jnp.where(hot2, before, 0.0), axis=0, keepdims=True)
    run_ref[...] = run_ref[...] + jnp.sum(hot, axis=1, keepdims=True)
    count_ref[...] = run_ref[...]

    r8 = lax.broadcasted_iota(I32, (8, tm), 0)
    out = jnp.where(r8 == 0, (i1 - N_GROUPS).astype(F32), 0.0)
    out = jnp.where(r8 == 1, (i2 - N_GROUPS).astype(F32), out)
    out = jnp.where(r8 == 2, rank1, out)
    out = jnp.where(r8 == 3, rank2, out)
    route_t_ref[...] = out

    r128 = lax.broadcasted_iota(I32, (LANES, tm), 0)
    slab = jnp.where(r128 == 0, c1, jnp.where(r128 == 1, c2, 0.0))
    for c in range(tm // LANES):
        comb_ref[c * LANES:(c + 1) * LANES, :] = slab[:, c * LANES:(c + 1) * LANES].T


def _post_mix(x2, y, w_out, g_ffn, w_r2, b_r, tm):
    n = x2.shape[0]
    n_r = ROUTER_ROWS
    const = lambda i: (0, 0)
    rows = lambda i: (i, 0)
    return pl.pallas_call(
        _post_mix_kernel,
        grid=(n // tm,),
        in_specs=[
            pl.BlockSpec((tm, D_MODEL), rows),
            pl.BlockSpec((tm, 2 * W_MIX), rows),
            pl.BlockSpec((2 * W_MIX, D_MODEL), const),
            pl.BlockSpec((1, D_MODEL), const),
            pl.BlockSpec((D_MODEL, 2 * LANES), const),
            pl.BlockSpec((1, LANES), const),
        ],
        out_specs=[
            pl.BlockSpec((tm, D_MODEL), rows),
            pl.BlockSpec((tm * PACK_ROWS, LANES), rows),
            pl.BlockSpec((8, tm), lambda i: (0, i)),
            pl.BlockSpec((tm, LANES), rows),
            pl.BlockSpec((n_r, LANES), const),
        ],
        out_shape=[
            jax.ShapeDtypeStruct((n, D_MODEL), F32),
            jax.ShapeDtypeStruct((n * PACK_ROWS, LANES), U32),
            jax.ShapeDtypeStruct((8, n), F32),
            jax.ShapeDtypeStruct((n, LANES), F32),
            jax.ShapeDtypeStruct((n_r, LANES), F32),
        ],
        scratch_shapes=[pltpu.VMEM((n_r, LANES), F32)],
        compiler_params=_params("arbitrary"),
        name="post_mix",
    )(x2, y, w_out, g_ffn, w_r2, b_r)


def _dispatch_kernel(zero_ref, d0_ref, d1_ref, hn_ref, wg_ref, wu_ref, wd_ref,
                     xb_ref, wgb_ref, wub_ref, wdb_ref, zbuf, sem):
    ts = hn_ref.shape[0] // PACK_ROWS
    block_rows = EXPERT_BLOCK * PACK_ROWS

    @pl.when(pl.program_id(0) == 0)
    def _():
        zbuf[...] = jnp.zeros_like(zbuf)

        def zero_copy(j):
            row0 = pl.multiple_of(jnp.maximum(zero_ref[j], 0) * block_rows, block_rows)
            return pltpu.make_async_copy(zbuf, xb_ref.at[pl.ds(row0, block_rows), :], sem.at[2])

        def start_zero(j, carry):
            @pl.when(zero_ref[j] >= 0)
            def _():
                zero_copy(j).start()
            return carry

        def wait_zero(j, carry):
            @pl.when(zero_ref[j] >= 0)
            def _():
                zero_copy(j).wait()
            return carry

        lax.fori_loop(0, zero_ref.shape[0], start_zero, 0)
        lax.fori_loop(0, zero_ref.shape[0], wait_zero, 0)

    def start(t, carry):
        src = pl.multiple_of(t * PACK_ROWS, PACK_ROWS)
        for k, d_ref in enumerate((d0_ref, d1_ref)):
            dst = pl.multiple_of(d_ref[0, 0, t] * PACK_ROWS, PACK_ROWS)
            pltpu.make_async_copy(hn_ref.at[pl.ds(src, PACK_ROWS), :], xb_ref.at[pl.ds(dst, PACK_ROWS), :],
                                  sem.at[k]).start(priority=k)
        return carry

    lax.fori_loop(0, ts, start, 0, unroll=DMA_UNROLL)
    wgb_ref[...] = wg_ref[...].astype(BF16)
    wub_ref[...] = wu_ref[...].astype(BF16)
    wdb_ref[...] = wd_ref[...].astype(BF16)

    for k in range(2):
        pltpu.make_async_copy(hn_ref, xb_ref.at[pl.ds(0, ts * PACK_ROWS), :], sem.at[k]).wait()


def _dispatch(zero_blocks, dest0, dest1, hn, w_gate, w_up, w_down, n_rows, ts):
    n = hn.shape[0] // PACK_ROWS
    steps = n // ts
    if steps >= N_EXPERTS:
        parts = steps // N_EXPERTS
        assert steps == parts * N_EXPERTS
        w_spec = lambda rows, cols: pl.BlockSpec((None, rows // parts, cols), lambda i, z: (i // parts, i % parts, 0))
    else:
        per_step = N_EXPERTS // steps
        assert N_EXPERTS == per_step * steps
        w_spec = lambda rows, cols: pl.BlockSpec((per_step, rows, cols), lambda i, z: (i, 0, 0))
    w_specs = [w_spec(D_MODEL, D_EXPERT), w_spec(D_MODEL, D_EXPERT), w_spec(D_EXPERT, D_MODEL)]
    grid_spec = pltpu.PrefetchScalarGridSpec(
        num_scalar_prefetch=1,
        grid=(steps,),
        in_specs=[
            pl.BlockSpec((1, 1, ts), lambda i, z: (i, 0, 0), memory_space=pltpu.SMEM),
            pl.BlockSpec((1, 1, ts), lambda i, z: (i, 0, 0), memory_space=pltpu.SMEM),
            pl.BlockSpec((ts * PACK_ROWS, LANES), lambda i, z: (i, 0)),
        ] + w_specs,
        out_specs=[pl.BlockSpec(memory_space=pl.ANY)] + w_specs,
        scratch_shapes=[pltpu.VMEM((EXPERT_BLOCK * PACK_ROWS, LANES), U32), pltpu.SemaphoreType.DMA((3,))],
    )
    return pl.pallas_call(
        _dispatch_kernel,
        grid_spec=grid_spec,
        out_shape=[jax.ShapeDtypeStruct((n_rows * PACK_ROWS, LANES), U32)]
        + [jax.ShapeDtypeStruct(w.shape, BF16) for w in (w_gate, w_up, w_down)],
        compiler_params=_params("arbitrary"),
        name="dispatch",
    )(zero_blocks, dest0, dest1, hn, w_gate, w_up, w_down)


def _experts_kernel(be_ref, bv_ref, nu_ref, xb_ref, wg_ref, wu_ref, wd_ref, yb_ref):
    valid = bv_ref[pl.program_id(0)]
    half = EXPERT_BLOCK // 2

    def mlp(rows):
        pairs = [_unpack_bf16_pair(xb_ref[pl.ds(j, rows, stride=PACK_ROWS), :]) for j in range(PACK_ROWS)]
        x = jnp.concatenate([lo for lo, _ in pairs] + [hi for _, hi in pairs], axis=1)
        a = (_silu(_dot(x, wg_ref[...])) * _dot(x, wu_ref[...])).astype(BF16)
        y = _dot(a, wd_ref[...])
        half_d = D_MODEL // 2
        packed = _pack_bf16_pair(y[:, :half_d], y[:, half_d:])
        for j in range(PACK_ROWS):
            yb_ref[pl.ds(j, rows, stride=PACK_ROWS), :] = packed[:, j * LANES:(j + 1) * LANES]

    @pl.when(valid > half)
    def _():
        mlp(EXPERT_BLOCK)

    @pl.when((valid > 0) & (valid <= half))
    def _():
        mlp(half)
        yb_ref[half * PACK_ROWS:, :] = jnp.zeros((half * PACK_ROWS, LANES), U32)

    @pl.when(valid == 0)
    def _():
        yb_ref[...] = jnp.zeros_like(yb_ref)


def _experts(block_e, block_valid, n_used, xb, w_gate, w_up, w_down):
    n_rows = xb.shape[0] // PACK_ROWS
    n_blocks = n_rows // EXPERT_BLOCK
    xrow = lambda i, be, bv, nu: (jnp.maximum(jnp.minimum(i, nu[0] - 1), 0), 0)
    wsel = lambda i, be, bv, nu: (be[i], 0, 0)
    grid_spec = pltpu.PrefetchScalarGridSpec(
        num_scalar_prefetch=3,
        grid=(n_blocks,),
        in_specs=[
            pl.BlockSpec((EXPERT_BLOCK * PACK_ROWS, LANES), xrow),
            pl.BlockSpec((None, D_MODEL, D_EXPERT), wsel),
            pl.BlockSpec((None, D_MODEL, D_EXPERT), wsel),
            pl.BlockSpec((None, D_EXPERT, D_MODEL), wsel),
        ],
        out_specs=pl.BlockSpec((EXPERT_BLOCK * PACK_ROWS, LANES), lambda i, be, bv, nu: (i, 0)),
    )
    return pl.pallas_call(
        _experts_kernel,
        grid_spec=grid_spec,
        out_shape=jax.ShapeDtypeStruct((n_rows * PACK_ROWS, LANES), U32),
        compiler_params=_params("arbitrary"),
        name="experts",
    )(block_e, block_valid, n_used, xb, w_gate, w_up, w_down)


def _combine_kernel(d0_ref, d1_ref, d0a_ref, d1a_ref, d0b_ref, d1b_ref, x1_ref, p_ref, comb_ref, gpl_ref, wplg_ref,
                    wplp_ref, gfin_ref, yb_ref, out_ref, gbuf, x3_ref, sem):
    tf = x1_ref.shape[0]
    i = pl.program_id(0)
    n_steps = pl.num_programs(0)
    slot = lax.rem(i, GATHER_BUFS)
    slot_a = lax.rem(i + 1, GATHER_BUFS)
    slot_b = lax.rem(i + 2, GATHER_BUFS)

    def start_gather(d_refs, s, t):
        row = pl.multiple_of(t * PACK_ROWS, PACK_ROWS)
        for k, d_ref in enumerate(d_refs):
            src = pl.multiple_of(d_ref[0, 0, t] * PACK_ROWS, PACK_ROWS)
            pltpu.make_async_copy(yb_ref.at[pl.ds(src, PACK_ROWS), :], gbuf.at[s, k, pl.ds(row, PACK_ROWS), :],
                                  sem.at[s, k]).start(priority=k)

    def wait_gathers(s):
        for k in range(2):
            pltpu.make_async_copy(yb_ref.at[pl.ds(0, tf * PACK_ROWS), :], gbuf.at[s, k], sem.at[s, k]).wait()

    @pl.when(i == 0)
    def _():
        lax.fori_loop(0, tf, lambda t, c: (start_gather((d0_ref, d1_ref), 0, t), c)[1], 0, unroll=DMA_UNROLL)
        lax.fori_loop(0, tf, lambda t, c: (start_gather((d0a_ref, d1a_ref), 1, t), c)[1], 0, unroll=DMA_UNROLL)

    wait_gathers(slot)

    def gathered(k):
        words = [gbuf[slot, k, pl.ds(j, tf, stride=PACK_ROWS), :] for j in range(PACK_ROWS)]
        return jnp.concatenate([pltpu.bitcast(w << 16, F32) for w in words]
                               + [pltpu.bitcast(w & jnp.uint32(0xFFFF0000), F32) for w in words], axis=1)

    comb = comb_ref[...]
    y = comb[:, 0:1] * gathered(0) + comb[:, 1:2] * gathered(1)
    x2 = x1_ref[...] + y
    hb = _rms(x2, gpl_ref[...]).astype(BF16)
    pb = p_ref[...].astype(BF16)
    n_col = D_MODEL // COMBINE_COLS
    for c in range(n_col):
        for t in range(c * tf // n_col, (c + 1) * tf // n_col):
            start_gather((d0b_ref, d1b_ref), slot_b, t)
        cs = slice(c * COMBINE_COLS, (c + 1) * COMBINE_COLS)
        gate = jax.nn.sigmoid(_dot(hb, wplg_ref[:, cs]))
        x3_ref[:, cs] = x2[:, cs] + gate * _dot(pb, wplp_ref[:, cs])
    x3 = x3_ref[...]
    out_ref[...] = _rms(x3, gfin_ref[...])

    @pl.when(i == n_steps - 1)
    def _():
        wait_gathers(slot_a)
        wait_gathers(slot_b)


def _combine(dest0, dest1, x1, p2, comb, g_pl, w_plg, w_plp, g_final, yb, tf):
    n = x1.shape[0]
    n_steps = n // tf
    assert n_steps >= GATHER_BUFS - 1
    const = lambda i: (0, 0)
    rows = lambda i: (i, 0)
    ahead = lambda a: (lambda i: (jnp.minimum(i + a, n_steps - 1), 0, 0))
    dest_specs = [pl.BlockSpec((1, 1, tf), ahead(a), memory_space=pltpu.SMEM) for a in range(GATHER_BUFS)
                  for _ in range(2)]
    return pl.pallas_call(
        _combine_kernel,
        grid=(n_steps,),
        in_specs=dest_specs + [
            pl.BlockSpec((tf, D_MODEL), rows),
            pl.BlockSpec((tf, PLE_DIM), rows),
            pl.BlockSpec((tf, LANES), rows),
            pl.BlockSpec((1, D_MODEL), const),
            pl.BlockSpec((D_MODEL, D_MODEL), const),
            pl.BlockSpec((PLE_DIM, D_MODEL), const),
            pl.BlockSpec((1, D_MODEL), const),
            pl.BlockSpec(memory_space=pl.ANY),
        ],
        out_specs=pl.BlockSpec((tf, D_MODEL), rows),
        out_shape=jax.ShapeDtypeStruct((n, D_MODEL), F32),
        scratch_shapes=[pltpu.VMEM((GATHER_BUFS, 2, tf * PACK_ROWS, LANES), U32), pltpu.VMEM((tf, D_MODEL), F32),
                        pltpu.SemaphoreType.DMA((GATHER_BUFS, 2))],
        compiler_params=_params("arbitrary"),
        name="combine",
    )(dest0, dest1, dest0, dest1, dest0, dest1, x1, p2, comb, g_pl, w_plg, w_plp, g_final, yb)


def _layer(x2, p2, batch, seq, g_mix, w_in, b_mgate, conv_qk, g_mlstm, lb, g_hgrn, w_out, g_ffn,
           w_rg, b_rg, w_re, b_re, w_e_gate, w_e_up, w_e_down, g_pl, w_pl_gate, w_pl_proj, g_out):
    n = x2.shape[0]
    n_chunks = seq // CHUNK
    m_cols = 4 * W_MIX
    n_gate = 2 * N_HEADS

    w_in_b = w_in.astype(BF16)
    w_m = w_in_b[:, :m_cols]
    w_gcols = w_in_b[:, m_cols:m_cols + n_gate]
    w_h = w_in_b[:, m_cols + n_gate:]
    w_g = jnp.pad(w_gcols, ((0, 0), (0, LANES - n_gate)))
    b_g = jnp.pad(b_mgate.astype(F32), (0, LANES - n_gate))[None, :]
    zm, zh, gate, gate_t = _in_proj(x2, g_mix[None, :], w_m, w_h, w_g, b_g, conv_qk, lb[None, :], seq,
                                    tm=min(1024, seq))

    y = _mixer(zm, zh, gate, gate_t, g_mlstm[None, :], g_hgrn[None, :], batch, n_chunks)

    n_logit = N_GROUPS + N_EXPERTS
    w_r = jnp.pad(jnp.concatenate([w_rg, w_re], axis=1), ((0, 0), (0, LANES - n_logit)))
    wr_hi = w_r.astype(BF16)
    w_r2 = jnp.concatenate([wr_hi, (w_r - wr_hi.astype(F32)).astype(BF16)], axis=1)
    b_r = jnp.pad(jnp.concatenate([b_rg, b_re]), (0, LANES - n_logit))[None, :]
    x1, hn, route_t, comb, counts = _post_mix(x2, y, w_out.astype(BF16), g_ffn[None, :], w_r2, b_r,
                                              tm=min(512, n))

    counts = counts[N_GROUPS:n_logit, 0].astype(I32)
    padded = (counts + EXPERT_BLOCK - 1) // EXPERT_BLOCK * EXPERT_BLOCK
    pend = jnp.cumsum(padded)
    pstart = pend - padded
    n_blocks = (2 * n) // EXPERT_BLOCK + N_EXPERTS
    n_rows = n_blocks * EXPERT_BLOCK
    n_used = (pend[-1] // EXPERT_BLOCK).astype(I32)
    block_ids = jnp.arange(n_blocks, dtype=I32)
    block_row = jnp.minimum(block_ids, n_used - 1) * EXPERT_BLOCK
    block_e = jnp.sum((pend[None, :] <= block_row[:, None]).astype(I32), axis=1)
    token_end = jnp.sum(jnp.where(block_e[:, None] == jnp.arange(N_EXPERTS, dtype=I32), pstart + counts, 0), axis=1)
    block_valid = jnp.where(block_ids < n_used, jnp.clip(token_end - block_row, 0, EXPERT_BLOCK), 0).astype(I32)
    expert_id = route_t[0:2].astype(I32)
    hot = expert_id[:, :, None] == jnp.arange(N_EXPERTS, dtype=I32)
    dest = jnp.sum(jnp.where(hot, pstart, 0), axis=-1) + route_t[2:4].astype(I32)

    ts = min(1024, n)
    last_block = jnp.where(counts % EXPERT_BLOCK != 0, pend // EXPERT_BLOCK - 1, -1)
    tail_block = n_used + jnp.arange(N_EXPERTS, dtype=I32)
    zero_blocks = jnp.concatenate([last_block, jnp.where(tail_block < n_blocks, tail_block, -1)]).astype(I32)
    xb, wg_b, wu_b, wd_b = _dispatch(zero_blocks, dest[0].reshape(n // ts, 1, ts), dest[1].reshape(n // ts, 1, ts),
                                     hn, w_e_gate, w_e_up, w_e_down, n_rows, ts)
    yb = _experts(block_e, block_valid, n_used[None], xb, wg_b, wu_b, wd_b)
    tf = min(512, n)
    return _combine(dest[0].reshape(n // tf, 1, tf), dest[1].reshape(n // tf, 1, tf), x1, p2, comb, g_pl[None, :],
                    w_pl_gate.astype(BF16), w_pl_proj.astype(BF16), g_out[None, :], yb, tf)


def kernel(x, p, g_mix, w_in, b_mgate, conv_qk, g_mlstm, hg_lb, g_hgrn, w_out, g_ffn, w_rg, b_rg, w_re, b_re,
           w_e_gate, w_e_up, w_e_down, g_pl, w_pl_gate, w_pl_proj, g_final):
    batch, seq, d = x.shape
    depth = p.shape[0]
    assert depth == 1, "the fused final norm assumes a single layer"
    lower_bounds = jnp.cumsum(jax.nn.softmax(hg_lb.astype(F32), axis=0), axis=0)
    i = 0
    out = _layer(x.reshape(batch * seq, d), p[i].reshape(batch * seq, PLE_DIM), batch, seq,
                 g_mix[i], w_in[i], b_mgate[i], conv_qk[i], g_mlstm[i], lower_bounds[i], g_hgrn[i], w_out[i],
                 g_ffn[i], w_rg[i], b_rg[i], w_re[i], b_re[i], w_e_gate[i], w_e_up[i], w_e_down[i],
                 g_pl[i], w_pl_gate[i], w_pl_proj[i], g_final)
    return out.reshape(batch, seq, d)
```

```python
import functools

import jax
import jax.numpy as jnp
from jax import lax
from jax.experimental import pallas as pl
from jax.experimental.pallas import tpu as pltpu

F32 = jnp.float32
BF16 = jnp.bfloat16
I32 = jnp.int32
U32 = jnp.uint32
EPS = 1e-6
LOG2E = 1.4426950408889634

LANES = 128
D_MODEL = 1024
W_MIX = 512
N_HEADS = 4
HEAD_DIM = 128
N_GROUPS = 4
EXPERTS_PER_GROUP = 8
N_EXPERTS = N_GROUPS * EXPERTS_PER_GROUP
D_EXPERT = 512
PLE_DIM = 256
CONV_WIDTH = 4
CHUNK = 128
EXPERT_BLOCK = 512
PACK_ROWS = D_MODEL // 2 // LANES
ROUTER_ROWS = 48
DMA_UNROLL = 32
COMBINE_COLS = 256
MIX_BATCH = 8
IN_PROJ_COLS = 256
IN_PROJ_AHEAD = 1
GATHER_BUFS = 3
VMEM_LIMIT = 56 * 1024 * 1024


def _dot(a, b):
    return jnp.dot(a, b, preferred_element_type=F32)


def _dot_nt(a, b):
    return lax.dot_general(a, b, (((1,), (1,)), ((), ())), preferred_element_type=F32)


def _dot_tn(a, b):
    return lax.dot_general(a, b, (((0,), (0,)), ((), ())), preferred_element_type=F32)


def _rms(u, g):
    return u * lax.rsqrt(jnp.mean(u * u, axis=-1, keepdims=True) + EPS) * g


def _silu(u):
    return u * jax.nn.sigmoid(u)


def _log_sigmoid(u):
    return jnp.minimum(u, 0.0) - jnp.log1p(jnp.exp(-jnp.abs(u)))


def _split_hi_lo(u):
    hi = u.astype(BF16)
    return hi, (u - hi.astype(F32)).astype(BF16)


def _params(*sem):
    return pltpu.CompilerParams(dimension_semantics=sem, vmem_limit_bytes=VMEM_LIMIT)


def _in_proj_kernel(tiles_per_seq, x_ref, g_ref, wm_ref, wh_ref, wg_ref, bg_ref, conv_ref, lb_ref,
                    zm_ref, zh_ref, gate_ref, gate_t_ref, cbuf):
    tm = x_ref.shape[0]
    W = W_MIX
    CB = IN_PROJ_COLS

    @pl.when(pl.program_id(0) % tiles_per_seq == 0)
    def _():
        cbuf[0:8, :] = jnp.zeros((8, 2 * W), F32)

    h = _rms(x_ref[...], g_ref[...]).astype(BF16)

    def conv_finish(col0, scale):
        def finish(z):
            cs = slice(col0, col0 + CB)
            cbuf[8:8 + tm, cs] = z
            acc = z * conv_ref[CONV_WIDTH - 1:CONV_WIDTH, cs]
            for j in range(CONV_WIDTH - 1):
                acc = acc + cbuf[5 + j:5 + j + tm, cs] * conv_ref[j:j + 1, cs]
            cbuf[0:8, cs] = cbuf[tm:tm + 8, cs]
            zm_ref[:, cs] = (_silu(acc) * scale).astype(BF16)
        return finish

    def store(ref, col0, fn):
        def finish(z):
            ref[:, col0:col0 + CB] = fn(z).astype(BF16)
        return finish

    def forget_finish(col0):
        def finish(z):
            lb = lb_ref[:, col0:col0 + CB]
            zh_ref[:, W + col0:W + col0 + CB] = ((1.0 - lb) * jax.nn.sigmoid(-z)).astype(BF16)
            lf_hi, lf_lo = _split_hi_lo(jnp.log2(lb + (1.0 - lb) * jax.nn.sigmoid(z)))
            zh_ref[:, 4 * W + col0:4 * W + col0 + CB] = lf_hi
            zh_ref[:, 5 * W + col0:5 * W + col0 + CB] = lf_lo
        return finish

    ident = lambda z: z
    plan = []
    for half in range(W // CB):
        c0 = half * CB
        plan += [
            (wm_ref, c0, conv_finish(c0, 1.0)),
            (wh_ref, 2 * W + c0, store(zh_ref, 2 * W + c0, ident)),
            (wm_ref, W + c0, conv_finish(W + c0, HEAD_DIM ** -0.5)),
            (wm_ref, 2 * W + c0, store(zm_ref, 2 * W + c0, ident)),
            (wh_ref, W + c0, forget_finish(c0)),
            (wm_ref, 3 * W + c0, store(zm_ref, 3 * W + c0, jax.nn.sigmoid)),
            (wh_ref, c0, store(zh_ref, c0, _silu)),
            (wh_ref, 3 * W + c0, store(zh_ref, 3 * W + c0, _silu)),
        ]

    project = lambda i: _dot(h, plan[i][0][:, plan[i][1]:plan[i][1] + CB])
    zs = [project(i) for i in range(IN_PROJ_AHEAD)]
    for i in range(len(plan)):
        if i + IN_PROJ_AHEAD < len(plan):
            zs.append(project(i + IN_PROJ_AHEAD))
        plan[i][2](zs[i])

    gate = _dot(h, wg_ref[...]) + bg_ref[...]
    gate_ref[...] = gate
    gate_t_ref[...] = jnp.concatenate([gate[c * LANES:(c + 1) * LANES, :].T for c in range(tm // LANES)],
                                      axis=1)[0:2 * N_HEADS, :]


def _in_proj(x2, g_mix, w_m, w_h, w_g, b_g, conv_w, lb, seq, tm):
    n = x2.shape[0]
    assert seq % tm == 0
    const = lambda i: (0, 0)
    rows = lambda i: (i, 0)
    return pl.pallas_call(
        functools.partial(_in_proj_kernel, seq // tm),
        grid=(n // tm,),
        in_specs=[
            pl.BlockSpec((tm, D_MODEL), rows),
            pl.BlockSpec((1, D_MODEL), const),
            pl.BlockSpec((D_MODEL, 4 * W_MIX), const),
            pl.BlockSpec((D_MODEL, 4 * W_MIX), const),
            pl.BlockSpec((D_MODEL, LANES), const),
            pl.BlockSpec((1, LANES), const),
            pl.BlockSpec((CONV_WIDTH, 2 * W_MIX), const),
            pl.BlockSpec((1, W_MIX), const),
        ],
        out_specs=[
            pl.BlockSpec((tm, 4 * W_MIX), rows),
            pl.BlockSpec((tm, 6 * W_MIX), rows),
            pl.BlockSpec((tm, LANES), rows),
            pl.BlockSpec((8, tm), lambda i: (0, i)),
        ],
        out_shape=[
            jax.ShapeDtypeStruct((n, 4 * W_MIX), BF16),
            jax.ShapeDtypeStruct((n, 6 * W_MIX), BF16),
            jax.ShapeDtypeStruct((n, LANES), F32),
            jax.ShapeDtypeStruct((8, n), F32),
        ],
        scratch_shapes=[pltpu.VMEM((tm + 8, 2 * W_MIX), F32)],
        compiler_params=_params("arbitrary"),
        name="in_proj",
    )(x2, g_mix, w_m, w_h, w_g, b_g, conv_w, lb)


def _rows_bcast(ref, rows, span, hs):
    return jnp.concatenate([jnp.broadcast_to(ref[r:r + 1, hs], (span, HEAD_DIM)) for r in rows], axis=0)


def _head_cols(base, h):
    return slice(base * W_MIX + h * HEAD_DIM, base * W_MIX + (h + 1) * HEAD_DIM)


def _lane_block(j):
    return slice(j * LANES, (j + 1) * LANES)


def _mixer_kernel(zm_ref, zh_ref, gate_ref, gate_t_ref, gm_ref, gh_ref, sel_ref, y_ref,
                  c_ref, n_ref, m_ref, st_ref, b_scr):
    L = CHUNK

    @pl.when(pl.program_id(1) == 0)
    def _():
        c_ref[...] = jnp.zeros_like(c_ref)
        n_ref[...] = jnp.zeros_like(n_ref)
        m_ref[...] = jnp.zeros_like(m_ref)
        st_ref[...] = jnp.zeros_like(st_ref)

    units = [(bi, h) for bi in range(MIX_BATCH) for h in range(N_HEADS)]
    row = lax.broadcasted_iota(I32, (L, L), 0)
    col = lax.broadcasted_iota(I32, (L, L), 1)
    causal = col <= row
    lower, upper = causal.astype(BF16), (row <= col).astype(BF16)

    gate_t, cols, bcum_r = [], [], []
    for bi in range(MIX_BATCH):
        g = gate_ref[bi]
        g_t = gate_t_ref[bi]
        lf_c, lf_r = _split_hi_lo(_log_sigmoid(g) * LOG2E), _split_hi_lo(_log_sigmoid(g_t) * LOG2E)
        g = g * LOG2E
        gate_t.append(g_t * LOG2E)
        bcum_c = _dot(lower, lf_c[0]) + _dot(lower, lf_c[1])
        bcum_r.append(_dot(lf_r[0], upper) + _dot(lf_r[1], upper))
        mixed = _split_hi_lo(jnp.where(col < N_HEADS, g, bcum_c))
        cols.append(_dot(mixed[0], sel_ref[...]) + _dot(mixed[1], sel_ref[...]))
        b_scr[bi] = (_dot(lower, zh_ref[bi, :, 4 * W_MIX:5 * W_MIX])
                     + _dot(lower, zh_ref[bi, :, 5 * W_MIX:6 * W_MIX]))

    qk, qc = {}, {}
    for u in units:
        bi, h = u
        qb = zm_ref[bi, :, _head_cols(0, h)]
        qk[u] = _dot_nt(qb, zm_ref[bi, :, _head_cols(1, h)])
        qc[u] = _dot(qb, c_ref[bi, h].astype(BF16))

    oi, blocks = {}, {}
    for u in units:
        bi, h = u
        hs = _head_cols(0, h)
        b = b_scr[bi, :, hs]
        q = zh_ref[bi, :, hs].astype(F32)
        k = zh_ref[bi, :, _head_cols(1, h)].astype(F32)
        oi[u] = _dot_nt((q * jnp.exp2(b)).astype(BF16), st_ref[bi, h].astype(BF16))
        parts = []
        span = L // 2
        while span >= 16:
            mids = range(span, L, 2 * span)
            r = _rows_bcast(b_scr.at[bi], [m for m in mids for _ in (0, 1)], span, hs)
            low = (row & span) != 0
            z = (jnp.where(low, q, k) * jnp.exp2(jnp.where(low, b - r, r - b))).astype(BF16)
            keep = ((row & -(2 * span)) == (col & -(2 * span))) & low & ((col & span) == 0)
            parts.append((keep, _dot_nt(z, z)))
            span //= 2
        r = _rows_bcast(b_scr.at[bi], range(0, L, 16), 16, hs)
        p = _dot_nt((q * jnp.exp2(b - r)).astype(BF16), (k * jnp.exp2(r - b)).astype(BF16))
        parts.append((((row & -16) == (col & -16)) & causal, p))
        blocks[u] = parts

    log_d, inter, row_max, qn = {}, {}, {}, {}
    for u in units:
        bi, h = u
        bc = cols[bi][:, _lane_block(N_HEADS + h)]
        br = bcum_r[bi][N_HEADS + h:N_HEADS + h + 1, :]
        ir = gate_t[bi][h:h + 1, :]
        log_d[u] = jnp.where(causal, bc - br + ir, -jnp.inf)
        inter[u] = bc + m_ref[bi, h:h + 1, :]
        row_max[u] = jnp.max(log_d[u], axis=-1, keepdims=True)
        q = zm_ref[bi, :, _head_cols(0, h)].astype(F32)
        qn[u] = jnp.sum(q * n_ref[bi, h:h + 1, :], axis=-1, keepdims=True)

    sv, row_sum, w_inter, m_t_all = {}, {}, {}, {}
    for u in units:
        bi, h = u
        m_t = jnp.maximum(inter[u], row_max[u])
        s = qk[u] * jnp.exp2(log_d[u] - m_t)
        w_inter[u] = jnp.exp2(inter[u] - m_t)
        m_t_all[u] = m_t
        row_sum[u] = jnp.sum(s, axis=-1, keepdims=True)
        sv[u] = _dot(s.astype(BF16), zm_ref[bi, :, _head_cols(2, h)])

    av, sk = {}, {}
    for u in units:
        bi, h = u
        hs = _head_cols(0, h)
        a = None
        for keep, p in blocks[u]:
            term = jnp.where(keep, p, 0.0)
            a = term if a is None else a + term
        v = zh_ref[bi, :, _head_cols(2, h)]
        av[u] = _dot(a.astype(BF16), v)
        k = zh_ref[bi, :, _head_cols(1, h)].astype(F32)
        ke = (k * jnp.exp2(b_scr[bi, L - 1:L, hs] - b_scr[bi, :, hs])).astype(BF16)
        sk[u] = _dot_tn(v, ke)

    for u in units:
        bi, h = u
        bc = cols[bi][:, _lane_block(N_HEADS + h)]
        ic = cols[bi][:, _lane_block(h)]
        m_prev = m_ref[bi, h:h + 1, :]
        b_last = bc[L - 1:L, :]
        w_log = b_last - bc + ic
        m_new = jnp.maximum(b_last + m_prev, jnp.max(w_log, axis=0, keepdims=True))
        k = zm_ref[bi, :, _head_cols(1, h)].astype(F32)
        kw = k * jnp.exp2(w_log - m_new)
        decay = jnp.exp2(b_last + m_prev - m_new)
        c_ref[bi, h] = decay * c_ref[bi, h] + _dot_tn(kw.astype(BF16), zm_ref[bi, :, _head_cols(2, h)])
        n_ref[bi, h:h + 1, :] = decay * n_ref[bi, h:h + 1, :] + jnp.sum(kw, axis=0, keepdims=True)
        m_ref[bi, h:h + 1, :] = m_new

    hh_all, o_all, hh_ms, o_ms = {}, {}, {}, {}
    for u in units:
        bi, h = u
        hs = _head_cols(0, h)
        den = row_sum[u] + w_inter[u] * qn[u]
        num = sv[u] + w_inter[u] * qc[u]
        hh = num * (1.0 / jnp.maximum(jnp.abs(den), jnp.exp2(-m_t_all[u])))
        hh = hh * zm_ref[bi, :, _head_cols(3, h)].astype(F32)
        hh_all[u] = hh
        hh_ms[u] = jnp.mean(hh * hh, axis=-1, keepdims=True)

        st_ref[bi, h] = jnp.exp2(b_scr[bi, L - 1:L, hs]) * st_ref[bi, h] + sk[u]
        o = oi[u] + av[u]
        o_all[u] = o
        o_ms[u] = jnp.mean(o * o, axis=-1, keepdims=True)

    for u in units:
        bi, h = u
        hs = _head_cols(0, h)
        y_ref[bi, :, hs] = (hh_all[u] * lax.rsqrt(hh_ms[u] + EPS) * gm_ref[:, hs]).astype(BF16)
        o = o_all[u] * lax.rsqrt(o_ms[u] + EPS) * gh_ref[...]
        o = o * zh_ref[bi, :, _head_cols(3, h)].astype(F32)
        y_ref[bi, :, _head_cols(1, h)] = o.astype(BF16)


def _mixer(zm, zh, gate, gate_t, g_mlstm, g_hgrn, batch, n_chunks):
    n = zm.shape[0]
    seq = n // batch
    assert batch % MIX_BATCH == 0
    blk = lambda b, c: (b, c, 0)
    const = lambda b, c: (0, 0)
    sel = (jnp.arange(LANES)[:, None] == jnp.arange(2 * N_HEADS * LANES)[None, :] // LANES).astype(BF16)
    return pl.pallas_call(
        _mixer_kernel,
        grid=(batch // MIX_BATCH, n_chunks),
        in_specs=[
            pl.BlockSpec((MIX_BATCH, CHUNK, 4 * W_MIX), blk),
            pl.BlockSpec((MIX_BATCH, CHUNK, 6 * W_MIX), blk),
            pl.BlockSpec((MIX_BATCH, CHUNK, LANES), blk),
            pl.BlockSpec((MIX_BATCH, 8, CHUNK), lambda b, c: (b, 0, c)),
            pl.BlockSpec((1, W_MIX), const),
            pl.BlockSpec((1, HEAD_DIM), const),
            pl.BlockSpec((LANES, 2 * N_HEADS * LANES), const),
        ],
        out_specs=pl.BlockSpec((MIX_BATCH, CHUNK, 2 * W_MIX), blk),
        out_shape=jax.ShapeDtypeStruct((batch, seq, 2 * W_MIX), BF16),
        scratch_shapes=[
            pltpu.VMEM((MIX_BATCH, N_HEADS, HEAD_DIM, HEAD_DIM), F32),
            pltpu.VMEM((MIX_BATCH, 8, HEAD_DIM), F32),
            pltpu.VMEM((MIX_BATCH, 8, LANES), F32),
            pltpu.VMEM((MIX_BATCH, N_HEADS, HEAD_DIM, HEAD_DIM), F32),
            pltpu.VMEM((MIX_BATCH, CHUNK, W_MIX), F32),
        ],
        compiler_params=_params("parallel", "arbitrary"),
        name="mixer",
    )(zm.reshape(batch, seq, 4 * W_MIX), zh.reshape(batch, seq, 6 * W_MIX), gate.reshape(batch, seq, LANES),
      gate_t.reshape(8, batch, seq).transpose(1, 0, 2), g_mlstm, g_hgrn, sel).reshape(n, 2 * W_MIX)


def _pack_bf16_pair(lo, hi):
    lo_bits = pltpu.bitcast(lo.astype(BF16).astype(F32), U32)
    hi_bits = pltpu.bitcast(hi.astype(BF16).astype(F32), U32)
    return (hi_bits & jnp.uint32(0xFFFF0000)) | (lo_bits >> 16)


def _unpack_bf16_pair(w):
    lo = pltpu.bitcast(w << 16, F32).astype(BF16)
    hi = pltpu.bitcast(w & jnp.uint32(0xFFFF0000), F32).astype(BF16)
    return lo, hi


def _post_mix_kernel(x_ref, y_ref, wo_ref, g_ref, wr_ref, br_ref,
                     x1_ref, hn_ref, route_t_ref, comb_ref, count_ref, run_ref):
    tm = x_ref.shape[0]
    n_r = ROUTER_ROWS

    @pl.when(pl.program_id(0) == 0)
    def _():
        run_ref[...] = jnp.zeros_like(run_ref)

    x1 = x_ref[...] + _dot(y_ref[...], wo_ref[...])
    x1_ref[...] = x1
    hn = _rms(x1, g_ref[...])
    half = D_MODEL // 2
    packed = _pack_bf16_pair(hn[:, :half], hn[:, half:])
    for j in range(PACK_ROWS):
        hn_ref[pl.ds(j, tm, stride=PACK_ROWS), :] = packed[:, j * LANES:(j + 1) * LANES]

    both = _dot(hn.astype(BF16), wr_ref[...])
    logits_c = both[:, :LANES] + both[:, LANES:] + br_ref[...]
    logits = jnp.concatenate([logits_c[c * LANES:(c + 1) * LANES, :].T for c in range(tm // LANES)],
                             axis=1)[0:n_r, :]
    rix = lax.broadcasted_iota(I32, (n_r, tm), 0)
    neg = -jnp.inf
    g_l = jnp.where(rix < N_GROUPS, logits, neg)
    g_max = jnp.max(g_l, axis=0, keepdims=True)
    g_sel = jnp.min(jnp.where(g_l == g_max, rix, n_r), axis=0, keepdims=True)
    g_val = 1.0 / jnp.sum(jnp.exp(g_l - g_max), axis=0, keepdims=True)

    e_row = rix - N_GROUPS
    in_group = (e_row >= g_sel * EXPERTS_PER_GROUP) & (e_row < (g_sel + 1) * EXPERTS_PER_GROUP)
    e_l = jnp.where(in_group, logits, neg)
    v1 = jnp.max(e_l, axis=0, keepdims=True)
    i1 = jnp.min(jnp.where(e_l == v1, rix, n_r), axis=0, keepdims=True)
    e_l2 = jnp.where(rix == i1, neg, e_l)
    v2 = jnp.max(e_l2, axis=0, keepdims=True)
    i2 = jnp.min(jnp.where(e_l2 == v2, rix, n_r), axis=0, keepdims=True)
    t = jnp.exp(v2 - v1)
    c1 = g_val / (1.0 + t)
    c2 = g_val * t / (1.0 + t)

    hot1 = rix == i1
    hot2 = rix == i2
    hot = (hot1 | hot2).astype(F32)
    r_i = lax.broadcasted_iota(I32, (tm, tm), 0)
    c_i = lax.broadcasted_iota(I32, (tm, tm), 1)
    before = _dot(hot.astype(BF16), (r_i < c_i).astype(BF16)) + run_ref[:, 0:1]
    rank1 = jnp.sum(jnp.where(hot1, before, 0.0), axis=0, keepdims=True)
    rank2 = jnp.sum(jnp.where(hot2, before, 0.0), axis=0, keepdims=True)
    run_ref[...] = run_ref[...] + jnp.sum(hot, axis=1, keepdims=True)
    count_ref[...] = run_ref[...]

    r8 = lax.broadcasted_iota(I32, (8, tm), 0)
    out = jnp.where(r8 == 0, (i1 - N_GROUPS).astype(F32), 0.0)
    out = jnp.where(r8 == 1, (i2 - N_GROUPS).astype(F32), out)
    out = jnp.where(r8 == 2, rank1, out)
    out = jnp.where(r8 == 3, rank2, out)
    route_t_ref[...] = out

    r128 = lax.broadcasted_iota(I32, (LANES, tm), 0)
    slab = jnp.where(r128 == 0, c1, jnp.where(r128 == 1, c2, 0.0))
    for c in range(tm // LANES):
        comb_ref[c * LANES:(c + 1) * LANES, :] = slab[:, c * LANES:(c + 1) * LANES].T


def _post_mix(x2, y, w_out, g_ffn, w_r2, b_r, tm):
    n = x2.shape[0]
    n_r = ROUTER_ROWS
    const = lambda i: (0, 0)
    rows = lambda i: (i, 0)
    return pl.pallas_call(
        _post_mix_kernel,
        grid=(n // tm,),
        in_specs=[
            pl.BlockSpec((tm, D_MODEL), rows),
            pl.BlockSpec((tm, 2 * W_MIX), rows),
            pl.BlockSpec((2 * W_MIX, D_MODEL), const),
            pl.BlockSpec((1, D_MODEL), const),
            pl.BlockSpec((D_MODEL, 2 * LANES), const),
            pl.BlockSpec((1, LANES), const),
        ],
        out_specs=[
            pl.BlockSpec((tm, D_MODEL), rows),
            pl.BlockSpec((tm * PACK_ROWS, LANES), rows),
            pl.BlockSpec((8, tm), lambda i: (0, i)),
            pl.BlockSpec((tm, LANES), rows),
            pl.BlockSpec((n_r, LANES), const),
        ],
        out_shape=[
            jax.ShapeDtypeStruct((n, D_MODEL), F32),
            jax.ShapeDtypeStruct((n * PACK_ROWS, LANES), U32),
            jax.ShapeDtypeStruct((8, n), F32),
            jax.ShapeDtypeStruct((n, LANES), F32),
            jax.ShapeDtypeStruct((n_r, LANES), F32),
        ],
        scratch_shapes=[pltpu.VMEM((n_r, LANES), F32)],
        compiler_params=_params("arbitrary"),
        name="post_mix",
    )(x2, y, w_out, g_ffn, w_r2, b_r)


def _dispatch_kernel(zero_ref, d0_ref, d1_ref, hn_ref, wg_ref, wu_ref, wd_ref,
                     xb_ref, wgb_ref, wub_ref, wdb_ref, zbuf, sem):
    ts = hn_ref.shape[0] // PACK_ROWS
    block_rows = EXPERT_BLOCK * PACK_ROWS

    @pl.when(pl.program_id(0) == 0)
    def _():
        zbuf[...] = jnp.zeros_like(zbuf)

        def zero_copy(j):
            row0 = pl.multiple_of(jnp.maximum(zero_ref[j], 0) * block_rows, block_rows)
            return pltpu.make_async_copy(zbuf, xb_ref.at[pl.ds(row0, block_rows), :], sem.at[2])

        def start_zero(j, carry):
            @pl.when(zero_ref[j] >= 0)
            def _():
                zero_copy(j).start()
            return carry

        def wait_zero(j, carry):
            @pl.when(zero_ref[j] >= 0)
            def _():
                zero_copy(j).wait()
            return carry

        lax.fori_loop(0, zero_ref.shape[0], start_zero, 0)
        lax.fori_loop(0, zero_ref.shape[0], wait_zero, 0)

    def start(t, carry):
        src = pl.multiple_of(t * PACK_ROWS, PACK_ROWS)
        for k, d_ref in enumerate((d0_ref, d1_ref)):
            dst = pl.multiple_of(d_ref[0, 0, t] * PACK_ROWS, PACK_ROWS)
            pltpu.make_async_copy(hn_ref.at[pl.ds(src, PACK_ROWS), :], xb_ref.at[pl.ds(dst, PACK_ROWS), :],
                                  sem.at[k]).start(priority=k)
        return carry

    lax.fori_loop(0, ts, start, 0, unroll=DMA_UNROLL)
    wgb_ref[...] = wg_ref[...].astype(BF16)
    wub_ref[...] = wu_ref[...].astype(BF16)
    wdb_ref[...] = wd_ref[...].astype(BF16)

    for k in range(2):
        pltpu.make_async_copy(hn_ref, xb_ref.at[pl.ds(0, ts * PACK_ROWS), :], sem.at[k]).wait()


def _dispatch(zero_blocks, dest0, dest1, hn, w_gate, w_up, w_down, n_rows, ts):
    n = hn.shape[0] // PACK_ROWS
    steps = n // ts
    if steps >= N_EXPERTS:
        parts = steps // N_EXPERTS
        assert steps == parts * N_EXPERTS
        w_spec = lambda rows, cols: pl.BlockSpec((None, rows // parts, cols), lambda i, z: (i // parts, i % parts, 0))
    else:
        per_step = N_EXPERTS // steps
        assert N_EXPERTS == per_step * steps
        w_spec = lambda rows, cols: pl.BlockSpec((per_step, rows, cols), lambda i, z: (i, 0, 0))
    w_specs = [w_spec(D_MODEL, D_EXPERT), w_spec(D_MODEL, D_EXPERT), w_spec(D_EXPERT, D_MODEL)]
    grid_spec = pltpu.PrefetchScalarGridSpec(
        num_scalar_prefetch=1,
        grid=(steps,),
        in_specs=[
            pl.BlockSpec((1, 1, ts), lambda i, z: (i, 0, 0), memory_space=pltpu.SMEM),
            pl.BlockSpec((1, 1, ts), lambda i, z: (i, 0, 0), memory_space=pltpu.SMEM),
            pl.BlockSpec((ts * PACK_ROWS, LANES), lambda i, z: (i, 0)),
        ] + w_specs,
        out_specs=[pl.BlockSpec(memory_space=pl.ANY)] + w_specs,
        scratch_shapes=[pltpu.VMEM((EXPERT_BLOCK * PACK_ROWS, LANES), U32), pltpu.SemaphoreType.DMA((3,))],
    )
    return pl.pallas_call(
        _dispatch_kernel,
        grid_spec=grid_spec,
        out_shape=[jax.ShapeDtypeStruct((n_rows * PACK_ROWS, LANES), U32)]
        + [jax.ShapeDtypeStruct(w.shape, BF16) for w in (w_gate, w_up, w_down)],
        compiler_params=_params("arbitrary"),
        name="dispatch",
    )(zero_blocks, dest0, dest1, hn, w_gate, w_up, w_down)


def _experts_kernel(be_ref, bv_ref, nu_ref, xb_ref, wg_ref, wu_ref, wd_ref, yb_ref):
    valid = bv_ref[pl.program_id(0)]
    half = EXPERT_BLOCK // 2

    def mlp(rows):
        pairs = [_unpack_bf16_pair(xb_ref[pl.ds(j, rows, stride=PACK_ROWS), :]) for j in range(PACK_ROWS)]
        x = jnp.concatenate([lo for lo, _ in pairs] + [hi for _, hi in pairs], axis=1)
        a = (_silu(_dot(x, wg_ref[...])) * _dot(x, wu_ref[...])).astype(BF16)
        y = _dot(a, wd_ref[...])
        half_d = D_MODEL // 2
        packed = _pack_bf16_pair(y[:, :half_d], y[:, half_d:])
        for j in range(PACK_ROWS):
            yb_ref[pl.ds(j, rows, stride=PACK_ROWS), :] = packed[:, j * LANES:(j + 1) * LANES]

    @pl.when(valid > half)
    def _():
        mlp(EXPERT_BLOCK)

    @pl.when((valid > 0) & (valid <= half))
    def _():
        mlp(half)
        yb_ref[half * PACK_ROWS:, :] = jnp.zeros((half * PACK_ROWS, LANES), U32)

    @pl.when(valid == 0)
    def _():
        yb_ref[...] = jnp.zeros_like(yb_ref)


def _experts(block_e, block_valid, n_used, xb, w_gate, w_up, w_down):
    n_rows = xb.shape[0] // PACK_ROWS
    n_blocks = n_rows // EXPERT_BLOCK
    xrow = lambda i, be, bv, nu: (jnp.maximum(jnp.minimum(i, nu[0] - 1), 0), 0)
    wsel = lambda i, be, bv, nu: (be[i], 0, 0)
    grid_spec = pltpu.PrefetchScalarGridSpec(
        num_scalar_prefetch=3,
        grid=(n_blocks,),
        in_specs=[
            pl.BlockSpec((EXPERT_BLOCK * PACK_ROWS, LANES), xrow),
            pl.BlockSpec((None, D_MODEL, D_EXPERT), wsel),
            pl.BlockSpec((None, D_MODEL, D_EXPERT), wsel),
            pl.BlockSpec((None, D_EXPERT, D_MODEL), wsel),
        ],
        out_specs=pl.BlockSpec((EXPERT_BLOCK * PACK_ROWS, LANES), lambda i, be, bv, nu: (i, 0)),
    )
    return pl.pallas_call(
        _experts_kernel,
        grid_spec=grid_spec,
        out_shape=jax.ShapeDtypeStruct((n_rows * PACK_ROWS, LANES), U32),
        compiler_params=_params("arbitrary"),
        name="experts",
    )(block_e, block_valid, n_used, xb, w_gate, w_up, w_down)


def _combine_kernel(d0_ref, d1_ref, d0a_ref, d1a_ref, d0b_ref, d1b_ref, x1_ref, p_ref, comb_ref, gpl_ref, wplg_ref,
                    wplp_ref, gfin_ref, yb_ref, out_ref, gbuf, x3_ref, sem):
    tf = x1_ref.shape[0]
    i = pl.program_id(0)
    n_steps = pl.num_programs(0)
    slot = lax.rem(i, GATHER_BUFS)
    slot_a = lax.rem(i + 1, GATHER_BUFS)
    slot_b = lax.rem(i + 2, GATHER_BUFS)

    def start_gather(d_refs, s, t):
        row = pl.multiple_of(t * PACK_ROWS, PACK_ROWS)
        for k, d_ref in enumerate(d_refs):
            src = pl.multiple_of(d_ref[0, 0, t] * PACK_ROWS, PACK_ROWS)
            pltpu.make_async_copy(yb_ref.at[pl.ds(src, PACK_ROWS), :], gbuf.at[s, k, pl.ds(row, PACK_ROWS), :],
                                  sem.at[s, k]).start(priority=k)

    def wait_gathers(s):
        for k in range(2):
            pltpu.make_async_copy(yb_ref.at[pl.ds(0, tf * PACK_ROWS), :], gbuf.at[s, k], sem.at[s, k]).wait()

    @pl.when(i == 0)
    def _():
        lax.fori_loop(0, tf, lambda t, c: (start_gather((d0_ref, d1_ref), 0, t), c)[1], 0, unroll=DMA_UNROLL)
        lax.fori_loop(0, tf, lambda t, c: (start_gather((d0a_ref, d1a_ref), 1, t), c)[1], 0, unroll=DMA_UNROLL)

    wait_gathers(slot)

    def gathered(k):
        words = [gbuf[slot, k, pl.ds(j, tf, stride=PACK_ROWS), :] for j in range(PACK_ROWS)]
        return jnp.concatenate([pltpu.bitcast(w << 16, F32) for w in words]
                               + [pltpu.bitcast(w & jnp.uint32(0xFFFF0000), F32) for w in words], axis=1)

    comb = comb_ref[...]
    y = comb[:, 0:1] * gathered(0) + comb[:, 1:2] * gathered(1)
    x2 = x1_ref[...] + y
    hb = _rms(x2, gpl_ref[...]).astype(BF16)
    pb = p_ref[...].astype(BF16)
    n_col = D_MODEL // COMBINE_COLS
    for c in range(n_col):
        for t in range(c * tf // n_col, (c + 1) * tf // n_col):
            start_gather((d0b_ref, d1b_ref), slot_b, t)
        cs = slice(c * COMBINE_COLS, (c + 1) * COMBINE_COLS)
        gate = jax.nn.sigmoid(_dot(hb, wplg_ref[:, cs]))
        x3_ref[:, cs] = x2[:, cs] + gate * _dot(pb, wplp_ref[:, cs])
    x3 = x3_ref[...]
    out_ref[...] = _rms(x3, gfin_ref[...])

    @pl.when(i == n_steps - 1)
    def _():
        wait_gathers(slot_a)
        wait_gathers(slot_b)


def _combine(dest0, dest1, x1, p2, comb, g_pl, w_plg, w_plp, g_final, yb, tf):
    n = x1.shape[0]
    n_steps = n // tf
    assert n_steps >= GATHER_BUFS - 1
    const = lambda i: (0, 0)
    rows = lambda i: (i, 0)
    ahead = lambda a: (lambda i: (jnp.minimum(i + a, n_steps - 1), 0, 0))
    dest_specs = [pl.BlockSpec((1, 1, tf), ahead(a), memory_space=pltpu.SMEM) for a in range(GATHER_BUFS)
                  for _ in range(2)]
    return pl.pallas_call(
        _combine_kernel,
        grid=(n_steps,),
        in_specs=dest_specs + [
            pl.BlockSpec((tf, D_MODEL), rows),
            pl.BlockSpec((tf, PLE_DIM), rows),
            pl.BlockSpec((tf, LANES), rows),
            pl.BlockSpec((1, D_MODEL), const),
            pl.BlockSpec((D_MODEL, D_MODEL), const),
            pl.BlockSpec((PLE_DIM, D_MODEL), const),
            pl.BlockSpec((1, D_MODEL), const),
            pl.BlockSpec(memory_space=pl.ANY),
        ],
        out_specs=pl.BlockSpec((tf, D_MODEL), rows),
        out_shape=jax.ShapeDtypeStruct((n, D_MODEL), F32),
        scratch_shapes=[pltpu.VMEM((GATHER_BUFS, 2, tf * PACK_ROWS, LANES), U32), pltpu.VMEM((tf, D_MODEL), F32),
                        pltpu.SemaphoreType.DMA((GATHER_BUFS, 2))],
        compiler_params=_params("arbitrary"),
        name="combine",
    )(dest0, dest1, dest0, dest1, dest0, dest1, x1, p2, comb, g_pl, w_plg, w_plp, g_final, yb)


def _layer(x2, p2, batch, seq, g_mix, w_in, b_mgate, conv_qk, g_mlstm, lb, g_hgrn, w_out, g_ffn,
           w_rg, b_rg, w_re, b_re, w_e_gate, w_e_up, w_e_down, g_pl, w_pl_gate, w_pl_proj, g_out):
    n = x2.shape[0]
    n_chunks = seq // CHUNK
    m_cols = 4 * W_MIX
    n_gate = 2 * N_HEADS

    w_in_b = w_in.astype(BF16)
    w_m = w_in_b[:, :m_cols]
    w_gcols = w_in_b[:, m_cols:m_cols + n_gate]
    w_h = w_in_b[:, m_cols + n_gate:]
    w_g = jnp.pad(w_gcols, ((0, 0), (0, LANES - n_gate)))
    b_g = jnp.pad(b_mgate.astype(F32), (0, LANES - n_gate))[None, :]
    zm, zh, gate, gate_t = _in_proj(x2, g_mix[None, :], w_m, w_h, w_g, b_g, conv_qk, lb[None, :], seq,
                                    tm=min(1024, seq))

    y = _mixer(zm, zh, gate, gate_t, g_mlstm[None, :], g_hgrn[None, :], batch, n_chunks)

    n_logit = N_GROUPS + N_EXPERTS
    w_r = jnp.pad(jnp.concatenate([w_rg, w_re], axis=1), ((0, 0), (0, LANES - n_logit)))
    wr_hi = w_r.astype(BF16)
    w_r2 = jnp.concatenate([wr_hi, (w_r - wr_hi.astype(F32)).astype(BF16)], axis=1)
    b_r = jnp.pad(jnp.concatenate([b_rg, b_re]), (0, LANES - n_logit))[None, :]
    x1, hn, route_t, comb, counts = _post_mix(x2, y, w_out.astype(BF16), g_ffn[None, :], w_r2, b_r,
                                              tm=min(512, n))

    counts = counts[N_GROUPS:n_logit, 0].astype(I32)
    padded = (counts + EXPERT_BLOCK - 1) // EXPERT_BLOCK * EXPERT_BLOCK
    pend = jnp.cumsum(padded)
    pstart = pend - padded
    n_blocks = (2 * n) // EXPERT_BLOCK + N_EXPERTS
    n_rows = n_blocks * EXPERT_BLOCK
    n_used = (pend[-1] // EXPERT_BLOCK).astype(I32)
    block_ids = jnp.arange(n_blocks, dtype=I32)
    block_row = jnp.minimum(block_ids, n_used - 1) * EXPERT_BLOCK
    block_e = jnp.sum((pend[None, :] <= block_row[:, None]).astype(I32), axis=1)
    token_end = jnp.sum(jnp.where(block_e[:, None] == jnp.arange(N_EXPERTS, dtype=I32), pstart + counts, 0), axis=1)
    block_valid = jnp.where(block_ids < n_used, jnp.clip(token_end - block_row, 0, EXPERT_BLOCK), 0).astype(I32)
    expert_id = route_t[0:2].astype(I32)
    hot = expert_id[:, :, None] == jnp.arange(N_EXPERTS, dtype=I32)
    dest = jnp.sum(jnp.where(hot, pstart, 0), axis=-1) + route_t[2:4].astype(I32)

    ts = min(1024, n)
    last_block = jnp.where(counts % EXPERT_BLOCK != 0, pend // EXPERT_BLOCK - 1, -1)
    tail_block = n_used + jnp.arange(N_EXPERTS, dtype=I32)
    zero_blocks = jnp.concatenate([last_block, jnp.where(tail_block < n_blocks, tail_block, -1)]).astype(I32)
    xb, wg_b, wu_b, wd_b = _dispatch(zero_blocks, dest[0].reshape(n // ts, 1, ts), dest[1].reshape(n // ts, 1, ts),
                                     hn, w_e_gate, w_e_up, w_e_down, n_rows, ts)
    yb = _experts(block_e, block_valid, n_used[None], xb, wg_b, wu_b, wd_b)
    tf = min(512, n)
    return _combine(dest[0].reshape(n // tf, 1, tf), dest[1].reshape(n // tf, 1, tf), x1, p2, comb, g_pl[None, :],
                    w_pl_gate.astype(BF16), w_pl_proj.astype(BF16), g_out[None, :], yb, tf)


def kernel(x, p, g_mix, w_in, b_mgate, conv_qk, g_mlstm, hg_lb, g_hgrn, w_out, g_ffn, w_rg, b_rg, w_re, b_re,
           w_e_gate, w_e_up, w_e_down, g_pl, w_pl_gate, w_pl_proj, g_final):
    batch, seq, d = x.shape
    depth = p.shape[0]
    assert depth == 1, "the fused final norm assumes a single layer"
    lower_bounds = jnp.cumsum(jax.nn.softmax(hg_lb.astype(F32), axis=0), axis=0)
    i = 0
    out = _layer(x.reshape(batch * seq, d), p[i].reshape(batch * seq, PLE_DIM), batch, seq,
                 g_mix[i], w_in[i], b_mgate[i], conv_qk[i], g_mlstm[i], lower_bounds[i], g_hgrn[i], w_out[i],
                 g_ffn[i], w_rg[i], b_rg[i], w_re[i], b_re[i], w_e_gate[i], w_e_up[i], w_e_down[i],
                 g_pl[i], w_pl_gate[i], w_pl_proj[i], g_final)
    return out.reshape(batch, seq, d)
```

```python
import functools

import jax
import jax.numpy as jnp
from jax import lax
from jax.experimental import pallas as pl
from jax.experimental.pallas import tpu as pltpu

F32 = jnp.float32
BF16 = jnp.bfloat16
I32 = jnp.int32
U32 = jnp.uint32
EPS = 1e-6
LOG2E = 1.4426950408889634

LANES = 128
D_MODEL = 1024
W_MIX = 512
N_HEADS = 4
HEAD_DIM = 128
N_GROUPS = 4
EXPERTS_PER_GROUP = 8
N_EXPERTS = N_GROUPS * EXPERTS_PER_GROUP
D_EXPERT = 512
PLE_DIM = 256
CONV_WIDTH = 4
CHUNK = 128
EXPERT_BLOCK = 512
PACK_ROWS = D_MODEL // 2 // LANES
ROUTER_ROWS = 48
DMA_UNROLL = 32
COMBINE_COLS = 256
MIX_BATCH = 8
IN_PROJ_COLS = 256
IN_PROJ_AHEAD = 1
GATHER_BUFS = 3
VMEM_LIMIT = 56 * 1024 * 1024


def _dot(a, b):
    return jnp.dot(a, b, preferred_element_type=F32)


def _dot_nt(a, b):
    return lax.dot_general(a, b, (((1,), (1,)), ((), ())), preferred_element_type=F32)


def _dot_tn(a, b):
    return lax.dot_general(a, b, (((0,), (0,)), ((), ())), preferred_element_type=F32)


def _rms(u, g):
    return u * lax.rsqrt(jnp.mean(u * u, axis=-1, keepdims=True) + EPS) * g


def _silu(u):
    return u * jax.nn.sigmoid(u)


def _log_sigmoid(u):
    return jnp.minimum(u, 0.0) - jnp.log1p(jnp.exp(-jnp.abs(u)))


def _split_hi_lo(u):
    hi = u.astype(BF16)
    return hi, (u - hi.astype(F32)).astype(BF16)


def _params(*sem):
    return pltpu.CompilerParams(dimension_semantics=sem, vmem_limit_bytes=VMEM_LIMIT)


def _in_proj_kernel(tiles_per_seq, x_ref, g_ref, wm_ref, wh_ref, wg_ref, bg_ref, conv_ref, lb_ref,
                    zm_ref, zh_ref, gate_ref, gate_t_ref, cbuf):
    tm = x_ref.shape[0]
    W = W_MIX
    CB = IN_PROJ_COLS

    @pl.when(pl.program_id(0) % tiles_per_seq == 0)
    def _():
        cbuf[0:8, :] = jnp.zeros((8, 2 * W), F32)

    h = _rms(x_ref[...], g_ref[...]).astype(BF16)

    def conv_finish(col0, scale):
        def finish(z):
            cs = slice(col0, col0 + CB)
            cbuf[8:8 + tm, cs] = z
            acc = z * conv_ref[CONV_WIDTH - 1:CONV_WIDTH, cs]
            for j in range(CONV_WIDTH - 1):
                acc = acc + cbuf[5 + j:5 + j + tm, cs] * conv_ref[j:j + 1, cs]
            cbuf[0:8, cs] = cbuf[tm:tm + 8, cs]
            zm_ref[:, cs] = (_silu(acc) * scale).astype(BF16)
        return finish

    def store(ref, col0, fn):
        def finish(z):
            ref[:, col0:col0 + CB] = fn(z).astype(BF16)
        return finish

    def forget_finish(col0):
        def finish(z):
            lb = lb_ref[:, col0:col0 + CB]
            zh_ref[:, W + col0:W + col0 + CB] = ((1.0 - lb) * jax.nn.sigmoid(-z)).astype(BF16)
            lf_hi, lf_lo = _split_hi_lo(jnp.log2(lb + (1.0 - lb) * jax.nn.sigmoid(z)))
            zh_ref[:, 4 * W + col0:4 * W + col0 + CB] = lf_hi
            zh_ref[:, 5 * W + col0:5 * W + col0 + CB] = lf_lo
        return finish

    ident = lambda z: z
    plan = []
    for half in range(W // CB):
        c0 = half * CB
        plan += [
            (wm_ref, c0, conv_finish(c0, 1.0)),
            (wh_ref, 2 * W + c0, store(zh_ref, 2 * W + c0, ident)),
            (wm_ref, W + c0, conv_finish(W + c0, HEAD_DIM ** -0.5)),
            (wm_ref, 2 * W + c0, store(zm_ref, 2 * W + c0, ident)),
            (wh_ref, W + c0, forget_finish(c0)),
            (wm_ref, 3 * W + c0, store(zm_ref, 3 * W + c0, jax.nn.sigmoid)),
            (wh_ref, c0, store(zh_ref, c0, _silu)),
            (wh_ref, 3 * W + c0, store(zh_ref, 3 * W + c0, _silu)),
        ]

    project = lambda i: _dot(h, plan[i][0][:, plan[i][1]:plan[i][1] + CB])
    zs = [project(i) for i in range(IN_PROJ_AHEAD)]
    for i in range(len(plan)):
        if i + IN_PROJ_AHEAD < len(plan):
            zs.append(project(i + IN_PROJ_AHEAD))
        plan[i][2](zs[i])

    gate = _dot(h, wg_ref[...]) + bg_ref[...]
    gate_ref[...] = gate
    gate_t_ref[...] = jnp.concatenate([gate[c * LANES:(c + 1) * LANES, :].T for c in range(tm // LANES)],
                                      axis=1)[0:2 * N_HEADS, :]


def _in_proj(x2, g_mix, w_m, w_h, w_g, b_g, conv_w, lb, seq, tm):
    n = x2.shape[0]
    assert seq % tm == 0
    tiles_per_seq = seq // tm
    const = lambda i: (0, 0)
    rows = lambda i: (i, 0)
    return pl.pallas_call(
        functools.partial(_in_proj_kernel, tiles_per_seq),
        grid=(n // tm,),
        in_specs=[
            pl.BlockSpec((tm, D_MODEL), rows),
            pl.BlockSpec((1, D_MODEL), const),
            pl.BlockSpec((D_MODEL, 4 * W_MIX), const),
            pl.BlockSpec((D_MODEL, 4 * W_MIX), const),
            pl.BlockSpec((D_MODEL, LANES), const),
            pl.BlockSpec((1, LANES), const),
            pl.BlockSpec((CONV_WIDTH, 2 * W_MIX), const),
            pl.BlockSpec((1, W_MIX), const),
        ],
        out_specs=[
            pl.BlockSpec((tm, 4 * W_MIX), rows),
            pl.BlockSpec((tm, 6 * W_MIX), rows),
            pl.BlockSpec((tm, LANES), rows),
            pl.BlockSpec((None, 2 * N_HEADS, tm), lambda i: (i // tiles_per_seq, 0, i % tiles_per_seq)),
        ],
        out_shape=[
            jax.ShapeDtypeStruct((n, 4 * W_MIX), BF16),
            jax.ShapeDtypeStruct((n, 6 * W_MIX), BF16),
            jax.ShapeDtypeStruct((n, LANES), F32),
            jax.ShapeDtypeStruct((n // seq, 2 * N_HEADS, seq), F32),
        ],
        scratch_shapes=[pltpu.VMEM((tm + 8, 2 * W_MIX), F32)],
        compiler_params=_params("arbitrary"),
        name="in_proj",
    )(x2, g_mix, w_m, w_h, w_g, b_g, conv_w, lb)


def _rows_bcast(ref, rows, span, hs):
    return jnp.concatenate([jnp.broadcast_to(ref[r:r + 1, hs], (span, HEAD_DIM)) for r in rows], axis=0)


def _head_cols(base, h):
    return slice(base * W_MIX + h * HEAD_DIM, base * W_MIX + (h + 1) * HEAD_DIM)


def _lane_block(j):
    return slice(j * LANES, (j + 1) * LANES)


def _mixer_kernel(zm_ref, zh_ref, gate_ref, gate_t_ref, gm_ref, gh_ref, sel_ref, y_ref,
                  c_ref, n_ref, m_ref, st_ref, b_scr):
    L = CHUNK

    @pl.when(pl.program_id(1) == 0)
    def _():
        c_ref[...] = jnp.zeros_like(c_ref)
        n_ref[...] = jnp.zeros_like(n_ref)
        m_ref[...] = jnp.zeros_like(m_ref)
        st_ref[...] = jnp.zeros_like(st_ref)

    units = [(bi, h) for bi in range(MIX_BATCH) for h in range(N_HEADS)]
    row = lax.broadcasted_iota(I32, (L, L), 0)
    col = lax.broadcasted_iota(I32, (L, L), 1)
    causal = col <= row
    lower, upper = causal.astype(BF16), (row <= col).astype(BF16)

    gate_t, cols, bcum_r = [], [], []
    for bi in range(MIX_BATCH):
        g = gate_ref[bi]
        g_t = gate_t_ref[bi]
        lf_c, lf_r = _split_hi_lo(_log_sigmoid(g) * LOG2E), _split_hi_lo(_log_sigmoid(g_t) * LOG2E)
        g = g * LOG2E
        gate_t.append(g_t * LOG2E)
        bcum_c = _dot(lower, lf_c[0]) + _dot(lower, lf_c[1])
        bcum_r.append(_dot(lf_r[0], upper) + _dot(lf_r[1], upper))
        mixed = _split_hi_lo(jnp.where(col < N_HEADS, g, bcum_c))
        cols.append(_dot(mixed[0], sel_ref[...]) + _dot(mixed[1], sel_ref[...]))
        b_scr[bi] = (_dot(lower, zh_ref[bi, :, 4 * W_MIX:5 * W_MIX])
                     + _dot(lower, zh_ref[bi, :, 5 * W_MIX:6 * W_MIX]))

    qk, qc = {}, {}
    for u in units:
        bi, h = u
        qb = zm_ref[bi, :, _head_cols(0, h)]
        qk[u] = _dot_nt(qb, zm_ref[bi, :, _head_cols(1, h)])
        qc[u] = _dot(qb, c_ref[bi, h].astype(BF16))

    oi, blocks = {}, {}
    for u in units:
        bi, h = u
        hs = _head_cols(0, h)
        b = b_scr[bi, :, hs]
        q = zh_ref[bi, :, hs].astype(F32)
        k = zh_ref[bi, :, _head_cols(1, h)].astype(F32)
        oi[u] = _dot_nt((q * jnp.exp2(b)).astype(BF16), st_ref[bi, h].astype(BF16))
        parts = []
        span = L // 2
        while span >= 16:
            mids = range(span, L, 2 * span)
            r = _rows_bcast(b_scr.at[bi], [m for m in mids for _ in (0, 1)], span, hs)
            low = (row & span) != 0
            z = (jnp.where(low, q, k) * jnp.exp2(jnp.where(low, b - r, r - b))).astype(BF16)
            keep = ((row & -(2 * span)) == (col & -(2 * span))) & low & ((col & span) == 0)
            parts.append((keep, _dot_nt(z, z)))
            span //= 2
        r = _rows_bcast(b_scr.at[bi], range(0, L, 16), 16, hs)
        p = _dot_nt((q * jnp.exp2(b - r)).astype(BF16), (k * jnp.exp2(r - b)).astype(BF16))
        parts.append((((row & -16) == (col & -16)) & causal, p))
        blocks[u] = parts

    log_d, inter, row_max, qn = {}, {}, {}, {}
    for u in units:
        bi, h = u
        bc = cols[bi][:, _lane_block(N_HEADS + h)]
        br = bcum_r[bi][N_HEADS + h:N_HEADS + h + 1, :]
        ir = gate_t[bi][h:h + 1, :]
        log_d[u] = jnp.where(causal, bc - br + ir, -jnp.inf)
        inter[u] = bc + m_ref[bi, h:h + 1, :]
        row_max[u] = jnp.max(log_d[u], axis=-1, keepdims=True)
        q = zm_ref[bi, :, _head_cols(0, h)].astype(F32)
        qn[u] = jnp.sum(q * n_ref[bi, h:h + 1, :], axis=-1, keepdims=True)

    sv, row_sum, w_inter, m_t_all = {}, {}, {}, {}
    for u in units:
        bi, h = u
        m_t = jnp.maximum(inter[u], row_max[u])
        s = qk[u] * jnp.exp2(log_d[u] - m_t)
        w_inter[u] = jnp.exp2(inter[u] - m_t)
        m_t_all[u] = m_t
        row_sum[u] = jnp.sum(s, axis=-1, keepdims=True)
        sv[u] = _dot(s.astype(BF16), zm_ref[bi, :, _head_cols(2, h)])

    av, sk = {}, {}
    for u in units:
        bi, h = u
        hs = _head_cols(0, h)
        a = None
        for keep, p in blocks[u]:
            term = jnp.where(keep, p, 0.0)
            a = term if a is None else a + term
        v = zh_ref[bi, :, _head_cols(2, h)]
        av[u] = _dot(a.astype(BF16), v)
        k = zh_ref[bi, :, _head_cols(1, h)].astype(F32)
        ke = (k * jnp.exp2(b_scr[bi, L - 1:L, hs] - b_scr[bi, :, hs])).astype(BF16)
        sk[u] = _dot_tn(v, ke)

    for u in units:
        bi, h = u
        bc = cols[bi][:, _lane_block(N_HEADS + h)]
        ic = cols[bi][:, _lane_block(h)]
        m_prev = m_ref[bi, h:h + 1, :]
        b_last = bc[L - 1:L, :]
        w_log = b_last - bc + ic
        m_new = jnp.maximum(b_last + m_prev, jnp.max(w_log, axis=0, keepdims=True))
        k = zm_ref[bi, :, _head_cols(1, h)].astype(F32)
        kw = k * jnp.exp2(w_log - m_new)
        decay = jnp.exp2(b_last + m_prev - m_new)
        c_ref[bi, h] = decay * c_ref[bi, h] + _dot_tn(kw.astype(BF16), zm_ref[bi, :, _head_cols(2, h)])
        n_ref[bi, h:h + 1, :] = decay * n_ref[bi, h:h + 1, :] + jnp.sum(kw, axis=0, keepdims=True)
        m_ref[bi, h:h + 1, :] = m_new

    hh_all, o_all, hh_ms, o_ms = {}, {}, {}, {}
    for u in units:
        bi, h = u
        hs = _head_cols(0, h)
        den = row_sum[u] + w_inter[u] * qn[u]
        num = sv[u] + w_inter[u] * qc[u]
        hh = num * (1.0 / jnp.maximum(jnp.abs(den), jnp.exp2(-m_t_all[u])))
        hh = hh * zm_ref[bi, :, _head_cols(3, h)].astype(F32)
        hh_all[u] = hh
        hh_ms[u] = jnp.mean(hh * hh, axis=-1, keepdims=True)

        st_ref[bi, h] = jnp.exp2(b_scr[bi, L - 1:L, hs]) * st_ref[bi, h] + sk[u]
        o = oi[u] + av[u]
        o_all[u] = o
        o_ms[u] = jnp.mean(o * o, axis=-1, keepdims=True)

    for u in units:
        bi, h = u
        hs = _head_cols(0, h)
        y_ref[bi, :, hs] = (hh_all[u] * lax.rsqrt(hh_ms[u] + EPS) * gm_ref[:, hs]).astype(BF16)
        o = o_all[u] * lax.rsqrt(o_ms[u] + EPS) * gh_ref[...]
        o = o * zh_ref[bi, :, _head_cols(3, h)].astype(F32)
        y_ref[bi, :, _head_cols(1, h)] = o.astype(BF16)


def _mixer(zm, zh, gate, gate_t, g_mlstm, g_hgrn, batch, n_chunks):
    n = zm.shape[0]
    seq = n // batch
    assert batch % MIX_BATCH == 0
    blk = lambda b, c: (b, c, 0)
    const = lambda b, c: (0, 0)
    sel = (jnp.arange(LANES)[:, None] == jnp.arange(2 * N_HEADS * LANES)[None, :] // LANES).astype(BF16)
    return pl.pallas_call(
        _mixer_kernel,
        grid=(batch // MIX_BATCH, n_chunks),
        in_specs=[
            pl.BlockSpec((MIX_BATCH, CHUNK, 4 * W_MIX), blk),
            pl.BlockSpec((MIX_BATCH, CHUNK, 6 * W_MIX), blk),
            pl.BlockSpec((MIX_BATCH, CHUNK, LANES), blk),
            pl.BlockSpec((MIX_BATCH, 8, CHUNK), lambda b, c: (b, 0, c)),
            pl.BlockSpec((1, W_MIX), const),
            pl.BlockSpec((1, HEAD_DIM), const),
            pl.BlockSpec((LANES, 2 * N_HEADS * LANES), const),
        ],
        out_specs=pl.BlockSpec((MIX_BATCH, CHUNK, 2 * W_MIX), blk),
        out_shape=jax.ShapeDtypeStruct((batch, seq, 2 * W_MIX), BF16),
        scratch_shapes=[
            pltpu.VMEM((MIX_BATCH, N_HEADS, HEAD_DIM, HEAD_DIM), F32),
            pltpu.VMEM((MIX_BATCH, 8, HEAD_DIM), F32),
            pltpu.VMEM((MIX_BATCH, 8, LANES), F32),
            pltpu.VMEM((MIX_BATCH, N_HEADS, HEAD_DIM, HEAD_DIM), F32),
            pltpu.VMEM((MIX_BATCH, CHUNK, W_MIX), F32),
        ],
        compiler_params=_params("parallel", "arbitrary"),
        name="mixer",
    )(zm.reshape(batch, seq, 4 * W_MIX), zh.reshape(batch, seq, 6 * W_MIX), gate.reshape(batch, seq, LANES),
      gate_t, g_mlstm, g_hgrn, sel).reshape(n, 2 * W_MIX)


def _pack_bf16_pair(lo, hi):
    lo_bits = pltpu.bitcast(lo.astype(BF16).astype(F32), U32)
    hi_bits = pltpu.bitcast(hi.astype(BF16).astype(F32), U32)
    return (hi_bits & jnp.uint32(0xFFFF0000)) | (lo_bits >> 16)


def _unpack_bf16_pair(w):
    lo = pltpu.bitcast(w << 16, F32).astype(BF16)
    hi = pltpu.bitcast(w & jnp.uint32(0xFFFF0000), F32).astype(BF16)
    return lo, hi


def _post_mix_kernel(x_ref, y_ref, wo_ref, g_ref, wr_ref, br_ref,
                     x1_ref, hn_ref, route_t_ref, comb_ref, count_ref, run_ref):
    tm = x_ref.shape[0]
    n_r = ROUTER_ROWS

    @pl.when(pl.program_id(0) == 0)
    def _():
        run_ref[...] = jnp.zeros_like(run_ref)

    x1 = x_ref[...] + _dot(y_ref[...], wo_ref[...])
    x1_ref[...] = x1
    hn = _rms(x1, g_ref[...])
    half = D_MODEL // 2
    packed = _pack_bf16_pair(hn[:, :half], hn[:, half:])
    for j in range(PACK_ROWS):
        hn_ref[pl.ds(j, tm, stride=PACK_ROWS), :] = packed[:, j * LANES:(j + 1) * LANES]

    both = _dot(hn.astype(BF16), wr_ref[...])
    logits_c = both[:, :LANES] + both[:, LANES:] + br_ref[...]
    logits = jnp.concatenate([logits_c[c * LANES:(c + 1) * LANES, :].T for c in range(tm // LANES)],
                             axis=1)[0:n_r, :]
    rix = lax.broadcasted_iota(I32, (n_r, tm), 0)
    neg = -jnp.inf
    g_l = jnp.where(rix < N_GROUPS, logits, neg)
    g_max = jnp.max(g_l, axis=0, keepdims=True)
    g_sel = jnp.min(jnp.where(g_l == g_max, rix, n_r), axis=0, keepdims=True)
    g_val = 1.0 / jnp.sum(jnp.exp(g_l - g_max), axis=0, keepdims=True)

    e_row = rix - N_GROUPS
    in_group = (e_row >= g_sel * EXPERTS_PER_GROUP) & (e_row < (g_sel + 1) * EXPERTS_PER_GROUP)
    e_l = jnp.where(in_group, logits, neg)
    v1 = jnp.max(e_l, axis=0, keepdims=True)
    i1 = jnp.min(jnp.where(e_l == v1, rix, n_r), axis=0, keepdims=True)
    e_l2 = jnp.where(rix == i1, neg, e_l)
    v2 = jnp.max(e_l2, axis=0, keepdims=True)
    i2 = jnp.min(jnp.where(e_l2 == v2, rix, n_r), axis=0, keepdims=True)
    t = jnp.exp(v2 - v1)
    c1 = g_val / (1.0 + t)
    c2 = g_val * t / (1.0 + t)

    hot1 = rix == i1
    hot2 = rix == i2
    hot = (hot1 | hot2).astype(F32)
    r_i = lax.broadcasted_iota(I32, (tm, tm), 0)
    c_i = lax.broadcasted_iota(I32, (tm, tm), 1)
    before = _dot(hot.astype(BF16), (r_i < c_i).astype(BF16)) + run_ref[:, 0:1]
    rank1 = jnp.sum(jnp.where(hot1, before, 0.0), axis=0, keepdims=True)
    rank2 = jnp.sum(jnp.where(hot2, before, 0.0), axis=0, keepdims=True)
    run_ref[...] = run_ref[...] + jnp.sum(hot, axis=1, keepdims=True)
    count_ref[...] = run_ref[...]

    r8 = lax.broadcasted_iota(I32, (8, tm), 0)
    out = jnp.where(r8 == 0, (i1 - N_GROUPS).astype(F32), 0.0)
    out = jnp.where(r8 == 1, (i2 - N_GROUPS).astype(F32), out)
    out = jnp.where(r8 == 2, rank1, out)
    out = jnp.where(r8 == 3, rank2, out)
    route_t_ref[...] = out

    r128 = lax.broadcasted_iota(I32, (LANES, tm), 0)
    slab = jnp.where(r128 == 0, c1, jnp.where(r128 == 1, c2, 0.0))
    for c in range(tm // LANES):
        comb_ref[c * LANES:(c + 1) * LANES, :] = slab[:, c * LANES:(c + 1) * LANES].T


def _post_mix(x2, y, w_out, g_ffn, w_r2, b_r, tm):
    n = x2.shape[0]
    n_r = ROUTER_ROWS
    const = lambda i: (0, 0)
    rows = lambda i: (i, 0)
    return pl.pallas_call(
        _post_mix_kernel,
        grid=(n // tm,),
        in_specs=[
            pl.BlockSpec((tm, D_MODEL), rows),
            pl.BlockSpec((tm, 2 * W_MIX), rows),
            pl.BlockSpec((2 * W_MIX, D_MODEL), const),
            pl.BlockSpec((1, D_MODEL), const),
            pl.BlockSpec((D_MODEL, 2 * LANES), const),
            pl.BlockSpec((1, LANES), const),
        ],
        out_specs=[
            pl.BlockSpec((tm, D_MODEL), rows),
            pl.BlockSpec((tm * PACK_ROWS, LANES), rows),
            pl.BlockSpec((8, tm), lambda i: (0, i)),
            pl.BlockSpec((tm, LANES), rows),
            pl.BlockSpec((n_r, LANES), const),
        ],
        out_shape=[
            jax.ShapeDtypeStruct((n, D_MODEL), F32),
            jax.ShapeDtypeStruct((n * PACK_ROWS, LANES), U32),
            jax.ShapeDtypeStruct((8, n), F32),
            jax.ShapeDtypeStruct((n, LANES), F32),
            jax.ShapeDtypeStruct((n_r, LANES), F32),
        ],
        scratch_shapes=[pltpu.VMEM((n_r, LANES), F32)],
        compiler_params=_params("arbitrary"),
        name="post_mix",
    )(x2, y, w_out, g_ffn, w_r2, b_r)


def _dispatch_kernel(zero_ref, d0_ref, d1_ref, hn_ref, wg_ref, wu_ref, wd_ref,
                     xb_ref, wgb_ref, wub_ref, wdb_ref, zbuf, sem):
    ts = hn_ref.shape[0] // PACK_ROWS
    block_rows = EXPERT_BLOCK * PACK_ROWS

    @pl.when(pl.program_id(0) == 0)
    def _():
        zbuf[...] = jnp.zeros_like(zbuf)

        def zero_copy(j):
            row0 = pl.multiple_of(jnp.maximum(zero_ref[j], 0) * block_rows, block_rows)
            return pltpu.make_async_copy(zbuf, xb_ref.at[pl.ds(row0, block_rows), :], sem.at[2])

        def start_zero(j, carry):
            @pl.when(zero_ref[j] >= 0)
            def _():
                zero_copy(j).start()
            return carry

        def wait_zero(j, carry):
            @pl.when(zero_ref[j] >= 0)
            def _():
                zero_copy(j).wait()
            return carry

        lax.fori_loop(0, zero_ref.shape[0], start_zero, 0)
        lax.fori_loop(0, zero_ref.shape[0], wait_zero, 0)

    def start(t, carry):
        src = pl.multiple_of(t * PACK_ROWS, PACK_ROWS)
        for k, d_ref in enumerate((d0_ref, d1_ref)):
            dst = pl.multiple_of(d_ref[0, 0, t] * PACK_ROWS, PACK_ROWS)
            pltpu.make_async_copy(hn_ref.at[pl.ds(src, PACK_ROWS), :], xb_ref.at[pl.ds(dst, PACK_ROWS), :],
                                  sem.at[k]).start(priority=k)
        return carry

    lax.fori_loop(0, ts, start, 0, unroll=DMA_UNROLL)
    wgb_ref[...] = wg_ref[...].astype(BF16)
    wub_ref[...] = wu_ref[...].astype(BF16)
    wdb_ref[...] = wd_ref[...].astype(BF16)

    for k in range(2):
        pltpu.make_async_copy(hn_ref, xb_ref.at[pl.ds(0, ts * PACK_ROWS), :], sem.at[k]).wait()


def _dispatch(zero_blocks, dest0, dest1, hn, w_gate, w_up, w_down, n_rows, ts):
    n = hn.shape[0] // PACK_ROWS
    steps = n // ts
    if steps >= N_EXPERTS:
        parts = steps // N_EXPERTS
        assert steps == parts * N_EXPERTS
        w_spec = lambda rows, cols: pl.BlockSpec((None, rows // parts, cols), lambda i, z: (i // parts, i % parts, 0))
    else:
        per_step = N_EXPERTS // steps
        assert N_EXPERTS == per_step * steps
        w_spec = lambda rows, cols: pl.BlockSpec((per_step, rows, cols), lambda i, z: (i, 0, 0))
    w_specs = [w_spec(D_MODEL, D_EXPERT), w_spec(D_MODEL, D_EXPERT), w_spec(D_EXPERT, D_MODEL)]
    grid_spec = pltpu.PrefetchScalarGridSpec(
        num_scalar_prefetch=1,
        grid=(steps,),
        in_specs=[
            pl.BlockSpec((1, 1, ts), lambda i, z: (i, 0, 0), memory_space=pltpu.SMEM),
            pl.BlockSpec((1, 1, ts), lambda i, z: (i, 0, 0), memory_space=pltpu.SMEM),
            pl.BlockSpec((ts * PACK_ROWS, LANES), lambda i, z: (i, 0)),
        ] + w_specs,
        out_specs=[pl.BlockSpec(memory_space=pl.ANY)] + w_specs,
        scratch_shapes=[pltpu.VMEM((EXPERT_BLOCK * PACK_ROWS, LANES), U32), pltpu.SemaphoreType.DMA((3,))],
    )
    return pl.pallas_call(
        _dispatch_kernel,
        grid_spec=grid_spec,
        out_shape=[jax.ShapeDtypeStruct((n_rows * PACK_ROWS, LANES), U32)]
        + [jax.ShapeDtypeStruct(w.shape, BF16) for w in (w_gate, w_up, w_down)],
        compiler_params=_params("arbitrary"),
        name="dispatch",
    )(zero_blocks, dest0, dest1, hn, w_gate, w_up, w_down)


def _experts_kernel(be_ref, bv_ref, nu_ref, xb_ref, wg_ref, wu_ref, wd_ref, yb_ref):
    valid = bv_ref[pl.program_id(0)]
    half = EXPERT_BLOCK // 2

    def mlp(rows):
        pairs = [_unpack_bf16_pair(xb_ref[pl.ds(j, rows, stride=PACK_ROWS), :]) for j in range(PACK_ROWS)]
        x = jnp.concatenate([lo for lo, _ in pairs] + [hi for _, hi in pairs], axis=1)
        a = (_silu(_dot(x, wg_ref[...])) * _dot(x, wu_ref[...])).astype(BF16)
        y = _dot(a, wd_ref[...])
        half_d = D_MODEL // 2
        packed = _pack_bf16_pair(y[:, :half_d], y[:, half_d:])
        for j in range(PACK_ROWS):
            yb_ref[pl.ds(j, rows, stride=PACK_ROWS), :] = packed[:, j * LANES:(j + 1) * LANES]

    @pl.when(valid > half)
    def _():
        mlp(EXPERT_BLOCK)

    @pl.when((valid > 0) & (valid <= half))
    def _():
        mlp(half)
        yb_ref[half * PACK_ROWS:, :] = jnp.zeros((half * PACK_ROWS, LANES), U32)

    @pl.when(valid == 0)
    def _():
        yb_ref[...] = jnp.zeros_like(yb_ref)


def _experts(block_e, block_valid, n_used, xb, w_gate, w_up, w_down):
    n_rows = xb.shape[0] // PACK_ROWS
    n_blocks = n_rows // EXPERT_BLOCK
    xrow = lambda i, be, bv, nu: (jnp.maximum(jnp.minimum(i, nu[0] - 1), 0), 0)
    wsel = lambda i, be, bv, nu: (be[i], 0, 0)
    grid_spec = pltpu.PrefetchScalarGridSpec(
        num_scalar_prefetch=3,
        grid=(n_blocks,),
        in_specs=[
            pl.BlockSpec((EXPERT_BLOCK * PACK_ROWS, LANES), xrow),
            pl.BlockSpec((None, D_MODEL, D_EXPERT), wsel),
            pl.BlockSpec((None, D_MODEL, D_EXPERT), wsel),
            pl.BlockSpec((None, D_EXPERT, D_MODEL), wsel),
        ],
        out_specs=pl.BlockSpec((EXPERT_BLOCK * PACK_ROWS, LANES), lambda i, be, bv, nu: (i, 0)),
    )
    return pl.pallas_call(
        _experts_kernel,
        grid_spec=grid_spec,
        out_shape=jax.ShapeDtypeStruct((n_rows * PACK_ROWS, LANES), U32),
        compiler_params=_params("arbitrary"),
        name="experts",
    )(block_e, block_valid, n_used, xb, w_gate, w_up, w_down)


def _combine_kernel(d0_ref, d1_ref, d0a_ref, d1a_ref, d0b_ref, d1b_ref, x1_ref, p_ref, comb_ref, gpl_ref, wplg_ref,
                    wplp_ref, gfin_ref, yb_ref, out_ref, gbuf, x3_ref, sem):
    tf = x1_ref.shape[0]
    i = pl.program_id(0)
    n_steps = pl.num_programs(0)
    slot = lax.rem(i, GATHER_BUFS)
    slot_a = lax.rem(i + 1, GATHER_BUFS)
    slot_b = lax.rem(i + 2, GATHER_BUFS)

    def start_gather(d_refs, s, t):
        row = pl.multiple_of(t * PACK_ROWS, PACK_ROWS)
        for k, d_ref in enumerate(d_refs):
            src = pl.multiple_of(d_ref[0, 0, t] * PACK_ROWS, PACK_ROWS)
            pltpu.make_async_copy(yb_ref.at[pl.ds(src, PACK_ROWS), :], gbuf.at[s, k, pl.ds(row, PACK_ROWS), :],
                                  sem.at[s, k]).start(priority=k)

    def wait_gathers(s):
        for k in range(2):
            pltpu.make_async_copy(yb_ref.at[pl.ds(0, tf * PACK_ROWS), :], gbuf.at[s, k], sem.at[s, k]).wait()

    @pl.when(i == 0)
    def _():
        lax.fori_loop(0, tf, lambda t, c: (start_gather((d0_ref, d1_ref), 0, t), c)[1], 0, unroll=DMA_UNROLL)
        lax.fori_loop(0, tf, lambda t, c: (start_gather((d0a_ref, d1a_ref), 1, t), c)[1], 0, unroll=DMA_UNROLL)

    wait_gathers(slot)

    def gathered(k):
        words = [gbuf[slot, k, pl.ds(j, tf, stride=PACK_ROWS), :] for j in range(PACK_ROWS)]
        return jnp.concatenate([pltpu.bitcast(w << 16, F32) for w in words]
                               + [pltpu.bitcast(w & jnp.uint32(0xFFFF0000), F32) for w in words], axis=1)

    comb = comb_ref[...]
    y = comb[:, 0:1] * gathered(0) + comb[:, 1:2] * gathered(1)
    x2 = x1_ref[...] + y
    hb = _rms(x2, gpl_ref[...]).astype(BF16)
    pb = p_ref[...].astype(BF16)
    n_col = D_MODEL // COMBINE_COLS
    for c in range(n_col):
        for t in range(c * tf // n_col, (c + 1) * tf // n_col):
            start_gather((d0b_ref, d1b_ref), slot_b, t)
        cs = slice(c * COMBINE_COLS, (c + 1) * COMBINE_COLS)
        gate = jax.nn.sigmoid(_dot(hb, wplg_ref[:, cs]))
        x3_ref[:, cs] = x2[:, cs] + gate * _dot(pb, wplp_ref[:, cs])
    x3 = x3_ref[...]
    out_ref[...] = _rms(x3, gfin_ref[...])

    @pl.when(i == n_steps - 1)
    def _():
        wait_gathers(slot_a)
        wait_gathers(slot_b)


def _combine(dest0, dest1, x1, p2, comb, g_pl, w_plg, w_plp, g_final, yb, tf):
    n = x1.shape[0]
    n_steps = n // tf
    assert n_steps >= GATHER_BUFS - 1
    const = lambda i: (0, 0)
    rows = lambda i: (i, 0)
    ahead = lambda a: (lambda i: (jnp.minimum(i + a, n_steps - 1), 0, 0))
    dest_specs = [pl.BlockSpec((1, 1, tf), ahead(a), memory_space=pltpu.SMEM) for a in range(GATHER_BUFS)
                  for _ in range(2)]
    return pl.pallas_call(
        _combine_kernel,
        grid=(n_steps,),
        in_specs=dest_specs + [
            pl.BlockSpec((tf, D_MODEL), rows),
            pl.BlockSpec((tf, PLE_DIM), rows),
            pl.BlockSpec((tf, LANES), rows),
            pl.BlockSpec((1, D_MODEL), const),
            pl.BlockSpec((D_MODEL, D_MODEL), const),
            pl.BlockSpec((PLE_DIM, D_MODEL), const),
            pl.BlockSpec((1, D_MODEL), const),
            pl.BlockSpec(memory_space=pl.ANY),
        ],
        out_specs=pl.BlockSpec((tf, D_MODEL), rows),
        out_shape=jax.ShapeDtypeStruct((n, D_MODEL), F32),
        scratch_shapes=[pltpu.VMEM((GATHER_BUFS, 2, tf * PACK_ROWS, LANES), U32), pltpu.VMEM((tf, D_MODEL), F32),
                        pltpu.SemaphoreType.DMA((GATHER_BUFS, 2))],
        compiler_params=_params("arbitrary"),
        name="combine",
    )(dest0, dest1, dest0, dest1, dest0, dest1, x1, p2, comb, g_pl, w_plg, w_plp, g_final, yb)


def _layer(x2, p2, batch, seq, g_mix, w_in, b_mgate, conv_qk, g_mlstm, lb, g_hgrn, w_out, g_ffn,
           w_rg, b_rg, w_re, b_re, w_e_gate, w_e_up, w_e_down, g_pl, w_pl_gate, w_pl_proj, g_out):
    n = x2.shape[0]
    n_chunks = seq // CHUNK
    m_cols = 4 * W_MIX
    n_gate = 2 * N_HEADS

    w_in_b = w_in.astype(BF16)
    w_m = w_in_b[:, :m_cols]
    w_gcols = w_in_b[:, m_cols:m_cols + n_gate]
    w_h = w_in_b[:, m_cols + n_gate:]
    w_g = jnp.pad(w_gcols, ((0, 0), (0, LANES - n_gate)))
    b_g = jnp.pad(b_mgate.astype(F32), (0, LANES - n_gate))[None, :]
    zm, zh, gate, gate_t = _in_proj(x2, g_mix[None, :], w_m, w_h, w_g, b_g, conv_qk, lb[None, :], seq,
                                    tm=min(1024, seq))

    y = _mixer(zm, zh, gate, gate_t, g_mlstm[None, :], g_hgrn[None, :], batch, n_chunks)

    n_logit = N_GROUPS + N_EXPERTS
    w_r = jnp.pad(jnp.concatenate([w_rg, w_re], axis=1), ((0, 0), (0, LANES - n_logit)))
    wr_hi = w_r.astype(BF16)
    w_r2 = jnp.concatenate([wr_hi, (w_r - wr_hi.astype(F32)).astype(BF16)], axis=1)
    b_r = jnp.pad(jnp.concatenate([b_rg, b_re]), (0, LANES - n_logit))[None, :]
    x1, hn, route_t, comb, counts = _post_mix(x2, y, w_out.astype(BF16), g_ffn[None, :], w_r2, b_r,
                                              tm=min(1024, n))

    counts = counts[N_GROUPS:n_logit, 0].astype(I32)
    padded = (counts + EXPERT_BLOCK - 1) // EXPERT_BLOCK * EXPERT_BLOCK
    pend = jnp.cumsum(padded)
    pstart = pend - padded
    n_blocks = (2 * n) // EXPERT_BLOCK + N_EXPERTS
    n_rows = n_blocks * EXPERT_BLOCK
    n_used = (pend[-1] // EXPERT_BLOCK).astype(I32)
    block_ids = jnp.arange(n_blocks, dtype=I32)
    block_row = jnp.minimum(block_ids, n_used - 1) * EXPERT_BLOCK
    block_e = jnp.sum((pend[None, :] <= block_row[:, None]).astype(I32), axis=1)
    token_end = jnp.sum(jnp.where(block_e[:, None] == jnp.arange(N_EXPERTS, dtype=I32), pstart + counts, 0), axis=1)
    block_valid = jnp.where(block_ids < n_used, jnp.clip(token_end - block_row, 0, EXPERT_BLOCK), 0).astype(I32)
    expert_id = route_t[0:2].astype(I32)
    hot = expert_id[:, :, None] == jnp.arange(N_EXPERTS, dtype=I32)
    dest = jnp.sum(jnp.where(hot, pstart, 0), axis=-1) + route_t[2:4].astype(I32)

    ts = min(1024, n)
    last_block = jnp.where(counts % EXPERT_BLOCK != 0, pend // EXPERT_BLOCK - 1, -1)
    tail_block = n_used + jnp.arange(N_EXPERTS, dtype=I32)
    zero_blocks = jnp.concatenate([last_block, jnp.where(tail_block < n_blocks, tail_block, -1)]).astype(I32)
    xb, wg_b, wu_b, wd_b = _dispatch(zero_blocks, dest[0].reshape(n // ts, 1, ts), dest[1].reshape(n // ts, 1, ts),
                                     hn, w_e_gate, w_e_up, w_e_down, n_rows, ts)
    yb = _experts(block_e, block_valid, n_used[None], xb, wg_b, wu_b, wd_b)
    tf = min(1024, n)
    return _combine(dest[0].reshape(n // tf, 1, tf), dest[1].reshape(n // tf, 1, tf), x1, p2, comb, g_pl[None, :],
                    w_pl_gate.astype(BF16), w_pl_proj.astype(BF16), g_out[None, :], yb, tf)


def kernel(x, p, g_mix, w_in, b_mgate, conv_qk, g_mlstm, hg_lb, g_hgrn, w_out, g_ffn, w_rg, b_rg, w_re, b_re,
           w_e_gate, w_e_up, w_e_down, g_pl, w_pl_gate, w_pl_proj, g_final):
    batch, seq, d = x.shape
    depth = p.shape[0]
    assert depth == 1, "the fused final norm assumes a single layer"
    lower_bounds = jnp.cumsum(jax.nn.softmax(hg_lb.astype(F32), axis=0), axis=0)
    i = 0
    out = _layer(x.reshape(batch * seq, d), p[i].reshape(batch * seq, PLE_DIM), batch, seq,
                 g_mix[i], w_in[i], b_mgate[i], conv_qk[i], g_mlstm[i], lower_bounds[i], g_hgrn[i], w_out[i],
                 g_ffn[i], w_rg[i], b_rg[i], w_re[i], b_re[i], w_e_gate[i], w_e_up[i], w_e_down[i],
                 g_pl[i], w_pl_gate[i], w_pl_proj[i], g_final)
    return out.reshape(batch, seq, d)
```
